```python
import jax, jax.numpy as jnp
from jax import lax
import numpy as np

D_MODEL = 2048
BATCH = 8
SEQ = 8192
DEPTH = 4

N_MIXERS = 3
CONV_W = 3
SC_DIM = D_MODEL
CHUNK = 128
SG_DIM = D_MODEL
SG_GROUPS = 8
SB_HEADS = 16
SB_HEAD_DIM = D_MODEL // SB_HEADS
Q_BLOCK = 128
D_FF = 5632
EPS = 1e-6
N_A = len(range(0, DEPTH, N_MIXERS))
N_B = len(range(1, DEPTH, N_MIXERS))
N_C = len(range(2, DEPTH, N_MIXERS))

kernel_name = "hybrid_shortconv_sgu_stickbreak_trunk"


def rmsnorm(x, g):
    xf = x.astype(jnp.float32)
    y = xf * lax.rsqrt(jnp.mean(xf * xf, axis=-1, keepdims=True) + EPS)
    return (y * g.astype(jnp.float32)).astype(x.dtype)


def layernorm(x, g, b):
    xf = x.astype(jnp.float32)
    mu = jnp.mean(xf, axis=-1, keepdims=True)
    var = jnp.mean(jnp.square(xf - mu), axis=-1, keepdims=True)
    y = (xf - mu) * lax.rsqrt(var + EPS)
    return (y * g.astype(jnp.float32) + b.astype(jnp.float32)).astype(x.dtype)


def causal_dwconv(x, w, b=None):
    s = x.shape[1]
    xp = jnp.pad(x, ((0, 0), (CONV_W - 1, 0), (0, 0)))
    y = w[0] * xp[:, 0:s]
    for k in range(1, CONV_W):
        y = y + w[k] * xp[:, k:k + s]
    if b is not None:
        y = y + b
    return y


def short_conv_mixer(x, w_in, w_conv, w_out):
    gb, gc, h = jnp.split(x @ w_in, 3, axis=-1)
    y = gb * causal_dwconv(gc * h, w_conv)
    return y @ w_out


def chunked_sgu_mixer(x, w_in, ln_g, ln_b, w_s, b_s, w_out):
    bn, s, _ = x.shape
    h = jax.nn.gelu(x @ w_in, approximate=False)
    u, v = jnp.split(h, 2, axis=-1)
    v = layernorm(v, ln_g, ln_b)
    v = v.reshape(bn, s // CHUNK, CHUNK, SG_GROUPS, SG_DIM // SG_GROUPS)
    mask = jnp.tril(jnp.ones((CHUNK, CHUNK), dtype=bool))
    ws = jnp.where(mask[None], w_s, 0.0).astype(v.dtype)
    mixed = jnp.einsum('gts,bnsgc->bntgc', ws, v) + b_s.T[None, None, :, :, None].astype(v.dtype)
    y = u * mixed.reshape(bn, s, SG_DIM)
    return y @ w_out


def stick_breaking_mixer(x, w_qkv, w_out):
    bn, s, _ = x.shape
    qkv = (x @ w_qkv).reshape(bn, s, 3, SB_HEADS, SB_HEAD_DIM)
    q = qkv[:, :, 0].transpose(0, 2, 1, 3)
    k = qkv[:, :, 1].transpose(0, 2, 1, 3)
    v = qkv[:, :, 2].transpose(0, 2, 1, 3)
    nb = s // Q_BLOCK
    q_blocks = q.reshape(bn, SB_HEADS, nb, Q_BLOCK, SB_HEAD_DIM).transpose(2, 0, 1, 3, 4)
    kpos = jnp.arange(s)
    scale = SB_HEAD_DIM ** -0.5

    def block(args):
        qb, i = args
        qpos = i * Q_BLOCK + jnp.arange(Q_BLOCK)
        mask = kpos[None, :] < qpos[:, None]
        z = jnp.einsum('bhqd,bhkd->bhqk', qb, k).astype(jnp.float32) * scale
        log_1mb = jnp.where(mask, jax.nn.log_sigmoid(-z), 0.0)
        acc = lax.cumsum(log_1mb, axis=3, reverse=True) - log_1mb
        a = jnp.where(mask, jnp.exp(jax.nn.log_sigmoid(z) + acc), 0.0)
        return jnp.einsum('bhqk,bhkd->bhqd', a.astype(v.dtype), v)

    o = lax.map(block, (q_blocks, jnp.arange(nb)))
    o = o.transpose(1, 0, 3, 2, 4).reshape(bn, s, SB_HEADS * SB_HEAD_DIM)
    return o @ w_out


def conv_glu_ffn(x, w_up, conv_w, conv_b, w_down):
    h = causal_dwconv(x @ w_up, conv_w, conv_b)
    g, val = jnp.split(h, 2, axis=-1)
    return (jax.nn.silu(g) * val) @ w_down


def _fwd_setup_inputs(seed: int = 0) -> dict:
    key = jax.random.key(seed)
    ks = jax.random.split(key, 20)
    f32 = jnp.float32

    def w(k, shape, fan_in):
        return jax.random.normal(k, shape, f32) * (fan_in ** -0.5)

    def gain(k, shape):
        return 1.0 + 0.05 * jax.random.normal(k, shape, f32)

    return {
        "x": jax.random.normal(ks[0], (BATCH, SEQ, D_MODEL), f32),
        "norm_mix_pre": gain(ks[1], (DEPTH, D_MODEL)),
        "norm_mix_post": gain(ks[2], (DEPTH, D_MODEL)),
        "norm_ffn_pre": gain(ks[3], (DEPTH, D_MODEL)),
        "norm_ffn_post": gain(ks[4], (DEPTH, D_MODEL)),
        "sc_w_in": w(ks[5], (N_A, D_MODEL, 3 * SC_DIM), D_MODEL),
        "sc_conv_w": w(ks[6], (N_A, CONV_W, SC_DIM), CONV_W),
        "sc_w_out": w(ks[7], (N_A, SC_DIM, D_MODEL), SC_DIM),
        "sg_w_in": w(ks[8], (N_B, D_MODEL, 2 * SG_DIM), D_MODEL),
        "sg_ln_g": gain(ks[9], (N_B, SG_DIM)),
        "sg_ln_b": 0.02 * jax.random.normal(ks[10], (N_B, SG_DIM), f32),
        "sg_w_s": w(ks[11], (N_B, SG_GROUPS, CHUNK, CHUNK), CHUNK),
        "sg_b_s": 1.0 + 0.1 * jax.random.normal(ks[12], (N_B, SG_GROUPS, CHUNK), f32),
        "sg_w_out": w(ks[13], (N_B, SG_DIM, D_MODEL), SG_DIM),
        "sb_w_qkv": w(ks[14], (N_C, D_MODEL, 3 * SB_HEADS * SB_HEAD_DIM), D_MODEL),
        "sb_w_out": w(ks[15], (N_C, SB_HEADS * SB_HEAD_DIM, D_MODEL), SB_HEADS * SB_HEAD_DIM),
        "ffn_w_up": w(ks[16], (DEPTH, D_MODEL, 2 * D_FF), D_MODEL),
        "ffn_conv_w": w(ks[17], (DEPTH, CONV_W, 2 * D_FF), CONV_W),
        "ffn_conv_b": 0.02 * jax.random.normal(ks[18], (DEPTH, 2 * D_FF), f32),
        "ffn_w_down": w(ks[19], (DEPTH, D_FF, D_MODEL), D_FF),
    }


def _fwd_reference(x, norm_mix_pre, norm_mix_post, norm_ffn_pre, norm_ffn_post,
              sc_w_in, sc_conv_w, sc_w_out,
              sg_w_in, sg_ln_g, sg_ln_b, sg_w_s, sg_b_s, sg_w_out,
              sb_w_qkv, sb_w_out,
              ffn_w_up, ffn_conv_w, ffn_conv_b, ffn_w_down):
    h = x
    for i in range(DEPTH):
        kind = i % N_MIXERS
        j = i // N_MIXERS
        hn = rmsnorm(h, norm_mix_pre[i])
        if kind == 0:
            m = short_conv_mixer(hn, sc_w_in[j], sc_conv_w[j], sc_w_out[j])
        elif kind == 1:
            m = chunked_sgu_mixer(hn, sg_w_in[j], sg_ln_g[j], sg_ln_b[j], sg_w_s[j], sg_b_s[j], sg_w_out[j])
        else:
            m = stick_breaking_mixer(hn, sb_w_qkv[j], sb_w_out[j])
        h = h + rmsnorm(m, norm_mix_post[i])
        f = conv_glu_ffn(rmsnorm(h, norm_ffn_pre[i]), ffn_w_up[i], ffn_conv_w[i], ffn_conv_b[i], ffn_w_down[i])
        h = h + rmsnorm(f, norm_ffn_post[i])
    return h


import jax as _jax
import jax.numpy as _jnp

TWIN_FORMAT = 'train_step'
FWD_PARAMS = ['x', 'norm_mix_pre', 'norm_mix_post', 'norm_ffn_pre', 'norm_ffn_post', 'sc_w_in', 'sc_conv_w', 'sc_w_out', 'sg_w_in', 'sg_ln_g', 'sg_ln_b', 'sg_w_s', 'sg_b_s', 'sg_w_out', 'sb_w_qkv', 'sb_w_out', 'ffn_w_up', 'ffn_conv_w', 'ffn_conv_b', 'ffn_w_down']
TWIN_WEIGHTS = ['norm_mix_pre', 'norm_mix_post', 'norm_ffn_pre', 'norm_ffn_post', 'sc_w_in', 'sc_conv_w', 'sc_w_out', 'sg_w_in', 'sg_ln_g', 'sg_ln_b', 'sg_w_s', 'sg_b_s', 'sg_w_out', 'sb_w_qkv', 'sb_w_out', 'ffn_w_up', 'ffn_conv_w', 'ffn_conv_b', 'ffn_w_down']
TWIN_DIFF_INPUT = 'x'
TWIN_INPUTS = ['x', 'norm_mix_pre', 'norm_mix_post', 'norm_ffn_pre', 'norm_ffn_post', 'sc_w_in', 'sc_conv_w', 'sc_w_out', 'sg_w_in', 'sg_ln_g', 'sg_ln_b', 'sg_w_s', 'sg_b_s', 'sg_w_out', 'sb_w_qkv', 'sb_w_out', 'ffn_w_up', 'ffn_conv_w', 'ffn_conv_b', 'ffn_w_down', 'loss_target', 'm_norm_mix_pre', 'm_norm_mix_post', 'm_norm_ffn_pre', 'm_norm_ffn_post', 'm_sc_w_in', 'm_sc_conv_w', 'm_sc_w_out', 'm_sg_w_in', 'm_sg_ln_g', 'm_sg_ln_b', 'm_sg_w_s', 'm_sg_b_s', 'm_sg_w_out', 'm_sb_w_qkv', 'm_sb_w_out', 'm_ffn_w_up', 'm_ffn_conv_w', 'm_ffn_conv_b', 'm_ffn_w_down', 'v_norm_mix_pre', 'v_norm_mix_post', 'v_norm_ffn_pre', 'v_norm_ffn_post', 'v_sc_w_in', 'v_sc_conv_w', 'v_sc_w_out', 'v_sg_w_in', 'v_sg_ln_g', 'v_sg_ln_b', 'v_sg_w_s', 'v_sg_b_s', 'v_sg_w_out', 'v_sb_w_qkv', 'v_sb_w_out', 'v_ffn_w_up', 'v_ffn_conv_w', 'v_ffn_conv_b', 'v_ffn_w_down']
TWIN_OUTPUTS = ['loss', 'grad_x', 'grad_norm_mix_pre', 'grad_norm_mix_post', 'grad_norm_ffn_pre', 'grad_norm_ffn_post', 'grad_sc_w_in', 'grad_sc_conv_w', 'grad_sc_w_out', 'grad_sg_w_in', 'grad_sg_ln_g', 'grad_sg_ln_b', 'grad_sg_w_s', 'grad_sg_b_s', 'grad_sg_w_out', 'grad_sb_w_qkv', 'grad_sb_w_out', 'grad_ffn_w_up', 'grad_ffn_conv_w', 'grad_ffn_conv_b', 'grad_ffn_w_down', 'delta_norm_mix_pre', 'delta_norm_mix_post', 'delta_norm_ffn_pre', 'delta_norm_ffn_post', 'delta_sc_w_in', 'delta_sc_conv_w', 'delta_sc_w_out', 'delta_sg_w_in', 'delta_sg_ln_g', 'delta_sg_ln_b', 'delta_sg_w_s', 'delta_sg_b_s', 'delta_sg_w_out', 'delta_sb_w_qkv', 'delta_sb_w_out', 'delta_ffn_w_up', 'delta_ffn_conv_w', 'delta_ffn_conv_b', 'delta_ffn_w_down', 'new_m_norm_mix_pre', 'new_m_norm_mix_post', 'new_m_norm_ffn_pre', 'new_m_norm_ffn_post', 'new_m_sc_w_in', 'new_m_sc_conv_w', 'new_m_sc_w_out', 'new_m_sg_w_in', 'new_m_sg_ln_g', 'new_m_sg_ln_b', 'new_m_sg_w_s', 'new_m_sg_b_s', 'new_m_sg_w_out', 'new_m_sb_w_qkv', 'new_m_sb_w_out', 'new_m_ffn_w_up', 'new_m_ffn_conv_w', 'new_m_ffn_conv_b', 'new_m_ffn_w_down', 'new_v_norm_mix_pre', 'new_v_norm_mix_post', 'new_v_norm_ffn_pre', 'new_v_norm_ffn_post', 'new_v_sc_w_in', 'new_v_sc_conv_w', 'new_v_sc_w_out', 'new_v_sg_w_in', 'new_v_sg_ln_g', 'new_v_sg_ln_b', 'new_v_sg_w_s', 'new_v_sg_b_s', 'new_v_sg_w_out', 'new_v_sb_w_qkv', 'new_v_sb_w_out', 'new_v_ffn_w_up', 'new_v_ffn_conv_w', 'new_v_ffn_conv_b', 'new_v_ffn_w_down']
TWIN_LEAF_KINDS = {'loss': 'loss', 'grad_x': 'grad_x', 'grad_norm_mix_pre': 'grad_w', 'grad_norm_mix_post': 'grad_w', 'grad_norm_ffn_pre': 'grad_w', 'grad_norm_ffn_post': 'grad_w', 'grad_sc_w_in': 'grad_w', 'grad_sc_conv_w': 'grad_w', 'grad_sc_w_out': 'grad_w', 'grad_sg_w_in': 'grad_w', 'grad_sg_ln_g': 'grad_w', 'grad_sg_ln_b': 'grad_w', 'grad_sg_w_s': 'grad_w', 'grad_sg_b_s': 'grad_w', 'grad_sg_w_out': 'grad_w', 'grad_sb_w_qkv': 'grad_w', 'grad_sb_w_out': 'grad_w', 'grad_ffn_w_up': 'grad_w', 'grad_ffn_conv_w': 'grad_w', 'grad_ffn_conv_b': 'grad_w', 'grad_ffn_w_down': 'grad_w', 'delta_norm_mix_pre': 'delta_w', 'delta_norm_mix_post': 'delta_w', 'delta_norm_ffn_pre': 'delta_w', 'delta_norm_ffn_post': 'delta_w', 'delta_sc_w_in': 'delta_w', 'delta_sc_conv_w': 'delta_w', 'delta_sc_w_out': 'delta_w', 'delta_sg_w_in': 'delta_w', 'delta_sg_ln_g': 'delta_w', 'delta_sg_ln_b': 'delta_w', 'delta_sg_w_s': 'delta_w', 'delta_sg_b_s': 'delta_w', 'delta_sg_w_out': 'delta_w', 'delta_sb_w_qkv': 'delta_w', 'delta_sb_w_out': 'delta_w', 'delta_ffn_w_up': 'delta_w', 'delta_ffn_conv_w': 'delta_w', 'delta_ffn_conv_b': 'delta_w', 'delta_ffn_w_down': 'delta_w', 'new_m_norm_mix_pre': 'new_m', 'new_m_norm_mix_post': 'new_m', 'new_m_norm_ffn_pre': 'new_m', 'new_m_norm_ffn_post': 'new_m', 'new_m_sc_w_in': 'new_m', 'new_m_sc_conv_w': 'new_m', 'new_m_sc_w_out': 'new_m', 'new_m_sg_w_in': 'new_m', 'new_m_sg_ln_g': 'new_m', 'new_m_sg_ln_b': 'new_m', 'new_m_sg_w_s': 'new_m', 'new_m_sg_b_s': 'new_m', 'new_m_sg_w_out': 'new_m', 'new_m_sb_w_qkv': 'new_m', 'new_m_sb_w_out': 'new_m', 'new_m_ffn_w_up': 'new_m', 'new_m_ffn_conv_w': 'new_m', 'new_m_ffn_conv_b': 'new_m', 'new_m_ffn_w_down': 'new_m', 'new_v_norm_mix_pre': 'new_v', 'new_v_norm_mix_post': 'new_v', 'new_v_norm_ffn_pre': 'new_v', 'new_v_norm_ffn_post': 'new_v', 'new_v_sc_w_in': 'new_v', 'new_v_sc_conv_w': 'new_v', 'new_v_sc_w_out': 'new_v', 'new_v_sg_w_in': 'new_v', 'new_v_sg_ln_g': 'new_v', 'new_v_sg_ln_b': 'new_v', 'new_v_sg_w_s': 'new_v', 'new_v_sg_b_s': 'new_v', 'new_v_sg_w_out': 'new_v', 'new_v_sb_w_qkv': 'new_v', 'new_v_sb_w_out': 'new_v', 'new_v_ffn_w_up': 'new_v', 'new_v_ffn_conv_w': 'new_v', 'new_v_ffn_conv_b': 'new_v', 'new_v_ffn_w_down': 'new_v'}


def _forward(args):
    return _fwd_reference(*[args[k] for k in FWD_PARAMS])


def _output_shape():
    def fwd():
        inp = _fwd_setup_inputs(0)
        return _fwd_reference(*[inp[k] for k in FWD_PARAMS])
    out = _jax.eval_shape(fwd)
    return out.shape, out.dtype

N_MICROBATCH = 1
ADAM_LR = 0.001
ADAM_B1 = 0.9
ADAM_B2 = 0.999
ADAM_EPS = 1e-08
ADAM_WD = 0.01
ADAM_STEP = 10
PER_EXAMPLE_BATCH_AXIS = {'x': 0, 'loss_target': 0}
SHARED_INPUTS = []
_WEIGHT_DTYPES = {'norm_mix_pre': _jnp.float32, 'norm_mix_post': _jnp.float32, 'norm_ffn_pre': _jnp.float32, 'norm_ffn_post': _jnp.float32, 'sc_w_in': _jnp.float32, 'sc_conv_w': _jnp.float32, 'sc_w_out': _jnp.float32, 'sg_w_in': _jnp.float32, 'sg_ln_g': _jnp.float32, 'sg_ln_b': _jnp.float32, 'sg_w_s': _jnp.float32, 'sg_b_s': _jnp.float32, 'sg_w_out': _jnp.float32, 'sb_w_qkv': _jnp.float32, 'sb_w_out': _jnp.float32, 'ffn_w_up': _jnp.float32, 'ffn_conv_w': _jnp.float32, 'ffn_conv_b': _jnp.float32, 'ffn_w_down': _jnp.float32}
MOMENT_SCALE = {'norm_mix_pre': 1.742390e+00, 'norm_mix_post': 3.213848e+01, 'norm_ffn_pre': 1.167089e+00, 'norm_ffn_post': 3.185187e+01, 'sc_w_in': 1.054903e+00, 'sc_conv_w': 1.077761e+00, 'sc_w_out': 1.079933e+00, 'sg_w_in': 7.938228e-01, 'sg_ln_g': 4.566616e-01, 'sg_ln_b': 4.734306e-01, 'sg_w_s': 6.385040e-01, 'sg_b_s': 1.034172e+00, 'sg_w_out': 3.882252e+00, 'sb_w_qkv': 1.391697e+00, 'sb_w_out': 2.633644e+00, 'ffn_w_up': 4.963506e-01, 'ffn_conv_w': 5.389005e-01, 'ffn_conv_b': 2.420506e+00, 'ffn_w_down': 9.271287e-01}


def _to_microbatches(a, axis):
    t = _jnp.moveaxis(a, axis, 0)
    t = t.reshape((N_MICROBATCH, t.shape[0] // N_MICROBATCH) + t.shape[1:])
    return _jnp.moveaxis(t, 1, axis + 1)


def setup_inputs(seed: int = 0) -> dict:
    inp = _fwd_setup_inputs(seed)
    key = _jax.random.fold_in(_jax.random.key(seed), 7919)
    shape, _ = _output_shape()
    out = dict(inp)
    out["loss_target"] = _jax.random.normal(_jax.random.fold_in(key, 0), shape, _jnp.float32)
    for i, name in enumerate(TWIN_WEIGHTS):
        w = inp[name].astype(_jnp.float32)
        if MOMENT_SCALE is None:
            s = _jnp.sqrt(_jnp.mean(_jnp.square(w)) + 1e-30)
        else:
            s = MOMENT_SCALE[name]
        km, kv = _jax.random.split(_jax.random.fold_in(key, i + 1))
        out[name] = w
        out["m_" + name] = s * _jax.random.normal(km, w.shape, _jnp.float32)
        out["v_" + name] = (s * s) * _jax.random.uniform(kv, w.shape, _jnp.float32, 0.5, 1.5)
    if N_MICROBATCH > 1:
        for name, axis in PER_EXAMPLE_BATCH_AXIS.items():
            out[name] = _to_microbatches(out[name], axis)
    return {'x': out['x'], 'norm_mix_pre': out['norm_mix_pre'], 'norm_mix_post': out['norm_mix_post'], 'norm_ffn_pre': out['norm_ffn_pre'], 'norm_ffn_post': out['norm_ffn_post'], 'sc_w_in': out['sc_w_in'], 'sc_conv_w': out['sc_conv_w'], 'sc_w_out': out['sc_w_out'], 'sg_w_in': out['sg_w_in'], 'sg_ln_g': out['sg_ln_g'], 'sg_ln_b': out['sg_ln_b'], 'sg_w_s': out['sg_w_s'], 'sg_b_s': out['sg_b_s'], 'sg_w_out': out['sg_w_out'], 'sb_w_qkv': out['sb_w_qkv'], 'sb_w_out': out['sb_w_out'], 'ffn_w_up': out['ffn_w_up'], 'ffn_conv_w': out['ffn_conv_w'], 'ffn_conv_b': out['ffn_conv_b'], 'ffn_w_down': out['ffn_w_down'], 'loss_target': out['loss_target'], 'm_norm_mix_pre': out['m_norm_mix_pre'], 'm_norm_mix_post': out['m_norm_mix_post'], 'm_norm_ffn_pre': out['m_norm_ffn_pre'], 'm_norm_ffn_post': out['m_norm_ffn_post'], 'm_sc_w_in': out['m_sc_w_in'], 'm_sc_conv_w': out['m_sc_conv_w'], 'm_sc_w_out': out['m_sc_w_out'], 'm_sg_w_in': out['m_sg_w_in'], 'm_sg_ln_g': out['m_sg_ln_g'], 'm_sg_ln_b': out['m_sg_ln_b'], 'm_sg_w_s': out['m_sg_w_s'], 'm_sg_b_s': out['m_sg_b_s'], 'm_sg_w_out': out['m_sg_w_out'], 'm_sb_w_qkv': out['m_sb_w_qkv'], 'm_sb_w_out': out['m_sb_w_out'], 'm_ffn_w_up': out['m_ffn_w_up'], 'm_ffn_conv_w': out['m_ffn_conv_w'], 'm_ffn_conv_b': out['m_ffn_conv_b'], 'm_ffn_w_down': out['m_ffn_w_down'], 'v_norm_mix_pre': out['v_norm_mix_pre'], 'v_norm_mix_post': out['v_norm_mix_post'], 'v_norm_ffn_pre': out['v_norm_ffn_pre'], 'v_norm_ffn_post': out['v_norm_ffn_post'], 'v_sc_w_in': out['v_sc_w_in'], 'v_sc_conv_w': out['v_sc_conv_w'], 'v_sc_w_out': out['v_sc_w_out'], 'v_sg_w_in': out['v_sg_w_in'], 'v_sg_ln_g': out['v_sg_ln_g'], 'v_sg_ln_b': out['v_sg_ln_b'], 'v_sg_w_s': out['v_sg_w_s'], 'v_sg_b_s': out['v_sg_b_s'], 'v_sg_w_out': out['v_sg_w_out'], 'v_sb_w_qkv': out['v_sb_w_qkv'], 'v_sb_w_out': out['v_sb_w_out'], 'v_ffn_w_up': out['v_ffn_w_up'], 'v_ffn_conv_w': out['v_ffn_conv_w'], 'v_ffn_conv_b': out['v_ffn_conv_b'], 'v_ffn_w_down': out['v_ffn_w_down']}


def _loss(weights, diff, rest, loss_target):
    with _jax.named_scope("forward"):
        args = {**rest, TWIN_DIFF_INPUT: diff, **{k: w.astype(_WEIGHT_DTYPES[k]) for k, w in weights.items()}}
        y = _forward(args)
    with _jax.named_scope("loss_head"):
        err = _jnp.square(y.astype(_jnp.float32) - loss_target)
        return 0.5 * _jnp.sum(_jnp.mean(err, axis=-1)) if err.ndim else 0.5 * err


def _adamw(w, g, m, v):
    m = ADAM_B1 * m + (1.0 - ADAM_B1) * g
    v = ADAM_B2 * v + (1.0 - ADAM_B2) * _jnp.square(g)
    m_hat = m / (1.0 - ADAM_B1 ** ADAM_STEP)
    v_hat = v / (1.0 - ADAM_B2 ** ADAM_STEP)
    delta = -ADAM_LR * (m_hat / (_jnp.sqrt(v_hat) + ADAM_EPS) + ADAM_WD * w)
    return delta, m, v


def reference(x, norm_mix_pre, norm_mix_post, norm_ffn_pre, norm_ffn_post, sc_w_in, sc_conv_w, sc_w_out, sg_w_in, sg_ln_g, sg_ln_b, sg_w_s, sg_b_s, sg_w_out, sb_w_qkv, sb_w_out, ffn_w_up, ffn_conv_w, ffn_conv_b, ffn_w_down, loss_target, m_norm_mix_pre, m_norm_mix_post, m_norm_ffn_pre, m_norm_ffn_post, m_sc_w_in, m_sc_conv_w, m_sc_w_out, m_sg_w_in, m_sg_ln_g, m_sg_ln_b, m_sg_w_s, m_sg_b_s, m_sg_w_out, m_sb_w_qkv, m_sb_w_out, m_ffn_w_up, m_ffn_conv_w, m_ffn_conv_b, m_ffn_w_down, v_norm_mix_pre, v_norm_mix_post, v_norm_ffn_pre, v_norm_ffn_post, v_sc_w_in, v_sc_conv_w, v_sc_w_out, v_sg_w_in, v_sg_ln_g, v_sg_ln_b, v_sg_w_s, v_sg_b_s, v_sg_w_out, v_sb_w_qkv, v_sb_w_out, v_ffn_w_up, v_ffn_conv_w, v_ffn_conv_b, v_ffn_w_down):
    given = dict(x=x, norm_mix_pre=norm_mix_pre, norm_mix_post=norm_mix_post, norm_ffn_pre=norm_ffn_pre, norm_ffn_post=norm_ffn_post, sc_w_in=sc_w_in, sc_conv_w=sc_conv_w, sc_w_out=sc_w_out, sg_w_in=sg_w_in, sg_ln_g=sg_ln_g, sg_ln_b=sg_ln_b, sg_w_s=sg_w_s, sg_b_s=sg_b_s, sg_w_out=sg_w_out, sb_w_qkv=sb_w_qkv, sb_w_out=sb_w_out, ffn_w_up=ffn_w_up, ffn_conv_w=ffn_conv_w, ffn_conv_b=ffn_conv_b, ffn_w_down=ffn_w_down, loss_target=loss_target, m_norm_mix_pre=m_norm_mix_pre, m_norm_mix_post=m_norm_mix_post, m_norm_ffn_pre=m_norm_ffn_pre, m_norm_ffn_post=m_norm_ffn_post, m_sc_w_in=m_sc_w_in, m_sc_conv_w=m_sc_conv_w, m_sc_w_out=m_sc_w_out, m_sg_w_in=m_sg_w_in, m_sg_ln_g=m_sg_ln_g, m_sg_ln_b=m_sg_ln_b, m_sg_w_s=m_sg_w_s, m_sg_b_s=m_sg_b_s, m_sg_w_out=m_sg_w_out, m_sb_w_qkv=m_sb_w_qkv, m_sb_w_out=m_sb_w_out, m_ffn_w_up=m_ffn_w_up, m_ffn_conv_w=m_ffn_conv_w, m_ffn_conv_b=m_ffn_conv_b, m_ffn_w_down=m_ffn_w_down, v_norm_mix_pre=v_norm_mix_pre, v_norm_mix_post=v_norm_mix_post, v_norm_ffn_pre=v_norm_ffn_pre, v_norm_ffn_post=v_norm_ffn_post, v_sc_w_in=v_sc_w_in, v_sc_conv_w=v_sc_conv_w, v_sc_w_out=v_sc_w_out, v_sg_w_in=v_sg_w_in, v_sg_ln_g=v_sg_ln_g, v_sg_ln_b=v_sg_ln_b, v_sg_w_s=v_sg_w_s, v_sg_b_s=v_sg_b_s, v_sg_w_out=v_sg_w_out, v_sb_w_qkv=v_sb_w_qkv, v_sb_w_out=v_sb_w_out, v_ffn_w_up=v_ffn_w_up, v_ffn_conv_w=v_ffn_conv_w, v_ffn_conv_b=v_ffn_conv_b, v_ffn_w_down=v_ffn_w_down)
    weights = {n: given[n] for n in TWIN_WEIGHTS}
    shared = {n: given[n] for n in SHARED_INPUTS}
    per_example = {n: given[n] for n in ['x']}
    grad_fn = _jax.value_and_grad(_loss, argnums=(0, 1))

    def one_microbatch(ex, loss_target):
        ex = dict(ex)
        diff = ex.pop(TWIN_DIFF_INPUT)
        return grad_fn(weights, diff, {**shared, **ex}, loss_target)

    if N_MICROBATCH == 1:
        loss, (grad_w, grad_x) = one_microbatch(per_example, given["loss_target"])
    else:
        def body(carry, xs):
            loss_sum, grad_sum = carry
            l_k, (gw_k, gx_k) = one_microbatch(xs[0], xs[1])
            with _jax.named_scope("update"):
                return (loss_sum + l_k, _jax.tree.map(_jnp.add, grad_sum, gw_k)), gx_k

        init = (_jnp.zeros((), _jnp.float32), _jax.tree.map(_jnp.zeros_like, weights))
        (loss, grad_w), grad_x = _jax.lax.scan(body, init, (per_example, given["loss_target"]))
    with _jax.named_scope("update"):
        delta_w, new_m, new_v = {}, {}, {}
        for n in TWIN_WEIGHTS:
            delta_w[n], new_m[n], new_v[n] = _adamw(weights[n], grad_w[n], given["m_" + n], given["v_" + n])
    return (loss, grad_x, *[grad_w[n] for n in TWIN_WEIGHTS], *[delta_w[n] for n in TWIN_WEIGHTS],
            *[new_m[n] for n in TWIN_WEIGHTS], *[new_v[n] for n in TWIN_WEIGHTS])
```

```python
import functools

import jax
import jax.numpy as jnp
from jax import lax
from jax.experimental import pallas as pl
from jax.experimental.pallas import tpu as pltpu

F32 = jnp.float32
BF16 = jnp.bfloat16

EPS = 1e-6
CHUNK = 128
HEAD_DIM = 128
SG_GROUPS = 8
HALO = 16
N_DEV = 8
LANES = 128
V7X_VMEM_LIMIT = 56 * 1024 * 1024

ADAM_LR = 0.001
ADAM_B1 = 0.9
ADAM_B2 = 0.999
ADAM_EPS = 1e-08
ADAM_WD = 0.01
ADAM_STEP = 10

MESH = pl.DeviceIdType.MESH
ANY = pl.BlockSpec(memory_space=pl.ANY)


def _tile(n, prefs):
    for p in prefs:
        if p <= n and n % p == 0:
            return p
    return n


def _params(*sem):
    return pltpu.CompilerParams(dimension_semantics=sem, vmem_limit_bytes=V7X_VMEM_LIMIT)


def _mm_nn(a, b, out_dtype, nsplit, name):
    m, k = a.shape
    n = b.shape[1]
    w = n // nsplit
    tm = _tile(m, (1024, 512, 256, 128)) if k <= 2048 else _tile(m, (512, 256, 128))
    tn = _tile(w, (512, 256, 128))
    per = w // tn

    def body(a_ref, b_ref, o_ref):
        o_ref[...] = jnp.dot(a_ref[...], b_ref[...], preferred_element_type=F32).astype(o_ref.dtype)

    if nsplit == 1:
        out_shape = jax.ShapeDtypeStruct((m, n), out_dtype)
        out_spec = pl.BlockSpec((tm, tn), lambda i, j: (i, j))
    else:
        out_shape = jax.ShapeDtypeStruct((nsplit, m, w), out_dtype)
        out_spec = pl.BlockSpec((None, tm, tn), lambda i, j: (j // per, i, j % per))
    return pl.pallas_call(
        body, grid=(m // tm, n // tn), name=name,
        in_specs=[pl.BlockSpec((tm, k), lambda i, j: (i, 0)), pl.BlockSpec((k, tn), lambda i, j: (0, j))],
        out_specs=out_spec, out_shape=out_shape,
        compiler_params=_params("parallel", "arbitrary"),
    )(a, b)


def _mm_nt(a, b, out_dtype, name):
    a3 = a if a.ndim == 3 else a[None]
    ns, m, w = a3.shape
    ko, n = b.shape
    assert n == ns * w
    tm = _tile(m, (512, 256, 128))
    tko = _tile(ko, (256, 128)) if n > 4096 else _tile(ko, (512, 256, 128))

    def body(a_ref, b_ref, o_ref):
        acc = None
        for s in range(ns):
            p = lax.dot_general(a_ref[s], b_ref[:, s * w:(s + 1) * w], (((1,), (1,)), ((), ())),
                                preferred_element_type=F32)
            acc = p if acc is None else acc + p
        o_ref[...] = acc.astype(o_ref.dtype)

    return pl.pallas_call(
        body, grid=(m // tm, ko // tko), name=name,
        in_specs=[pl.BlockSpec((ns, tm, w), lambda i, j: (0, i, 0)), pl.BlockSpec((tko, n), lambda i, j: (j, 0))],
        out_specs=pl.BlockSpec((tm, tko), lambda i, j: (i, j)),
        out_shape=jax.ShapeDtypeStruct((m, ko), out_dtype),
        compiler_params=_params("parallel", "arbitrary"),
    )(a3, b)


def _mm_tn(a, b, out_dtype, name):
    b3 = b if b.ndim == 3 else b[None]
    ns, m, w = b3.shape
    k = a.shape[1]
    n = ns * w
    tk = _tile(k, (512, 256, 128))
    tn = _tile(w, (2816, 2048, 1024, 512, 256, 128))
    ts = _tile(m, (1024, 512, 256, 128))
    per = w // tn
    n_s = m // ts

    def body(a_ref, b_ref, o_ref, acc_ref):
        s = pl.program_id(2)

        @pl.when(s == 0)
        def _():
            acc_ref[...] = jnp.zeros_like(acc_ref)

        acc_ref[...] += lax.dot_general(a_ref[...], b_ref[...], (((0,), (0,)), ((), ())),
                                        preferred_element_type=F32)

        @pl.when(s == n_s - 1)
        def _():
            o_ref[...] = acc_ref[...].astype(o_ref.dtype)

    return pl.pallas_call(
        body, grid=(k // tk, n // tn, n_s), name=name,
        in_specs=[pl.BlockSpec((ts, tk), lambda i, j, s: (s, i)),
                  pl.BlockSpec((None, ts, tn), lambda i, j, s: (j // per, s, j % per))],
        out_specs=pl.BlockSpec((tk, tn), lambda i, j, s: (i, j)),
        out_shape=jax.ShapeDtypeStruct((k, n), out_dtype),
        scratch_shapes=[pltpu.VMEM((tk, tn), F32)],
        compiler_params=_params("parallel", "parallel", "arbitrary"),
    )(a, b3)


def _rms_rows(s):
    return _tile(s, (256, 128))


def _rms_fwd(x, g, name):
    s, d = x.shape
    ts = _rms_rows(s)

    def body(x_ref, g_ref, o_ref):
        xf = x_ref[...]
        r = lax.rsqrt(jnp.mean(xf * xf, axis=-1, keepdims=True) + EPS)
        o_ref[...] = ((xf * r) * g_ref[...]).astype(o_ref.dtype)

    return pl.pallas_call(
        body, grid=(s // ts,), name=name,
        in_specs=[pl.BlockSpec((ts, d), lambda i: (i, 0)), pl.BlockSpec((1, d), lambda i: (0, 0))],
        out_specs=pl.BlockSpec((ts, d), lambda i: (i, 0)),
        out_shape=jax.ShapeDtypeStruct((s, d), BF16),
        compiler_params=_params("parallel"),
    )(x, g)


def _resid_rms_fwd(h, m, g, name):
    s, d = h.shape
    ts = _rms_rows(s)

    def body(h_ref, m_ref, g_ref, o_ref):
        mf = m_ref[...]
        r = lax.rsqrt(jnp.mean(mf * mf, axis=-1, keepdims=True) + EPS)
        o_ref[...] = h_ref[...] + (mf * r) * g_ref[...]

    return pl.pallas_call(
        body, grid=(s // ts,), name=name,
        in_specs=[pl.BlockSpec((ts, d), lambda i: (i, 0)), pl.BlockSpec((ts, d), lambda i: (i, 0)),
                  pl.BlockSpec((1, d), lambda i: (0, 0))],
        out_specs=pl.BlockSpec((ts, d), lambda i: (i, 0)),
        out_shape=jax.ShapeDtypeStruct((s, d), F32),
        compiler_params=_params("parallel"),
    )(h, m, g)


def _rms_bwd(x, g, dy, dres, out_dtype, name):
    s, d = x.shape
    ts = _rms_rows(s)
    has_res = dres is not None

    def body(*refs):
        if has_res:
            x_ref, g_ref, dy_ref, dres_ref, dx_ref, dg_ref = refs
        else:
            x_ref, g_ref, dy_ref, dx_ref, dg_ref = refs
        i = pl.program_id(0)
        xf = x_ref[...]
        dyf = dy_ref[...].astype(F32)
        r = lax.rsqrt(jnp.mean(xf * xf, axis=-1, keepdims=True) + EPS)
        xh = xf * r
        dxh = dyf * g_ref[...]
        dx = r * (dxh - xh * jnp.mean(dxh * xh, axis=-1, keepdims=True))
        if has_res:
            dx = dx + dres_ref[...]
        dx_ref[...] = dx.astype(dx_ref.dtype)

        @pl.when(i == 0)
        def _():
            dg_ref[...] = jnp.zeros_like(dg_ref)

        dg_ref[...] += jnp.sum(dyf * xh, axis=0, keepdims=True)

    row = pl.BlockSpec((ts, d), lambda i: (i, 0))
    vec = pl.BlockSpec((1, d), lambda i: (0, 0))
    ins = [x, g, dy] + ([dres] if has_res else [])
    return pl.pallas_call(
        body, grid=(s // ts,), name=name,
        in_specs=[row, vec, row] + ([row] if has_res else []),
        out_specs=(row, vec),
        out_shape=(jax.ShapeDtypeStruct((s, d), out_dtype), jax.ShapeDtypeStruct((1, d), F32)),
        compiler_params=_params("arbitrary"),
    )(*ins)


def _loss_head(h, target, name):
    s, d = h.shape
    ts = _rms_rows(s)
    n_i = s // ts

    def body(h_ref, t_ref, dh_ref, loss_ref, acc_ref):
        i = pl.program_id(0)
        err = h_ref[...] - t_ref[...]
        dh_ref[...] = err * (1.0 / d)

        @pl.when(i == 0)
        def _():
            acc_ref[...] = jnp.zeros_like(acc_ref)

        acc_ref[...] += jnp.sum(err * err, axis=0, keepdims=True)

        @pl.when(i == n_i - 1)
        def _():
            loss_ref[...] = jnp.sum(acc_ref[...], axis=1, keepdims=True) * (0.5 / d)

    row = pl.BlockSpec((ts, d), lambda i: (i, 0))
    return pl.pallas_call(
        body, grid=(n_i,), name=name,
        in_specs=[row, row],
        out_specs=(row, pl.BlockSpec((1, 1), lambda i: (0, 0))),
        out_shape=(jax.ShapeDtypeStruct((s, d), F32), jax.ShapeDtypeStruct((1, 1), F32)),
        scratch_shapes=[pltpu.VMEM((1, d), F32)],
        compiler_params=_params("arbitrary"),
    )(h, target)


def _halo_specs(ns, s, ts, tc):
    hb = ts // HALO
    last = s // HALO - 1
    cur = pl.BlockSpec((ns, ts, tc), lambda j, i: (0, i, j))
    prev = pl.BlockSpec((ns, HALO, tc), lambda j, i: (0, jnp.maximum(i * hb - 1, 0), j))
    nxt = pl.BlockSpec((ns, HALO, tc), lambda j, i: (0, jnp.minimum((i + 1) * hb, last), j))
    return cur, prev, nxt


def _ext(prev, cur, nxt, first, last):
    p = jnp.where(first, 0.0, prev.astype(F32))
    n = jnp.where(last, 0.0, nxt.astype(F32))
    return jnp.concatenate([p, cur.astype(F32), n], axis=0)


def _dn(xe, k):
    return xe if k == 0 else pltpu.roll(xe, k, 0)


def _up(xe, k):
    return xe if k == 0 else pltpu.roll(xe, xe.shape[0] - k, 0)


def _conv_e(xe, w):
    return w[0:1, :] * _dn(xe, 2) + w[1:2, :] * _dn(xe, 1) + w[2:3, :] * xe


def _conv_t_e(de, w):
    return w[2:3, :] * de + w[1:2, :] * _up(de, 1) + w[0:1, :] * _up(de, 2)


def _conv_dw(de, xe, ts):
    return [jnp.sum((de * _dn(xe, 2 - k))[HALO:HALO + ts], axis=0, keepdims=True) for k in range(3)]


def _conv_tiles(s, c):
    return _tile(s, (256, 128)), _tile(c, (512, 256, 128))


def _sc_gate_fwd(p, w, name):
    _, s, c = p.shape
    ts, tc = _conv_tiles(s, c)
    n_i = s // ts

    def body(p_ref, pp_ref, w_ref, y_ref):
        i = pl.program_id(1)
        ze = _ext(pp_ref[1], p_ref[1], pp_ref[1], i == 0, True) * _ext(pp_ref[2], p_ref[2], pp_ref[2], i == 0, True)
        cz = _conv_e(ze, w_ref[...])[HALO:HALO + ts]
        y_ref[...] = (p_ref[0].astype(F32) * cz).astype(y_ref.dtype)

    cur, prev, _ = _halo_specs(3, s, ts, tc)
    return pl.pallas_call(
        body, grid=(c // tc, n_i), name=name,
        in_specs=[cur, prev, pl.BlockSpec((3, tc), lambda j, i: (0, j))],
        out_specs=pl.BlockSpec((ts, tc), lambda j, i: (i, j)),
        out_shape=jax.ShapeDtypeStruct((s, c), BF16),
        compiler_params=_params("parallel", "arbitrary"),
    )(p, p, w)


def _sc_gate_bwd(p, dy, w, name):
    _, s, c = p.shape
    ts, tc = _conv_tiles(s, c)
    n_i = s // ts

    def body(p_ref, pp_ref, pn_ref, dy_ref, dyn_ref, w_ref, dp_ref, dw_ref):
        i = pl.program_id(1)
        first, last = i == 0, i == n_i - 1
        wv = w_ref[...]
        gbe = _ext(pp_ref[0], p_ref[0], pn_ref[0], first, last)
        gce = _ext(pp_ref[1], p_ref[1], pn_ref[1], first, last)
        hhe = _ext(pp_ref[2], p_ref[2], pn_ref[2], first, last)
        dye = _ext(dyn_ref[...], dy_ref[...], dyn_ref[...], True, last)
        ze = gce * hhe
        cze = _conv_e(ze, wv)
        dcze = dye * gbe
        dze = _conv_t_e(dcze, wv)
        rows = slice(HALO, HALO + ts)
        dp_ref[0] = (dye * cze)[rows].astype(dp_ref.dtype)
        dp_ref[1] = (dze * hhe)[rows].astype(dp_ref.dtype)
        dp_ref[2] = (dze * gce)[rows].astype(dp_ref.dtype)

        @pl.when(first)
        def _():
            dw_ref[...] = jnp.zeros_like(dw_ref)

        dws = _conv_dw(dcze, ze, ts)
        for k in range(3):
            dw_ref[k:k + 1, :] += dws[k]

    cur, prev, nxt = _halo_specs(3, s, ts, tc)
    hb = ts // HALO
    last_h = s // HALO - 1
    return pl.pallas_call(
        body, grid=(c // tc, n_i), name=name,
        in_specs=[cur, prev, nxt,
                  pl.BlockSpec((ts, tc), lambda j, i: (i, j)),
                  pl.BlockSpec((HALO, tc), lambda j, i: (jnp.minimum((i + 1) * hb, last_h), j)),
                  pl.BlockSpec((3, tc), lambda j, i: (0, j))],
        out_specs=(cur, pl.BlockSpec((3, tc), lambda j, i: (0, j))),
        out_shape=(jax.ShapeDtypeStruct((3, s, c), BF16), jax.ShapeDtypeStruct((3, c), F32)),
        compiler_params=_params("parallel", "arbitrary"),
    )(p, p, p, dy, dy, w)


def _ffn_act_fwd(u, w, b, name):
    _, s, f = u.shape
    ts, tc = _conv_tiles(s, f)
    n_i = s // ts

    def body(u_ref, up_ref, w_ref, b_ref, a_ref):
        i = pl.program_id(1)
        rows = slice(HALO, HALO + ts)
        cg = _conv_e(_ext(up_ref[0], u_ref[0], up_ref[0], i == 0, True), w_ref[0])[rows] + b_ref[0]
        cv = _conv_e(_ext(up_ref[1], u_ref[1], up_ref[1], i == 0, True), w_ref[1])[rows] + b_ref[1]
        a_ref[...] = (cg * jax.nn.sigmoid(cg) * cv).astype(a_ref.dtype)

    cur, prev, _ = _halo_specs(2, s, ts, tc)
    return pl.pallas_call(
        body, grid=(f // tc, n_i), name=name,
        in_specs=[cur, prev, pl.BlockSpec((2, 3, tc), lambda j, i: (0, 0, j)),
                  pl.BlockSpec((2, 1, tc), lambda j, i: (0, 0, j))],
        out_specs=pl.BlockSpec((ts, tc), lambda j, i: (i, j)),
        out_shape=jax.ShapeDtypeStruct((s, f), BF16),
        compiler_params=_params("parallel", "arbitrary"),
    )(u, u, w, b)


def _ffn_act_bwd(u, da, w, b, name):
    _, s, f = u.shape
    ts, tc = _conv_tiles(s, f)
    n_i = s // ts

    def body(u_ref, up_ref, un_ref, da_ref, dan_ref, w_ref, b_ref, du_ref, dw_ref, db_ref):
        i = pl.program_id(1)
        first, last = i == 0, i == n_i - 1
        rows = slice(HALO, HALO + ts)
        uge = _ext(up_ref[0], u_ref[0], un_ref[0], first, last)
        uve = _ext(up_ref[1], u_ref[1], un_ref[1], first, last)
        dae = _ext(dan_ref[...], da_ref[...], dan_ref[...], True, last)
        cg = _conv_e(uge, w_ref[0]) + b_ref[0]
        cv = _conv_e(uve, w_ref[1]) + b_ref[1]
        sg = jax.nn.sigmoid(cg)
        dcg = dae * cv * (sg * (1.0 + cg * (1.0 - sg)))
        dcv = dae * (cg * sg)
        du_ref[0] = _conv_t_e(dcg, w_ref[0])[rows].astype(du_ref.dtype)
        du_ref[1] = _conv_t_e(dcv, w_ref[1])[rows].astype(du_ref.dtype)

        @pl.when(first)
        def _():
            dw_ref[...] = jnp.zeros_like(dw_ref)
            db_ref[...] = jnp.zeros_like(db_ref)

        for h, (de, xe) in enumerate(((dcg, uge), (dcv, uve))):
            dws = _conv_dw(de, xe, ts)
            for k in range(3):
                dw_ref[h, k:k + 1, :] += dws[k]
            db_ref[h] += jnp.sum(de[rows], axis=0, keepdims=True)

    cur, prev, nxt = _halo_specs(2, s, ts, tc)
    hb = ts // HALO
    last_h = s // HALO - 1
    wspec = pl.BlockSpec((2, 3, tc), lambda j, i: (0, 0, j))
    bspec = pl.BlockSpec((2, 1, tc), lambda j, i: (0, 0, j))
    return pl.pallas_call(
        body, grid=(f // tc, n_i), name=name,
        in_specs=[cur, prev, nxt,
                  pl.BlockSpec((ts, tc), lambda j, i: (i, j)),
                  pl.BlockSpec((HALO, tc), lambda j, i: (jnp.minimum((i + 1) * hb, last_h), j)),
                  wspec, bspec],
        out_specs=(cur, wspec, bspec),
        out_shape=(jax.ShapeDtypeStruct((2, s, f), BF16), jax.ShapeDtypeStruct((2, 3, f), F32),
                   jax.ShapeDtypeStruct((2, 1, f), F32)),
        compiler_params=_params("parallel", "arbitrary"),
    )(u, u, u, da, da, w, b)


_INV_SQRT2 = 0.7071067811865476
_INV_SQRT_2PI = 0.3989422804014327


def _gelu(x):
    return 0.5 * x * (1.0 + lax.erf(x * _INV_SQRT2))


def _gelu_grad(x):
    return 0.5 * (1.0 + lax.erf(x * _INV_SQRT2)) + x * (_INV_SQRT_2PI * jnp.exp(-0.5 * x * x))


def _tril_bf16(w):
    t = lax.broadcasted_iota(jnp.int32, w.shape, 0)
    s = lax.broadcasted_iota(jnp.int32, w.shape, 1)
    return jnp.where(s <= t, w, 0.0).astype(BF16)


def _sgu_fwd(p, ln_g, ln_b, ws, bs, name):
    _, s, c = p.shape
    g_n = ws.shape[0]
    cg = c // g_n

    def body(p_ref, lg_ref, lb_ref, ws_ref, bs_ref, y_ref):
        u = _gelu(p_ref[0].astype(F32))
        v = _gelu(p_ref[1].astype(F32))
        mu = jnp.mean(v, axis=-1, keepdims=True)
        vc = v - mu
        rstd = lax.rsqrt(jnp.mean(vc * vc, axis=-1, keepdims=True) + EPS)
        vn = ((vc * rstd) * lg_ref[...] + lb_ref[...]).astype(BF16)
        for g in range(g_n):
            cols = slice(g * cg, (g + 1) * cg)
            mixed = jnp.dot(_tril_bf16(ws_ref[g]), vn[:, cols], preferred_element_type=F32) + bs_ref[g]
            y_ref[:, cols] = (u[:, cols] * mixed).astype(y_ref.dtype)

    vec = pl.BlockSpec((1, c), lambda i: (0, 0))
    return pl.pallas_call(
        body, grid=(s // CHUNK,), name=name,
        in_specs=[pl.BlockSpec((2, CHUNK, c), lambda i: (0, i, 0)), vec, vec,
                  pl.BlockSpec((g_n, CHUNK, CHUNK), lambda i: (0, 0, 0)),
                  pl.BlockSpec((g_n, CHUNK, 1), lambda i: (0, 0, 0))],
        out_specs=pl.BlockSpec((CHUNK, c), lambda i: (i, 0)),
        out_shape=jax.ShapeDtypeStruct((s, c), BF16),
        compiler_params=_params("parallel"),
    )(p, ln_g, ln_b, ws, bs)


def _sgu_bwd(p, dy, ln_g, ln_b, ws, ws_t, bs, name):
    _, s, c = p.shape
    g_n = ws.shape[0]
    cg = c // g_n

    def body(p_ref, dy_ref, lg_ref, lb_ref, ws_ref, wst_ref, bs_ref, dp_ref, dlg_ref, dlb_ref, dws_ref, dbs_ref,
             dvn_ref):
        i = pl.program_id(0)
        pu = p_ref[0].astype(F32)
        pv = p_ref[1].astype(F32)
        dyf = dy_ref[...].astype(F32)
        u = _gelu(pu)
        v = _gelu(pv)
        mu = jnp.mean(v, axis=-1, keepdims=True)
        vc = v - mu
        rstd = lax.rsqrt(jnp.mean(vc * vc, axis=-1, keepdims=True) + EPS)
        vhat = vc * rstd
        vn = (vhat * lg_ref[...] + lb_ref[...]).astype(BF16)

        @pl.when(i == 0)
        def _():
            dlg_ref[...] = jnp.zeros_like(dlg_ref)
            dlb_ref[...] = jnp.zeros_like(dlb_ref)
            dws_ref[...] = jnp.zeros_like(dws_ref)
            dbs_ref[...] = jnp.zeros_like(dbs_ref)

        t_i = lax.broadcasted_iota(jnp.int32, (CHUNK, CHUNK), 0)
        s_i = lax.broadcasted_iota(jnp.int32, (CHUNK, CHUNK), 1)
        for g in range(g_n):
            cols = slice(g * cg, (g + 1) * cg)
            mixed = jnp.dot(_tril_bf16(ws_ref[g]), vn[:, cols], preferred_element_type=F32) + bs_ref[g]
            dp_ref[0, :, cols] = (dyf[:, cols] * mixed * _gelu_grad(pu[:, cols])).astype(dp_ref.dtype)
            dmixed = dyf[:, cols] * u[:, cols]
            dmb = dmixed.astype(BF16)
            wt = jnp.where(t_i <= s_i, wst_ref[g], 0.0).astype(BF16)
            dvn_ref[:, cols] = jnp.dot(wt, dmb, preferred_element_type=F32)
            dwg = lax.dot_general(dmb, vn[:, cols], (((1,), (1,)), ((), ())), preferred_element_type=F32)
            dws_ref[g] += jnp.where(s_i <= t_i, dwg, 0.0)
            dbs_ref[g] += jnp.sum(dmixed, axis=1, keepdims=True)
        dvn = dvn_ref[...]
        dlg_ref[...] += jnp.sum(dvn * vhat, axis=0, keepdims=True)
        dlb_ref[...] += jnp.sum(dvn, axis=0, keepdims=True)
        dvh = dvn * lg_ref[...]
        dv = rstd * (dvh - jnp.mean(dvh, axis=-1, keepdims=True) - vhat * jnp.mean(dvh * vhat, axis=-1, keepdims=True))
        dp_ref[1] = (dv * _gelu_grad(pv)).astype(dp_ref.dtype)

    vec = pl.BlockSpec((1, c), lambda i: (0, 0))
    wspec = pl.BlockSpec((g_n, CHUNK, CHUNK), lambda i: (0, 0, 0))
    bspec = pl.BlockSpec((g_n, CHUNK, 1), lambda i: (0, 0, 0))
    pspec = pl.BlockSpec((2, CHUNK, c), lambda i: (0, i, 0))
    return pl.pallas_call(
        body, grid=(s // CHUNK,), name=name,
        in_specs=[pspec, pl.BlockSpec((CHUNK, c), lambda i: (i, 0)), vec, vec, wspec, wspec, bspec],
        out_specs=(pspec, vec, vec, wspec, bspec),
        out_shape=(jax.ShapeDtypeStruct((2, s, c), BF16), jax.ShapeDtypeStruct((1, c), F32),
                   jax.ShapeDtypeStruct((1, c), F32), jax.ShapeDtypeStruct((g_n, CHUNK, CHUNK), F32),
                   jax.ShapeDtypeStruct((g_n, CHUNK, 1), F32)),
        scratch_shapes=[pltpu.VMEM((CHUNK, c), F32)],
        compiler_params=_params("arbitrary"),
    )(p, dy, ln_g, ln_b, ws, ws_t, bs)


def _split_dot(x, tri):
    hi = x.astype(BF16)
    lo = (x - hi.astype(F32)).astype(BF16)
    return jnp.dot(hi, tri, preferred_element_type=F32) + jnp.dot(lo, tri, preferred_element_type=F32)


def _sb_block(q, k, masked):
    scale = HEAD_DIM ** -0.5
    z = lax.dot_general(q, k, (((1,), (1,)), ((), ())), preferred_element_type=F32) * scale
    e = jnp.exp(-jnp.abs(z))
    lp = jnp.log1p(e)
    lb = jnp.minimum(z, 0.0) - lp
    l1 = jnp.minimum(-z, 0.0) - lp
    if masked:
        t_i = lax.broadcasted_iota(jnp.int32, z.shape, 0)
        s_i = lax.broadcasted_iota(jnp.int32, z.shape, 1)
        mask = s_i < t_i
        l1 = jnp.where(mask, l1, 0.0)
    else:
        mask = None
    return z, e, lb, l1, mask


def _tri(cond):
    j = lax.broadcasted_iota(jnp.int32, (CHUNK, CHUNK), 0)
    s = lax.broadcasted_iota(jnp.int32, (CHUNK, CHUNK), 1)
    return jnp.where(cond(j, s), 1.0, 0.0).astype(BF16)


def _sb_fwd(qkv, name):
    _, s, c = qkv.shape
    heads = c // HEAD_DIM
    nq = s // CHUNK

    def body(q_ref, k_ref, v_ref, o_ref):
        i = pl.program_id(1)
        q = q_ref[...]
        after = _tri(lambda j, s_: j > s_)

        def block(kb, carry, masked):
            o_acc, run = carry
            rows = pl.ds(pl.multiple_of(kb * CHUNK, CHUNK), CHUNK)
            _, _, lb, l1, mask = _sb_block(q, k_ref[rows, :], masked)
            acc = _split_dot(l1, after) + run
            a = jnp.exp(lb + acc)
            if masked:
                a = jnp.where(mask, a, 0.0)
            o_acc = o_acc + jnp.dot(a.astype(BF16), v_ref[rows, :], preferred_element_type=F32)
            return o_acc, run + jnp.sum(l1, axis=1, keepdims=True)

        carry = block(i, (jnp.zeros((CHUNK, HEAD_DIM), F32), jnp.zeros((CHUNK, 1), F32)), True)
        o_acc, _ = lax.fori_loop(0, i, lambda n, cr: block(i - 1 - n, cr, False), carry)
        o_ref[...] = o_acc.astype(o_ref.dtype)

    return pl.pallas_call(
        body, grid=(heads, nq), name=name,
        in_specs=[pl.BlockSpec((None, CHUNK, HEAD_DIM), lambda h, i: (0, i, h)),
                  pl.BlockSpec((None, s, HEAD_DIM), lambda h, i: (1, 0, h)),
                  pl.BlockSpec((None, s, HEAD_DIM), lambda h, i: (2, 0, h))],
        out_specs=pl.BlockSpec((CHUNK, HEAD_DIM), lambda h, i: (i, h)),
        out_shape=jax.ShapeDtypeStruct((s, c), BF16),
        compiler_params=_params("parallel", "arbitrary"),
    )(qkv, qkv, qkv)


def _sb_bwd(qkv, do, name):
    _, s, c = qkv.shape
    heads = c // HEAD_DIM
    nq = s // CHUNK
    scale = HEAD_DIM ** -0.5

    def body(q_ref, k_ref, v_ref, do_ref, dq_ref, dkv_ref, run_ref, dk_acc, dv_acc):
        i = pl.program_id(1)
        q = q_ref[...]
        dob = do_ref[...]
        after = _tri(lambda j, s_: j > s_)
        before = _tri(lambda j, s_: j < s_)

        @pl.when(i == 0)
        def _():
            dk_acc[...] = jnp.zeros_like(dk_acc)
            dv_acc[...] = jnp.zeros_like(dv_acc)

        def pre(kb, run, masked):
            rows = pl.ds(pl.multiple_of(kb * CHUNK, CHUNK), CHUNK)
            run_ref[kb] = run
            _, _, _, l1, _ = _sb_block(q, k_ref[rows, :], masked)
            return run + jnp.sum(l1, axis=1, keepdims=True)

        run0 = pre(i, jnp.zeros((CHUNK, 1), F32), True)
        lax.fori_loop(0, i, lambda n, r: pre(i - 1 - n, r, False), run0)

        def main(kb, carry, masked):
            dq_acc, csum = carry
            rows = pl.ds(pl.multiple_of(kb * CHUNK, CHUNK), CHUNK)
            kblk = k_ref[rows, :]
            vblk = v_ref[rows, :]
            z, e, lb, l1, mask = _sb_block(q, kblk, masked)
            acc = _split_dot(l1, after) + run_ref[kb]
            a = jnp.exp(lb + acc)
            if masked:
                a = jnp.where(mask, a, 0.0)
            d_a = lax.dot_general(dob, vblk, (((1,), (1,)), ((), ())), preferred_element_type=F32)
            ee = a * d_a
            cc = _split_dot(ee, before) + csum
            beta = jnp.where(z >= 0.0, 1.0, e) / (1.0 + e)
            dz = (ee * (1.0 - beta) - cc * beta) * scale
            if masked:
                dz = jnp.where(mask, dz, 0.0)
            dzb = dz.astype(BF16)
            dq_acc = dq_acc + jnp.dot(dzb, kblk, preferred_element_type=F32)
            dk_acc[rows, :] += jnp.dot(dz.T.astype(BF16), q, preferred_element_type=F32)
            dv_acc[rows, :] += jnp.dot(a.T.astype(BF16), dob, preferred_element_type=F32)
            return dq_acc, csum + jnp.sum(ee, axis=1, keepdims=True)

        carry = (jnp.zeros((CHUNK, HEAD_DIM), F32), jnp.zeros((CHUNK, 1), F32))
        carry = lax.fori_loop(0, i, lambda kb, cr: main(kb, cr, False), carry)
        dq_acc, _ = main(i, carry, True)
        dq_ref[...] = dq_acc.astype(dq_ref.dtype)

        @pl.when(i == nq - 1)
        def _():
            dkv_ref[0] = dk_acc[...].astype(dkv_ref.dtype)
            dkv_ref[1] = dv_acc[...].astype(dkv_ref.dtype)

    blk = pl.BlockSpec((CHUNK, HEAD_DIM), lambda h, i: (i, h))
    return pl.pallas_call(
        body, grid=(heads, nq), name=name,
        in_specs=[pl.BlockSpec((None, CHUNK, HEAD_DIM), lambda h, i: (0, i, h)),
                  pl.BlockSpec((None, s, HEAD_DIM), lambda h, i: (1, 0, h)),
                  pl.BlockSpec((None, s, HEAD_DIM), lambda h, i: (2, 0, h)),
                  blk],
        out_specs=(blk, pl.BlockSpec((2, s, HEAD_DIM), lambda h, i: (0, 0, h))),
        out_shape=(jax.ShapeDtypeStruct((s, c), BF16), jax.ShapeDtypeStruct((2, s, c), BF16)),
        scratch_shapes=[pltpu.VMEM((nq, CHUNK, 1), F32), pltpu.VMEM((s, HEAD_DIM), F32),
                        pltpu.VMEM((s, HEAD_DIM), F32)],
        compiler_params=_params("parallel", "arbitrary"),
    )(qkv, qkv, qkv, do)


def _shard_of(ref, dev, axis, r, c):
    if axis == 0:
        return ref.at[pl.ds(dev * r, r), :]
    return ref.at[:, pl.ds(dev * c, c)]


def _all_gather(x, axis, name):
    r, c = x.shape
    full = (N_DEV * r, c) if axis == 0 else (r, N_DEV * c)

    def body(x_ref, out_ref, send_sems, recv_sems, local_sem):
        mx, my, mc = lax.axis_index("x"), lax.axis_index("y"), lax.axis_index("c")
        me, sibling = (mx, my, mc), (mx, my, 1 - mc)
        chips = [(1 - mx, my), (mx, 1 - my), (1 - mx, 1 - my)]

        def rows(px, py, pc):
            return _shard_of(out_ref, 4 * px + 2 * py + pc, axis, r, c)

        def copy(k, block, to, src=None):
            return pltpu.make_async_remote_copy(
                src_ref=rows(*block) if src is None else src, dst_ref=rows(*block),
                send_sem=send_sems.at[k], recv_sem=recv_sems.at[k], device_id=to, device_id_type=MESH)

        mine = pltpu.make_async_copy(x_ref, rows(*me), local_sem)
        mine.start()
        first = [copy(0, me, sibling, src=x_ref)]
        first += [copy(1 + j, me, (*chip, mc), src=x_ref) for j, chip in enumerate(chips)]
        for cp in first:
            cp.start()
        passed = [copy(4 + j, (*chip, mc), sibling) for j, chip in enumerate(chips)]
        for j, chip in enumerate(chips):
            copy(1 + j, (*chip, mc), me).wait_recv()
            passed[j].start()
        copy(0, sibling, me).wait_recv()
        for j, chip in enumerate(chips):
            copy(4 + j, (*chip, 1 - mc), me).wait_recv()
        for cp in first + passed:
            cp.wait_send()
        mine.wait()

    return pl.pallas_call(
        body, name=name,
        out_shape=jax.ShapeDtypeStruct(full, x.dtype),
        in_specs=[ANY], out_specs=ANY,
        scratch_shapes=[pltpu.SemaphoreType.DMA((7,)), pltpu.SemaphoreType.DMA((7,)), pltpu.SemaphoreType.DMA(())],
    )(x)


def _rs_core_exchange(g, axis, name):
    r, c = (g.shape[0] // N_DEV, g.shape[1]) if axis == 0 else (g.shape[0], g.shape[1] // N_DEV)

    def body(g_ref, recv_ref, send_sems, recv_sems):
        mx, my, mc = lax.axis_index("x"), lax.axis_index("y"), lax.axis_index("c")
        copies = [pltpu.make_async_remote_copy(
            src_ref=_shard_of(g_ref, 2 * k + (1 - mc), axis, r, c), dst_ref=recv_ref.at[k],
            send_sem=send_sems.at[k], recv_sem=recv_sems.at[k], device_id=(mx, my, 1 - mc), device_id_type=MESH)
            for k in range(4)]
        for cp in copies:
            cp.start()
        for cp in copies:
            cp.wait()

    return pl.pallas_call(
        body, name=name,
        out_shape=jax.ShapeDtypeStruct((4, r, c), g.dtype),
        in_specs=[ANY], out_specs=ANY,
        scratch_shapes=[pltpu.SemaphoreType.DMA((4,)), pltpu.SemaphoreType.DMA((4,))],
    )(g)


def _ew_tiles(r, c):
    return _tile(r, (256, 128, 64, 32, 16, 8)), _tile(c, (1024, 512, 256, 128))


def _rs_core_add(g, recv, core, axis, name):
    _, r, c = recv.shape
    tr, tc = _ew_tiles(r, c)
    nrb, ncb = r // tr, c // tc

    def body(core_ref, g_ref, x_ref, o_ref):
        del core_ref
        o_ref[...] = (g_ref[...].astype(F32) + x_ref[...].astype(F32)).astype(o_ref.dtype)

    if axis == 0:
        g_spec = pl.BlockSpec((tr, tc), lambda k, i, j, core: ((2 * k + core[0]) * nrb + i, j))
    else:
        g_spec = pl.BlockSpec((tr, tc), lambda k, i, j, core: (i, (2 * k + core[0]) * ncb + j))
    blk = pl.BlockSpec((None, tr, tc), lambda k, i, j, core: (k, i, j))
    return pl.pallas_call(
        body, name=name,
        grid_spec=pltpu.PrefetchScalarGridSpec(
            num_scalar_prefetch=1, grid=(4, nrb, ncb), in_specs=[g_spec, blk], out_specs=blk),
        out_shape=jax.ShapeDtypeStruct(recv.shape, BF16),
        compiler_params=_params("parallel", "parallel", "parallel"),
    )(core, g, recv)


def _rs_chip_exchange(p, name):
    _, r, c = p.shape

    def body(p_ref, recv_ref, send_sems, recv_sems):
        mx, my, mc = lax.axis_index("x"), lax.axis_index("y"), lax.axis_index("c")
        chips = [(1 - mx, my), (mx, 1 - my), (1 - mx, 1 - my)]
        copies = [pltpu.make_async_remote_copy(
            src_ref=p_ref.at[2 * px + py], dst_ref=recv_ref.at[j],
            send_sem=send_sems.at[j], recv_sem=recv_sems.at[j], device_id=(px, py, mc), device_id_type=MESH)
            for j, (px, py) in enumerate(chips)]
        for cp in copies:
            cp.start()
        for cp in copies:
            cp.wait()

    return pl.pallas_call(
        body, name=name,
        out_shape=jax.ShapeDtypeStruct((3, r, c), p.dtype),
        in_specs=[ANY], out_specs=ANY,
        scratch_shapes=[pltpu.SemaphoreType.DMA((3,)), pltpu.SemaphoreType.DMA((3,))],
    )(p)


def _adamw_math(w, g, m, v):
    m = ADAM_B1 * m + (1.0 - ADAM_B1) * g
    v = ADAM_B2 * v + (1.0 - ADAM_B2) * (g * g)
    m_hat = m / (1.0 - ADAM_B1 ** ADAM_STEP)
    v_hat = v / (1.0 - ADAM_B2 ** ADAM_STEP)
    delta = -ADAM_LR * (m_hat / (jnp.sqrt(v_hat) + ADAM_EPS) + ADAM_WD * w)
    return delta, m, v


def _adamw_shard(w, m, v, p, recv, chip, name):
    r, c = w.shape
    tr, tc = _ew_tiles(r, c)

    def body(chip_ref, w_ref, m_ref, v_ref, p_ref, x_ref, g_out, d_out, m_out, v_out):
        del chip_ref
        g = p_ref[...].astype(F32)
        for j in range(3):
            g = g + x_ref[j].astype(F32)
        d, mn, vn = _adamw_math(w_ref[...], g, m_ref[...], v_ref[...])
        g_out[...] = g
        d_out[...] = d
        m_out[...] = mn
        v_out[...] = vn

    blk = pl.BlockSpec((tr, tc), lambda i, j, chip: (i, j))
    out = jax.ShapeDtypeStruct((r, c), F32)
    return pl.pallas_call(
        body, name=name,
        grid_spec=pltpu.PrefetchScalarGridSpec(
            num_scalar_prefetch=1, grid=(r // tr, c // tc),
            in_specs=[blk, blk, blk, pl.BlockSpec((None, tr, tc), lambda i, j, chip: (chip[0], i, j)),
                      pl.BlockSpec((3, tr, tc), lambda i, j, chip: (0, i, j))],
            out_specs=(blk, blk, blk, blk)),
        out_shape=(out, out, out, out),
        compiler_params=_params("parallel", "parallel"),
    )(chip, w, m, v, p, recv)


def _adamw_packed(w, m, v, g, name):
    r, c = w.shape
    tr = _tile(r, (512, 256, 128, 64, 32, 16, 8))

    def body(w_ref, m_ref, v_ref, g_ref, d_out, m_out, v_out):
        d, mn, vn = _adamw_math(w_ref[...], g_ref[...], m_ref[...], v_ref[...])
        d_out[...] = d
        m_out[...] = mn
        v_out[...] = vn

    blk = pl.BlockSpec((tr, c), lambda i: (i, 0))
    out = jax.ShapeDtypeStruct((r, c), F32)
    return pl.pallas_call(
        body, grid=(r // tr,), name=name, in_specs=[blk, blk, blk, blk], out_specs=(blk, blk, blk),
        out_shape=(out, out, out), compiler_params=_params("parallel"),
    )(w, m, v, g)


def _sum_devices(x, name):
    r = x.shape[0] // N_DEV
    tr = _tile(r, (512, 256, 128, 64, 32, 16, 8))

    def body(x_ref, o_ref):
        acc = x_ref[0]
        for d in range(1, N_DEV):
            acc = acc + x_ref[d]
        o_ref[...] = acc

    return pl.pallas_call(
        body, grid=(r // tr,), name=name,
        in_specs=[pl.BlockSpec((N_DEV, tr, LANES), lambda i: (0, i, 0))],
        out_specs=pl.BlockSpec((tr, LANES), lambda i: (i, 0)),
        out_shape=jax.ShapeDtypeStruct((r, LANES), F32),
        compiler_params=_params("parallel"),
    )(x.reshape(N_DEV, r, LANES))


def _pack(arrays):
    flat = []
    for a in arrays:
        f = a.reshape(-1).astype(F32)
        flat.append(jnp.pad(f, (0, (-f.shape[0]) % LANES)))
    f = jnp.concatenate(flat)
    f = jnp.pad(f, (0, (-f.shape[0]) % (8 * LANES)))
    return f.reshape(-1, LANES)


def _unpack(packed, shapes):
    flat = packed.reshape(-1)
    out, off = [], 0
    for shp in shapes:
        n = 1
        for d in shp:
            n *= d
        out.append(flat[off:off + n].reshape(shp))
        off += n + (-n) % LANES
    return out


def kernel(x, norm_mix_pre, norm_mix_post, norm_ffn_pre, norm_ffn_post, sc_w_in, sc_conv_w, sc_w_out, sg_w_in, sg_ln_g, sg_ln_b, sg_w_s, sg_b_s, sg_w_out, sb_w_qkv, sb_w_out, ffn_w_up, ffn_conv_w, ffn_conv_b, ffn_w_down, loss_target, m_norm_mix_pre, m_norm_mix_post, m_norm_ffn_pre, m_norm_ffn_post, m_sc_w_in, m_sc_conv_w, m_sc_w_out, m_sg_w_in, m_sg_ln_g, m_sg_ln_b, m_sg_w_s, m_sg_b_s, m_sg_w_out, m_sb_w_qkv, m_sb_w_out, m_ffn_w_up, m_ffn_conv_w, m_ffn_conv_b, m_ffn_w_down, v_norm_mix_pre, v_norm_mix_post, v_norm_ffn_pre, v_norm_ffn_post, v_sc_w_in, v_sc_conv_w, v_sc_w_out, v_sg_w_in, v_sg_ln_g, v_sg_ln_b, v_sg_w_s, v_sg_b_s, v_sg_w_out, v_sb_w_qkv, v_sb_w_out, v_ffn_w_up, v_ffn_conv_w, v_ffn_conv_b, v_ffn_w_down):
    weights = dict(norm_mix_pre=norm_mix_pre, norm_mix_post=norm_mix_post, norm_ffn_pre=norm_ffn_pre,
                   norm_ffn_post=norm_ffn_post, sc_w_in=sc_w_in, sc_conv_w=sc_conv_w, sc_w_out=sc_w_out,
                   sg_w_in=sg_w_in, sg_ln_g=sg_ln_g, sg_ln_b=sg_ln_b, sg_w_s=sg_w_s, sg_b_s=sg_b_s,
                   sg_w_out=sg_w_out, sb_w_qkv=sb_w_qkv, sb_w_out=sb_w_out, ffn_w_up=ffn_w_up,
                   ffn_conv_w=ffn_conv_w, ffn_conv_b=ffn_conv_b, ffn_w_down=ffn_w_down)
    mom1 = dict(norm_mix_pre=m_norm_mix_pre, norm_mix_post=m_norm_mix_post, norm_ffn_pre=m_norm_ffn_pre,
                norm_ffn_post=m_norm_ffn_post, sc_w_in=m_sc_w_in, sc_conv_w=m_sc_conv_w, sc_w_out=m_sc_w_out,
                sg_w_in=m_sg_w_in, sg_ln_g=m_sg_ln_g, sg_ln_b=m_sg_ln_b, sg_w_s=m_sg_w_s, sg_b_s=m_sg_b_s,
                sg_w_out=m_sg_w_out, sb_w_qkv=m_sb_w_qkv, sb_w_out=m_sb_w_out, ffn_w_up=m_ffn_w_up,
                ffn_conv_w=m_ffn_conv_w, ffn_conv_b=m_ffn_conv_b, ffn_w_down=m_ffn_w_down)
    mom2 = dict(norm_mix_pre=v_norm_mix_pre, norm_mix_post=v_norm_mix_post, norm_ffn_pre=v_norm_ffn_pre,
                norm_ffn_post=v_norm_ffn_post, sc_w_in=v_sc_w_in, sc_conv_w=v_sc_conv_w, sc_w_out=v_sc_w_out,
                sg_w_in=v_sg_w_in, sg_ln_g=v_sg_ln_g, sg_ln_b=v_sg_ln_b, sg_w_s=v_sg_w_s, sg_b_s=v_sg_b_s,
                sg_w_out=v_sg_w_out, sb_w_qkv=v_sb_w_qkv, sb_w_out=v_sb_w_out, ffn_w_up=v_ffn_w_up,
                ffn_conv_w=v_ffn_conv_w, ffn_conv_b=v_ffn_conv_b, ffn_w_down=v_ffn_w_down)
    order = list(weights)

    mx, my, mc = lax.axis_index("x"), lax.axis_index("y"), lax.axis_index("c")
    dev = 4 * mx + 2 * my + mc
    core_arr = jnp.reshape(mc, (1,)).astype(jnp.int32)
    chip_arr = jnp.reshape(2 * mx + my, (1,)).astype(jnp.int32)

    h = x[0]
    target = loss_target[0]
    depth = norm_mix_pre.shape[0]
    d_model = h.shape[1]
    f_loc = ffn_w_down.shape[1]
    d_ff = f_loc * N_DEV

    def gather_rows8(a2d, name):
        rows = a2d.shape[0]
        pad = (-rows) % 8
        return _all_gather(jnp.pad(a2d, ((0, pad), (0, 0))), 1, name)[:rows]

    n_a = sc_w_in.shape[0]
    sc_conv_full = gather_rows8(sc_conv_w.reshape(n_a * 3, -1), "ag_sc_conv").reshape(n_a, 3, d_model)
    ffn_conv_full = gather_rows8(ffn_conv_w.reshape(depth * 3, -1), "ag_ffn_conv").reshape(depth, 3, 2 * d_ff)

    big = {}

    def gather_big(name, j, axis):
        big[(name, j)] = _all_gather(weights[name][j].astype(BF16), axis, f"ag_{name}")

    kinds = [i % 3 for i in range(depth)]
    for i in range(depth):
        j = i // 3
        if kinds[i] == 0:
            gather_big("sc_w_in", j, 1)
            gather_big("sc_w_out", j, 0)
        elif kinds[i] == 1:
            gather_big("sg_w_in", j, 1)
            gather_big("sg_w_out", j, 0)
        else:
            gather_big("sb_w_qkv", j, 1)
            gather_big("sb_w_out", j, 0)
        gather_big("ffn_w_up", i, 1)
        gather_big("ffn_w_down", i, 0)

    sg_ws_t = jnp.swapaxes(sg_w_s, -1, -2)
    sg_bs3 = sg_b_s[..., None]

    saved = []
    for i in range(depth):
        j = i // 3
        sv = {"h_in": h}
        hn = _rms_fwd(h, norm_mix_pre[i:i + 1], "rms_fwd")
        sv["hn"] = hn
        if kinds[i] == 0:
            p = _mm_nn(hn, big[("sc_w_in", j)], BF16, 3, "sc_in_proj")
            y = _sc_gate_fwd(p, sc_conv_full[j], "sc_gate_fwd")
            m = _mm_nn(y, big[("sc_w_out", j)], F32, 1, "mix_out_proj")
        elif kinds[i] == 1:
            p = _mm_nn(hn, big[("sg_w_in", j)], BF16, 2, "sg_in_proj")
            y = _sgu_fwd(p, sg_ln_g[j:j + 1], sg_ln_b[j:j + 1], sg_w_s[j], sg_bs3[j], "sgu_fwd")
            m = _mm_nn(y, big[("sg_w_out", j)], F32, 1, "mix_out_proj")
        else:
            p = _mm_nn(hn, big[("sb_w_qkv", j)], BF16, 3, "sb_qkv_proj")
            y = _sb_fwd(p, "sb_fwd")
            m = _mm_nn(y, big[("sb_w_out", j)], F32, 1, "mix_out_proj")
        sv.update(p=p, y=y, m=m)
        h = _resid_rms_fwd(h, m, norm_mix_post[i:i + 1], "resid_rms_fwd")
        sv["h_mid"] = h
        hn2 = _rms_fwd(h, norm_ffn_pre[i:i + 1], "rms_fwd")
        u = _mm_nn(hn2, big[("ffn_w_up", i)], BF16, 2, "ffn_up_proj")
        cw = ffn_conv_full[i].reshape(3, 2, d_ff).transpose(1, 0, 2)
        cb = ffn_conv_b[i].reshape(2, 1, d_ff)
        a = _ffn_act_fwd(u, cw, cb, "ffn_act_fwd")
        f = _mm_nn(a, big[("ffn_w_down", i)], F32, 1, "ffn_down_proj")
        sv.update(hn2=hn2, u=u, a=a, f=f, cw=cw, cb=cb)
        h = _resid_rms_fwd(h, f, norm_ffn_post[i:i + 1], "resid_rms_fwd")
        saved.append(sv)

    dh, loss_part = _loss_head(h, target, "loss_head")

    grads_big = {}
    small = {}

    def reduce_scatter(name, j, g_full, axis):
        recv1 = _rs_core_exchange(g_full, axis, f"rs1_{name}")
        part = _rs_core_add(g_full, recv1, core_arr, axis, f"rs_add_{name}")
        recv2 = _rs_chip_exchange(part, f"rs2_{name}")
        grads_big[(name, j)] = (part, recv2)

    for i in reversed(range(depth)):
        j = i // 3
        sv = saved[i]
        df, dg = _rms_bwd(sv["f"], norm_ffn_post[i:i + 1], dh, None, BF16, "rms_bwd_post")
        small.setdefault("norm_ffn_post", {})[i] = dg
        da = _mm_nt(df, big[("ffn_w_down", i)], BF16, "ffn_down_dx")
        reduce_scatter("ffn_w_down", i, _mm_tn(sv["a"], df, BF16, "ffn_down_dw"), 0)
        du, dcw, dcb = _ffn_act_bwd(sv["u"], da, sv["cw"], sv["cb"], "ffn_act_bwd")
        small.setdefault("ffn_conv_w", {})[i] = dcw.transpose(1, 0, 2).reshape(3, 2 * d_ff)
        small.setdefault("ffn_conv_b", {})[i] = dcb.reshape(2 * d_ff)
        dhn2 = _mm_nt(du, big[("ffn_w_up", i)], F32, "ffn_up_dx")
        reduce_scatter("ffn_w_up", i, _mm_tn(sv["hn2"], du, BF16, "ffn_up_dw"), 1)
        dh, dg = _rms_bwd(sv["h_mid"], norm_ffn_pre[i:i + 1], dhn2, dh, F32, "rms_bwd_pre")
        small.setdefault("norm_ffn_pre", {})[i] = dg
        dm, dg = _rms_bwd(sv["m"], norm_mix_post[i:i + 1], dh, None, BF16, "rms_bwd_post")
        small.setdefault("norm_mix_post", {})[i] = dg
        if kinds[i] == 0:
            wo, wi = "sc_w_out", "sc_w_in"
        elif kinds[i] == 1:
            wo, wi = "sg_w_out", "sg_w_in"
        else:
            wo, wi = "sb_w_out", "sb_w_qkv"
        dy = _mm_nt(dm, big[(wo, j)], BF16, "mix_out_dx")
        reduce_scatter(wo, j, _mm_tn(sv["y"], dm, BF16, "mix_out_dw"), 0)
        if kinds[i] == 0:
            dp, dcw = _sc_gate_bwd(sv["p"], dy, sc_conv_full[j], "sc_gate_bwd")
            small.setdefault("sc_conv_w", {})[j] = dcw
        elif kinds[i] == 1:
            dp, dlg, dlb, dws, dbs = _sgu_bwd(sv["p"], dy, sg_ln_g[j:j + 1], sg_ln_b[j:j + 1], sg_w_s[j], sg_ws_t[j],
                                              sg_bs3[j], "sgu_bwd")
            small.setdefault("sg_ln_g", {})[j] = dlg[0]
            small.setdefault("sg_ln_b", {})[j] = dlb[0]
            small.setdefault("sg_w_s", {})[j] = dws
            small.setdefault("sg_b_s", {})[j] = dbs[..., 0]
        else:
            dq, dkv = _sb_bwd(sv["p"], dy, "sb_bwd")
            dp = jnp.concatenate([dq[None], dkv], axis=0)
        dhn = _mm_nt(dp, big[(wi, j)], F32, "mix_in_dx")
        reduce_scatter(wi, j, _mm_tn(sv["hn"], dp, BF16, "mix_in_dw"), 1)
        dh, dg = _rms_bwd(sv["h_in"], norm_mix_pre[i:i + 1], dhn, dh, F32, "rms_bwd_pre")
        small.setdefault("norm_mix_pre", {})[i] = dg

    grad_x = dh[None]

    small_names = ["norm_mix_pre", "norm_mix_post", "norm_ffn_pre", "norm_ffn_post", "sg_ln_g", "sg_ln_b", "sg_w_s",
                   "sg_b_s", "ffn_conv_b", "sc_conv_w", "ffn_conv_w"]
    full_shapes = {n: weights[n].shape for n in small_names}
    full_shapes["sc_conv_w"] = (n_a, 3, d_model)
    full_shapes["ffn_conv_w"] = (depth, 3, 2 * d_ff)
    partial = [jnp.stack([small[n][k].reshape(full_shapes[n][1:]) for k in sorted(small[n])]) for n in small_names]
    packed = _pack(partial + [loss_part])
    gathered = _all_gather(packed, 0, "ag_small_grads")
    summed = _sum_devices(gathered, "sum_small_grads")
    pieces = _unpack(summed, [full_shapes[n] for n in small_names] + [(1, 1)])
    loss = pieces[-1].reshape(())
    small_grads = dict(zip(small_names, pieces[:-1]))
    for n in ("sc_conv_w", "ffn_conv_w"):
        c_loc = weights[n].shape[-1]
        small_grads[n] = lax.dynamic_slice_in_dim(small_grads[n], dev * c_loc, c_loc, axis=2)
    shapes = [weights[n].shape for n in small_names]
    d_p, m_p, v_p = _adamw_packed(_pack([weights[n] for n in small_names]), _pack([mom1[n] for n in small_names]),
                                  _pack([mom2[n] for n in small_names]), _pack([small_grads[n] for n in small_names]),
                                  "adamw_small")
    out_g, out_d, out_m, out_v = dict(small_grads), {}, {}, {}
    for n, d_, m_, v_ in zip(small_names, _unpack(d_p, shapes), _unpack(m_p, shapes), _unpack(v_p, shapes)):
        out_d[n], out_m[n], out_v[n] = d_, m_, v_

    for n in order:
        if n in small_names:
            continue
        res = [_adamw_shard(weights[n][j], mom1[n][j], mom2[n][j], *grads_big[(n, j)], chip_arr, f"adamw_{n}")
               for j in range(weights[n].shape[0])]
        out_g[n], out_d[n], out_m[n], out_v[n] = (jnp.stack([r_[t] for r_ in res]) for t in range(4))

    return (loss, grad_x, *[out_g[n] for n in order], *[out_d[n] for n in order],
            *[out_m[n] for n in order], *[out_v[n] for n in order])
```

```python
import functools

import jax
import jax.numpy as jnp
from jax import lax
from jax.experimental import pallas as pl
from jax.experimental.pallas import tpu as pltpu

F32 = jnp.float32
BF16 = jnp.bfloat16

EPS = 1e-6
CHUNK = 128
HEAD_DIM = 128
SG_GROUPS = 8
HALO = 16
N_DEV = 8
LANES = 128
V7X_VMEM_LIMIT = 56 * 1024 * 1024

ADAM_LR = 0.001
ADAM_B1 = 0.9
ADAM_B2 = 0.999
ADAM_EPS = 1e-08
ADAM_WD = 0.01
ADAM_STEP = 10

MESH = pl.DeviceIdType.MESH
ANY = pl.BlockSpec(memory_space=pl.ANY)


def _tile(n, prefs):
    for p in prefs:
        if p <= n and n % p == 0:
            return p
    return n


def _params(*sem):
    return pltpu.CompilerParams(dimension_semantics=sem, vmem_limit_bytes=V7X_VMEM_LIMIT)


def _mm_nn(a, b, out_dtype, nsplit, name):
    m, k = a.shape
    n = b.shape[1]
    w = n // nsplit
    tm = _tile(m, (1024, 512, 256, 128)) if k <= 2048 else _tile(m, (512, 256, 128))
    tn = _tile(w, (512, 256, 128))
    per = w // tn

    def body(a_ref, b_ref, o_ref):
        o_ref[...] = jnp.dot(a_ref[...], b_ref[...], preferred_element_type=F32).astype(o_ref.dtype)

    if nsplit == 1:
        out_shape = jax.ShapeDtypeStruct((m, n), out_dtype)
        out_spec = pl.BlockSpec((tm, tn), lambda i, j: (i, j))
    else:
        out_shape = jax.ShapeDtypeStruct((nsplit, m, w), out_dtype)
        out_spec = pl.BlockSpec((None, tm, tn), lambda i, j: (j // per, i, j % per))
    return pl.pallas_call(
        body, grid=(m // tm, n // tn), name=name,
        in_specs=[pl.BlockSpec((tm, k), lambda i, j: (i, 0)), pl.BlockSpec((k, tn), lambda i, j: (0, j))],
        out_specs=out_spec, out_shape=out_shape,
        compiler_params=_params("parallel", "arbitrary"),
    )(a, b)


def _mm_nt(a, b, out_dtype, name):
    a3 = a if a.ndim == 3 else a[None]
    ns, m, w = a3.shape
    ko, n = b.shape
    assert n == ns * w
    tm = _tile(m, (512, 256, 128))
    tko = _tile(ko, (256, 128)) if n > 4096 else _tile(ko, (512, 256, 128))

    def body(a_ref, b_ref, o_ref):
        acc = None
        for s in range(ns):
            p = lax.dot_general(a_ref[s], b_ref[:, s * w:(s + 1) * w], (((1,), (1,)), ((), ())),
                                preferred_element_type=F32)
            acc = p if acc is None else acc + p
        o_ref[...] = acc.astype(o_ref.dtype)

    return pl.pallas_call(
        body, grid=(m // tm, ko // tko), name=name,
        in_specs=[pl.BlockSpec((ns, tm, w), lambda i, j: (0, i, 0)), pl.BlockSpec((tko, n), lambda i, j: (j, 0))],
        out_specs=pl.BlockSpec((tm, tko), lambda i, j: (i, j)),
        out_shape=jax.ShapeDtypeStruct((m, ko), out_dtype),
        compiler_params=_params("parallel", "arbitrary"),
    )(a3, b)


def _mm_tn(a, b, out_dtype, name):
    b3 = b if b.ndim == 3 else b[None]
    ns, m, w = b3.shape
    k = a.shape[1]
    n = ns * w
    tk = _tile(k, (512, 256, 128))
    tn = _tile(w, (2816, 2048, 1024, 512, 256, 128))
    ts = _tile(m, (1024, 512, 256, 128))
    per = w // tn
    n_s = m // ts

    def body(a_ref, b_ref, o_ref, acc_ref):
        s = pl.program_id(2)

        @pl.when(s == 0)
        def _():
            acc_ref[...] = jnp.zeros_like(acc_ref)

        acc_ref[...] += lax.dot_general(a_ref[...], b_ref[...], (((0,), (0,)), ((), ())),
                                        preferred_element_type=F32)

        @pl.when(s == n_s - 1)
        def _():
            o_ref[...] = acc_ref[...].astype(o_ref.dtype)

    return pl.pallas_call(
        body, grid=(k // tk, n // tn, n_s), name=name,
        in_specs=[pl.BlockSpec((ts, tk), lambda i, j, s: (s, i)),
                  pl.BlockSpec((None, ts, tn), lambda i, j, s: (j // per, s, j % per))],
        out_specs=pl.BlockSpec((tk, tn), lambda i, j, s: (i, j)),
        out_shape=jax.ShapeDtypeStruct((k, n), out_dtype),
        scratch_shapes=[pltpu.VMEM((tk, tn), F32)],
        compiler_params=_params("parallel", "parallel", "arbitrary"),
    )(a, b3)


def _rms_rows(s):
    return _tile(s, (256, 128))


def _rms_fwd(x, g, name):
    s, d = x.shape
    ts = _rms_rows(s)

    def body(x_ref, g_ref, o_ref):
        xf = x_ref[...]
        r = lax.rsqrt(jnp.mean(xf * xf, axis=-1, keepdims=True) + EPS)
        o_ref[...] = ((xf * r) * g_ref[...]).astype(o_ref.dtype)

    return pl.pallas_call(
        body, grid=(s // ts,), name=name,
        in_specs=[pl.BlockSpec((ts, d), lambda i: (i, 0)), pl.BlockSpec((1, d), lambda i: (0, 0))],
        out_specs=pl.BlockSpec((ts, d), lambda i: (i, 0)),
        out_shape=jax.ShapeDtypeStruct((s, d), BF16),
        compiler_params=_params("parallel"),
    )(x, g)


def _resid_rms_fwd(h, m, g, name):
    s, d = h.shape
    ts = _rms_rows(s)

    def body(h_ref, m_ref, g_ref, o_ref):
        mf = m_ref[...]
        r = lax.rsqrt(jnp.mean(mf * mf, axis=-1, keepdims=True) + EPS)
        o_ref[...] = h_ref[...] + (mf * r) * g_ref[...]

    return pl.pallas_call(
        body, grid=(s // ts,), name=name,
        in_specs=[pl.BlockSpec((ts, d), lambda i: (i, 0)), pl.BlockSpec((ts, d), lambda i: (i, 0)),
                  pl.BlockSpec((1, d), lambda i: (0, 0))],
        out_specs=pl.BlockSpec((ts, d), lambda i: (i, 0)),
        out_shape=jax.ShapeDtypeStruct((s, d), F32),
        compiler_params=_params("parallel"),
    )(h, m, g)


def _rms_bwd(x, g, dy, dres, out_dtype, name):
    s, d = x.shape
    ts = _rms_rows(s)
    has_res = dres is not None

    def body(*refs):
        if has_res:
            x_ref, g_ref, dy_ref, dres_ref, dx_ref, dg_ref = refs
        else:
            x_ref, g_ref, dy_ref, dx_ref, dg_ref = refs
        i = pl.program_id(0)
        xf = x_ref[...]
        dyf = dy_ref[...].astype(F32)
        r = lax.rsqrt(jnp.mean(xf * xf, axis=-1, keepdims=True) + EPS)
        xh = xf * r
        dxh = dyf * g_ref[...]
        dx = r * (dxh - xh * jnp.mean(dxh * xh, axis=-1, keepdims=True))
        if has_res:
            dx = dx + dres_ref[...]
        dx_ref[...] = dx.astype(dx_ref.dtype)

        @pl.when(i == 0)
        def _():
            dg_ref[...] = jnp.zeros_like(dg_ref)

        dg_ref[...] += jnp.sum(dyf * xh, axis=0, keepdims=True)

    row = pl.BlockSpec((ts, d), lambda i: (i, 0))
    vec = pl.BlockSpec((1, d), lambda i: (0, 0))
    ins = [x, g, dy] + ([dres] if has_res else [])
    return pl.pallas_call(
        body, grid=(s // ts,), name=name,
        in_specs=[row, vec, row] + ([row] if has_res else []),
        out_specs=(row, vec),
        out_shape=(jax.ShapeDtypeStruct((s, d), out_dtype), jax.ShapeDtypeStruct((1, d), F32)),
        compiler_params=_params("arbitrary"),
    )(*ins)


def _loss_head(h, target, name):
    s, d = h.shape
    ts = _rms_rows(s)
    n_i = s // ts

    def body(h_ref, t_ref, dh_ref, loss_ref, acc_ref):
        i = pl.program_id(0)
        err = h_ref[...] - t_ref[...]
        dh_ref[...] = err * (1.0 / d)

        @pl.when(i == 0)
        def _():
            acc_ref[...] = jnp.zeros_like(acc_ref)

        acc_ref[...] += jnp.sum(err * err, axis=0, keepdims=True)

        @pl.when(i == n_i - 1)
        def _():
            loss_ref[...] = jnp.sum(acc_ref[...], axis=1, keepdims=True) * (0.5 / d)

    row = pl.BlockSpec((ts, d), lambda i: (i, 0))
    return pl.pallas_call(
        body, grid=(n_i,), name=name,
        in_specs=[row, row],
        out_specs=(row, pl.BlockSpec((1, 1), lambda i: (0, 0))),
        out_shape=(jax.ShapeDtypeStruct((s, d), F32), jax.ShapeDtypeStruct((1, 1), F32)),
        scratch_shapes=[pltpu.VMEM((1, d), F32)],
        compiler_params=_params("arbitrary"),
    )(h, target)


def _halo_specs(ns, s, ts, tc):
    hb = ts // HALO
    last = s // HALO - 1
    cur = pl.BlockSpec((ns, ts, tc), lambda j, i: (0, i, j))
    prev = pl.BlockSpec((ns, HALO, tc), lambda j, i: (0, jnp.maximum(i * hb - 1, 0), j))
    nxt = pl.BlockSpec((ns, HALO, tc), lambda j, i: (0, jnp.minimum((i + 1) * hb, last), j))
    return cur, prev, nxt


def _ext(prev, cur, nxt, first, last):
    p = jnp.where(first, 0.0, prev.astype(F32))
    n = jnp.where(last, 0.0, nxt.astype(F32))
    return jnp.concatenate([p, cur.astype(F32), n], axis=0)


def _dn(xe, k):
    return xe if k == 0 else pltpu.roll(xe, k, 0)


def _up(xe, k):
    return xe if k == 0 else pltpu.roll(xe, xe.shape[0] - k, 0)


def _conv_e(xe, w):
    return w[0:1, :] * _dn(xe, 2) + w[1:2, :] * _dn(xe, 1) + w[2:3, :] * xe


def _conv_t_e(de, w):
    return w[2:3, :] * de + w[1:2, :] * _up(de, 1) + w[0:1, :] * _up(de, 2)


def _conv_dw(de, xe, ts):
    return [jnp.sum((de * _dn(xe, 2 - k))[HALO:HALO + ts], axis=0, keepdims=True) for k in range(3)]


def _conv_tiles(s, c):
    return _tile(s, (256, 128)), _tile(c, (512, 256, 128))


def _sc_gate_fwd(p, w, name):
    _, s, c = p.shape
    ts, tc = _conv_tiles(s, c)
    n_i = s // ts

    def body(p_ref, pp_ref, w_ref, y_ref):
        i = pl.program_id(1)
        ze = _ext(pp_ref[1], p_ref[1], pp_ref[1], i == 0, True) * _ext(pp_ref[2], p_ref[2], pp_ref[2], i == 0, True)
        cz = _conv_e(ze, w_ref[...])[HALO:HALO + ts]
        y_ref[...] = (p_ref[0].astype(F32) * cz).astype(y_ref.dtype)

    cur, prev, _ = _halo_specs(3, s, ts, tc)
    return pl.pallas_call(
        body, grid=(c // tc, n_i), name=name,
        in_specs=[cur, prev, pl.BlockSpec((3, tc), lambda j, i: (0, j))],
        out_specs=pl.BlockSpec((ts, tc), lambda j, i: (i, j)),
        out_shape=jax.ShapeDtypeStruct((s, c), BF16),
        compiler_params=_params("parallel", "arbitrary"),
    )(p, p, w)


def _sc_gate_bwd(p, dy, w, name):
    _, s, c = p.shape
    ts, tc = _conv_tiles(s, c)
    n_i = s // ts

    def body(p_ref, pp_ref, pn_ref, dy_ref, dyn_ref, w_ref, dp_ref, dw_ref):
        i = pl.program_id(1)
        first, last = i == 0, i == n_i - 1
        wv = w_ref[...]
        gbe = _ext(pp_ref[0], p_ref[0], pn_ref[0], first, last)
        gce = _ext(pp_ref[1], p_ref[1], pn_ref[1], first, last)
        hhe = _ext(pp_ref[2], p_ref[2], pn_ref[2], first, last)
        dye = _ext(dyn_ref[...], dy_ref[...], dyn_ref[...], True, last)
        ze = gce * hhe
        cze = _conv_e(ze, wv)
        dcze = dye * gbe
        dze = _conv_t_e(dcze, wv)
        rows = slice(HALO, HALO + ts)
        dp_ref[0] = (dye * cze)[rows].astype(dp_ref.dtype)
        dp_ref[1] = (dze * hhe)[rows].astype(dp_ref.dtype)
        dp_ref[2] = (dze * gce)[rows].astype(dp_ref.dtype)

        @pl.when(first)
        def _():
            dw_ref[...] = jnp.zeros_like(dw_ref)

        dws = _conv_dw(dcze, ze, ts)
        for k in range(3):
            dw_ref[k:k + 1, :] += dws[k]

    cur, prev, nxt = _halo_specs(3, s, ts, tc)
    hb = ts // HALO
    last_h = s // HALO - 1
    return pl.pallas_call(
        body, grid=(c // tc, n_i), name=name,
        in_specs=[cur, prev, nxt,
                  pl.BlockSpec((ts, tc), lambda j, i: (i, j)),
                  pl.BlockSpec((HALO, tc), lambda j, i: (jnp.minimum((i + 1) * hb, last_h), j)),
                  pl.BlockSpec((3, tc), lambda j, i: (0, j))],
        out_specs=(cur, pl.BlockSpec((3, tc), lambda j, i: (0, j))),
        out_shape=(jax.ShapeDtypeStruct((3, s, c), BF16), jax.ShapeDtypeStruct((3, c), F32)),
        compiler_params=_params("parallel", "arbitrary"),
    )(p, p, p, dy, dy, w)


def _ffn_act_fwd(u, w, b, name):
    _, s, f = u.shape
    ts, tc = _conv_tiles(s, f)
    n_i = s // ts

    def body(u_ref, up_ref, w_ref, b_ref, a_ref):
        i = pl.program_id(1)
        rows = slice(HALO, HALO + ts)
        cg = _conv_e(_ext(up_ref[0], u_ref[0], up_ref[0], i == 0, True), w_ref[0])[rows] + b_ref[0]
        cv = _conv_e(_ext(up_ref[1], u_ref[1], up_ref[1], i == 0, True), w_ref[1])[rows] + b_ref[1]
        a_ref[...] = (cg * jax.nn.sigmoid(cg) * cv).astype(a_ref.dtype)

    cur, prev, _ = _halo_specs(2, s, ts, tc)
    return pl.pallas_call(
        body, grid=(f // tc, n_i), name=name,
        in_specs=[cur, prev, pl.BlockSpec((2, 3, tc), lambda j, i: (0, 0, j)),
                  pl.BlockSpec((2, 1, tc), lambda j, i: (0, 0, j))],
        out_specs=pl.BlockSpec((ts, tc), lambda j, i: (i, j)),
        out_shape=jax.ShapeDtypeStruct((s, f), BF16),
        compiler_params=_params("parallel", "arbitrary"),
    )(u, u, w, b)


def _ffn_act_bwd(u, da, w, b, name):
    _, s, f = u.shape
    ts, tc = _conv_tiles(s, f)
    n_i = s // ts

    def body(u_ref, up_ref, un_ref, da_ref, dan_ref, w_ref, b_ref, du_ref, dw_ref, db_ref):
        i = pl.program_id(1)
        first, last = i == 0, i == n_i - 1
        rows = slice(HALO, HALO + ts)
        uge = _ext(up_ref[0], u_ref[0], un_ref[0], first, last)
        uve = _ext(up_ref[1], u_ref[1], un_ref[1], first, last)
        dae = _ext(dan_ref[...], da_ref[...], dan_ref[...], True, last)
        cg = _conv_e(uge, w_ref[0]) + b_ref[0]
        cv = _conv_e(uve, w_ref[1]) + b_ref[1]
        sg = jax.nn.sigmoid(cg)
        dcg = dae * cv * (sg * (1.0 + cg * (1.0 - sg)))
        dcv = dae * (cg * sg)
        du_ref[0] = _conv_t_e(dcg, w_ref[0])[rows].astype(du_ref.dtype)
        du_ref[1] = _conv_t_e(dcv, w_ref[1])[rows].astype(du_ref.dtype)

        @pl.when(first)
        def _():
            dw_ref[...] = jnp.zeros_like(dw_ref)
            db_ref[...] = jnp.zeros_like(db_ref)

        for h, (de, xe) in enumerate(((dcg, uge), (dcv, uve))):
            dws = _conv_dw(de, xe, ts)
            for k in range(3):
                dw_ref[h, k:k + 1, :] += dws[k]
            db_ref[h] += jnp.sum(de[rows], axis=0, keepdims=True)

    cur, prev, nxt = _halo_specs(2, s, ts, tc)
    hb = ts // HALO
    last_h = s // HALO - 1
    wspec = pl.BlockSpec((2, 3, tc), lambda j, i: (0, 0, j))
    bspec = pl.BlockSpec((2, 1, tc), lambda j, i: (0, 0, j))
    return pl.pallas_call(
        body, grid=(f // tc, n_i), name=name,
        in_specs=[cur, prev, nxt,
                  pl.BlockSpec((ts, tc), lambda j, i: (i, j)),
                  pl.BlockSpec((HALO, tc), lambda j, i: (jnp.minimum((i + 1) * hb, last_h), j)),
                  wspec, bspec],
        out_specs=(cur, wspec, bspec),
        out_shape=(jax.ShapeDtypeStruct((2, s, f), BF16), jax.ShapeDtypeStruct((2, 3, f), F32),
                   jax.ShapeDtypeStruct((2, 1, f), F32)),
        compiler_params=_params("parallel", "arbitrary"),
    )(u, u, u, da, da, w, b)


_INV_SQRT2 = 0.7071067811865476
_INV_SQRT_2PI = 0.3989422804014327


def _gelu(x):
    return 0.5 * x * (1.0 + lax.erf(x * _INV_SQRT2))


def _gelu_grad(x):
    return 0.5 * (1.0 + lax.erf(x * _INV_SQRT2)) + x * (_INV_SQRT_2PI * jnp.exp(-0.5 * x * x))


def _tril_bf16(w):
    t = lax.broadcasted_iota(jnp.int32, w.shape, 0)
    s = lax.broadcasted_iota(jnp.int32, w.shape, 1)
    return jnp.where(s <= t, w, 0.0).astype(BF16)


def _sgu_fwd(p, ln_g, ln_b, ws, bs, name):
    _, s, c = p.shape
    g_n = ws.shape[0]
    cg = c // g_n

    def body(p_ref, lg_ref, lb_ref, ws_ref, bs_ref, y_ref):
        u = _gelu(p_ref[0].astype(F32))
        v = _gelu(p_ref[1].astype(F32))
        mu = jnp.mean(v, axis=-1, keepdims=True)
        vc = v - mu
        rstd = lax.rsqrt(jnp.mean(vc * vc, axis=-1, keepdims=True) + EPS)
        vn = ((vc * rstd) * lg_ref[...] + lb_ref[...]).astype(BF16)
        for g in range(g_n):
            cols = slice(g * cg, (g + 1) * cg)
            mixed = jnp.dot(_tril_bf16(ws_ref[g]), vn[:, cols], preferred_element_type=F32) + bs_ref[g]
            y_ref[:, cols] = (u[:, cols] * mixed).astype(y_ref.dtype)

    vec = pl.BlockSpec((1, c), lambda i: (0, 0))
    return pl.pallas_call(
        body, grid=(s // CHUNK,), name=name,
        in_specs=[pl.BlockSpec((2, CHUNK, c), lambda i: (0, i, 0)), vec, vec,
                  pl.BlockSpec((g_n, CHUNK, CHUNK), lambda i: (0, 0, 0)),
                  pl.BlockSpec((g_n, CHUNK, 1), lambda i: (0, 0, 0))],
        out_specs=pl.BlockSpec((CHUNK, c), lambda i: (i, 0)),
        out_shape=jax.ShapeDtypeStruct((s, c), BF16),
        compiler_params=_params("parallel"),
    )(p, ln_g, ln_b, ws, bs)


def _sgu_bwd(p, dy, ln_g, ln_b, ws, ws_t, bs, name):
    _, s, c = p.shape
    g_n = ws.shape[0]
    cg = c // g_n

    def body(p_ref, dy_ref, lg_ref, lb_ref, ws_ref, wst_ref, bs_ref, dp_ref, dlg_ref, dlb_ref, dws_ref, dbs_ref,
             dvn_ref):
        i = pl.program_id(0)
        pu = p_ref[0].astype(F32)
        pv = p_ref[1].astype(F32)
        dyf = dy_ref[...].astype(F32)
        u = _gelu(pu)
        v = _gelu(pv)
        mu = jnp.mean(v, axis=-1, keepdims=True)
        vc = v - mu
        rstd = lax.rsqrt(jnp.mean(vc * vc, axis=-1, keepdims=True) + EPS)
        vhat = vc * rstd
        vn = (vhat * lg_ref[...] + lb_ref[...]).astype(BF16)

        @pl.when(i == 0)
        def _():
            dlg_ref[...] = jnp.zeros_like(dlg_ref)
            dlb_ref[...] = jnp.zeros_like(dlb_ref)
            dws_ref[...] = jnp.zeros_like(dws_ref)
            dbs_ref[...] = jnp.zeros_like(dbs_ref)

        t_i = lax.broadcasted_iota(jnp.int32, (CHUNK, CHUNK), 0)
        s_i = lax.broadcasted_iota(jnp.int32, (CHUNK, CHUNK), 1)
        for g in range(g_n):
            cols = slice(g * cg, (g + 1) * cg)
            mixed = jnp.dot(_tril_bf16(ws_ref[g]), vn[:, cols], preferred_element_type=F32) + bs_ref[g]
            dp_ref[0, :, cols] = (dyf[:, cols] * mixed * _gelu_grad(pu[:, cols])).astype(dp_ref.dtype)
            dmixed = dyf[:, cols] * u[:, cols]
            dmb = dmixed.astype(BF16)
            wt = jnp.where(t_i <= s_i, wst_ref[g], 0.0).astype(BF16)
            dvn_ref[:, cols] = jnp.dot(wt, dmb, preferred_element_type=F32)
            dwg = lax.dot_general(dmb, vn[:, cols], (((1,), (1,)), ((), ())), preferred_element_type=F32)
            dws_ref[g] += jnp.where(s_i <= t_i, dwg, 0.0)
            dbs_ref[g] += jnp.sum(dmixed, axis=1, keepdims=True)
        dvn = dvn_ref[...]
        dlg_ref[...] += jnp.sum(dvn * vhat, axis=0, keepdims=True)
        dlb_ref[...] += jnp.sum(dvn, axis=0, keepdims=True)
        dvh = dvn * lg_ref[...]
        dv = rstd * (dvh - jnp.mean(dvh, axis=-1, keepdims=True) - vhat * jnp.mean(dvh * vhat, axis=-1, keepdims=True))
        dp_ref[1] = (dv * _gelu_grad(pv)).astype(dp_ref.dtype)

    vec = pl.BlockSpec((1, c), lambda i: (0, 0))
    wspec = pl.BlockSpec((g_n, CHUNK, CHUNK), lambda i: (0, 0, 0))
    bspec = pl.BlockSpec((g_n, CHUNK, 1), lambda i: (0, 0, 0))
    pspec = pl.BlockSpec((2, CHUNK, c), lambda i: (0, i, 0))
    return pl.pallas_call(
        body, grid=(s // CHUNK,), name=name,
        in_specs=[pspec, pl.BlockSpec((CHUNK, c), lambda i: (i, 0)), vec, vec, wspec, wspec, bspec],
        out_specs=(pspec, vec, vec, wspec, bspec),
        out_shape=(jax.ShapeDtypeStruct((2, s, c), BF16), jax.ShapeDtypeStruct((1, c), F32),
                   jax.ShapeDtypeStruct((1, c), F32), jax.ShapeDtypeStruct((g_n, CHUNK, CHUNK), F32),
                   jax.ShapeDtypeStruct((g_n, CHUNK, 1), F32)),
        scratch_shapes=[pltpu.VMEM((CHUNK, c), F32)],
        compiler_params=_params("arbitrary"),
    )(p, dy, ln_g, ln_b, ws, ws_t, bs)


SB_Q = 512


def _sb_cumsum(x, tri, edge, reverse):
    n = x.shape[0]
    hi = x.astype(BF16)
    lo = (x - hi.astype(F32)).astype(BF16)
    chunks = x.shape[1] // CHUNK
    outs = [None] * chunks
    for c in (reversed(range(chunks)) if reverse else range(chunks)):
        cols = slice(c * CHUNK, (c + 1) * CHUNK)
        both = jnp.dot(jnp.concatenate([hi[:, cols], lo[:, cols]], axis=0), tri, preferred_element_type=F32)
        outs[c] = both[:n] + both[n:] + edge
        edge = edge + jnp.sum(x[:, cols], axis=1, keepdims=True)
    return jnp.concatenate(outs, axis=1), edge


def _sb_block(q, k, offset):
    scale = HEAD_DIM ** -0.5
    z = lax.dot_general(q, k, (((1,), (1,)), ((), ())), preferred_element_type=F32) * scale
    e = jnp.exp(-jnp.abs(z))
    lp = jnp.log(1.0 + e)
    lb = jnp.minimum(z, 0.0) - lp
    l1 = jnp.minimum(-z, 0.0) - lp
    if offset is not None:
        t_i = lax.broadcasted_iota(jnp.int32, z.shape, 0)
        s_i = lax.broadcasted_iota(jnp.int32, z.shape, 1)
        mask = (t_i - s_i) > offset
        l1 = jnp.where(mask, l1, 0.0)
    else:
        mask = None
    return z, e, lb, l1, mask


def _tri(cond):
    j = lax.broadcasted_iota(jnp.int32, (CHUNK, CHUNK), 0)
    s = lax.broadcasted_iota(jnp.int32, (CHUNK, CHUNK), 1)
    return jnp.where(cond(j, s), 1.0, 0.0).astype(BF16)


def _sb_fwd(qkv, name):
    _, s, c = qkv.shape
    heads = c // HEAD_DIM
    bq = _tile(s, (SB_Q, 256, CHUNK))
    nq = s // bq

    def body(q_ref, k_ref, v_ref, o_ref):
        i = pl.program_id(1)
        q = q_ref[...]
        after = _tri(lambda j, s_: j > s_)

        def tile(g, carry, offset):
            o_acc, run = carry
            rows = pl.ds(pl.multiple_of(g * bq, bq), bq)
            _, _, lb, l1, mask = _sb_block(q, k_ref[rows, :], offset)
            acc, run = _sb_cumsum(l1, after, run, True)
            a = jnp.exp(lb + acc)
            if mask is not None:
                a = jnp.where(mask, a, 0.0)
            o_acc = o_acc + jnp.dot(a.astype(BF16), v_ref[rows, :], preferred_element_type=F32)
            return o_acc, run

        carry = tile(i, (jnp.zeros((bq, HEAD_DIM), F32), jnp.zeros((bq, 1), F32)), 0)
        o_acc, _ = lax.fori_loop(0, i, lambda n, cr: tile(i - 1 - n, cr, None), carry)
        o_ref[...] = o_acc.astype(o_ref.dtype)

    return pl.pallas_call(
        body, grid=(heads, nq), name=name,
        in_specs=[pl.BlockSpec((None, bq, HEAD_DIM), lambda h, i: (0, i, h)),
                  pl.BlockSpec((None, s, HEAD_DIM), lambda h, i: (1, 0, h)),
                  pl.BlockSpec((None, s, HEAD_DIM), lambda h, i: (2, 0, h))],
        out_specs=pl.BlockSpec((bq, HEAD_DIM), lambda h, i: (i, h)),
        out_shape=jax.ShapeDtypeStruct((s, c), BF16),
        compiler_params=_params("parallel", "arbitrary"),
    )(qkv, qkv, qkv)


def _sb_bwd(qkv, do, name):
    _, s, c = qkv.shape
    heads = c // HEAD_DIM
    bq = _tile(s, (SB_Q, 256, CHUNK))
    nq = s // bq
    scale = HEAD_DIM ** -0.5
    tn_dims = (((0,), (0,)), ((), ()))

    def body(q_ref, k_ref, v_ref, do_ref, dq_ref, dkv_ref, run_ref, dk_acc, dv_acc):
        i = pl.program_id(1)
        q = q_ref[...]
        dob = do_ref[...]
        after = _tri(lambda j, s_: j > s_)
        before = _tri(lambda j, s_: j < s_)

        @pl.when(i == 0)
        def _():
            dk_acc[...] = jnp.zeros_like(dk_acc)
            dv_acc[...] = jnp.zeros_like(dv_acc)

        def pre(g, run, offset):
            rows = pl.ds(pl.multiple_of(g * bq, bq), bq)
            run_ref[g] = run
            _, _, _, l1, _ = _sb_block(q, k_ref[rows, :], offset)
            return run + jnp.sum(l1, axis=1, keepdims=True)

        run0 = pre(i, jnp.zeros((bq, 1), F32), 0)
        lax.fori_loop(0, i, lambda n, r: pre(i - 1 - n, r, None), run0)

        def main(g, carry, offset):
            dq_acc, csum = carry
            rows = pl.ds(pl.multiple_of(g * bq, bq), bq)
            kblk = k_ref[rows, :]
            vblk = v_ref[rows, :]
            z, e, lb, l1, mask = _sb_block(q, kblk, offset)
            acc, _ = _sb_cumsum(l1, after, run_ref[g], True)
            a = jnp.exp(lb + acc)
            if mask is not None:
                a = jnp.where(mask, a, 0.0)
            d_a = lax.dot_general(dob, vblk, (((1,), (1,)), ((), ())), preferred_element_type=F32)
            ee = a * d_a
            cc, csum = _sb_cumsum(ee, before, csum, False)
            beta = jnp.where(z >= 0.0, 1.0, e) / (1.0 + e)
            dz = (ee - (ee + cc) * beta) * scale
            if mask is not None:
                dz = jnp.where(mask, dz, 0.0)
            dzb = dz.astype(BF16)
            dq_acc = dq_acc + jnp.dot(dzb, kblk, preferred_element_type=F32)
            dk_acc[rows, :] += lax.dot_general(dzb, q, tn_dims, preferred_element_type=F32)
            dv_acc[rows, :] += lax.dot_general(a.astype(BF16), dob, tn_dims, preferred_element_type=F32)
            return dq_acc, csum

        carry = (jnp.zeros((bq, HEAD_DIM), F32), jnp.zeros((bq, 1), F32))
        carry = lax.fori_loop(0, i, lambda g, cr: main(g, cr, None), carry)
        dq_acc, _ = main(i, carry, 0)
        dq_ref[...] = dq_acc.astype(dq_ref.dtype)

        @pl.when(i == nq - 1)
        def _():
            dkv_ref[0] = dk_acc[...].astype(dkv_ref.dtype)
            dkv_ref[1] = dv_acc[...].astype(dkv_ref.dtype)

    blk = pl.BlockSpec((bq, HEAD_DIM), lambda h, i: (i, h))
    return pl.pallas_call(
        body, grid=(heads, nq), name=name,
        in_specs=[pl.BlockSpec((None, bq, HEAD_DIM), lambda h, i: (0, i, h)),
                  pl.BlockSpec((None, s, HEAD_DIM), lambda h, i: (1, 0, h)),
                  pl.BlockSpec((None, s, HEAD_DIM), lambda h, i: (2, 0, h)),
                  blk],
        out_specs=(blk, pl.BlockSpec((2, s, HEAD_DIM), lambda h, i: (0, 0, h))),
        out_shape=(jax.ShapeDtypeStruct((s, c), BF16), jax.ShapeDtypeStruct((2, s, c), BF16)),
        scratch_shapes=[pltpu.VMEM((nq, bq, 1), F32), pltpu.VMEM((s, HEAD_DIM), F32),
                        pltpu.VMEM((s, HEAD_DIM), F32)],
        compiler_params=_params("parallel", "arbitrary"),
    )(qkv, qkv, qkv, do)


def _shard_of(ref, dev, axis, r, c):
    if axis == 0:
        return ref.at[pl.ds(dev * r, r), :]
    return ref.at[:, pl.ds(dev * c, c)]


def _all_gather(x, axis, name):
    r, c = x.shape
    full = (N_DEV * r, c) if axis == 0 else (r, N_DEV * c)

    def body(x_ref, out_ref, send_sems, recv_sems, local_sem):
        mx, my, mc = lax.axis_index("x"), lax.axis_index("y"), lax.axis_index("c")
        me, sibling = (mx, my, mc), (mx, my, 1 - mc)
        chips = [(1 - mx, my), (mx, 1 - my), (1 - mx, 1 - my)]

        def rows(px, py, pc):
            return _shard_of(out_ref, 4 * px + 2 * py + pc, axis, r, c)

        def copy(k, block, to, src=None):
            return pltpu.make_async_remote_copy(
                src_ref=rows(*block) if src is None else src, dst_ref=rows(*block),
                send_sem=send_sems.at[k], recv_sem=recv_sems.at[k], device_id=to, device_id_type=MESH)

        mine = pltpu.make_async_copy(x_ref, rows(*me), local_sem)
        mine.start()
        first = [copy(0, me, sibling, src=x_ref)]
        first += [copy(1 + j, me, (*chip, mc), src=x_ref) for j, chip in enumerate(chips)]
        for cp in first:
            cp.start()
        passed = [copy(4 + j, (*chip, mc), sibling) for j, chip in enumerate(chips)]
        for j, chip in enumerate(chips):
            copy(1 + j, (*chip, mc), me).wait_recv()
            passed[j].start()
        copy(0, sibling, me).wait_recv()
        for j, chip in enumerate(chips):
            copy(4 + j, (*chip, 1 - mc), me).wait_recv()
        for cp in first + passed:
            cp.wait_send()
        mine.wait()

    return pl.pallas_call(
        body, name=name,
        out_shape=jax.ShapeDtypeStruct(full, x.dtype),
        in_specs=[ANY], out_specs=ANY,
        scratch_shapes=[pltpu.SemaphoreType.DMA((7,)), pltpu.SemaphoreType.DMA((7,)), pltpu.SemaphoreType.DMA(())],
    )(x)


def _rs_core_exchange(g, axis, name):
    r, c = (g.shape[0] // N_DEV, g.shape[1]) if axis == 0 else (g.shape[0], g.shape[1] // N_DEV)

    def body(g_ref, recv_ref, send_sems, recv_sems):
        mx, my, mc = lax.axis_index("x"), lax.axis_index("y"), lax.axis_index("c")
        copies = [pltpu.make_async_remote_copy(
            src_ref=_shard_of(g_ref, 2 * k + (1 - mc), axis, r, c), dst_ref=recv_ref.at[k],
            send_sem=send_sems.at[k], recv_sem=recv_sems.at[k], device_id=(mx, my, 1 - mc), device_id_type=MESH)
            for k in range(4)]
        for cp in copies:
            cp.start()
        for cp in copies:
            cp.wait()

    return pl.pallas_call(
        body, name=name,
        out_shape=jax.ShapeDtypeStruct((4, r, c), g.dtype),
        in_specs=[ANY], out_specs=ANY,
        scratch_shapes=[pltpu.SemaphoreType.DMA((4,)), pltpu.SemaphoreType.DMA((4,))],
    )(g)


def _ew_tiles(r, c):
    tc = c if c <= 2048 else _tile(c, (2048, 1024, 512, 256, 128))
    return _tile(r, [p for p in (512, 256, 128, 64, 32, 16, 8) if p * tc <= 256 * 1024]), tc


def _rs_core_add(g, recv, core, axis, name):
    _, r, c = recv.shape
    tr, tc = _ew_tiles(r, c)
    nrb, ncb = r // tr, c // tc

    def body(core_ref, g_ref, x_ref, o_ref):
        del core_ref
        o_ref[...] = (g_ref[...].astype(F32) + x_ref[...].astype(F32)).astype(o_ref.dtype)

    if axis == 0:
        g_spec = pl.BlockSpec((tr, tc), lambda k, i, j, core: ((2 * k + core[0]) * nrb + i, j))
    else:
        g_spec = pl.BlockSpec((tr, tc), lambda k, i, j, core: (i, (2 * k + core[0]) * ncb + j))
    blk = pl.BlockSpec((None, tr, tc), lambda k, i, j, core: (k, i, j))
    return pl.pallas_call(
        body, name=name,
        grid_spec=pltpu.PrefetchScalarGridSpec(
            num_scalar_prefetch=1, grid=(4, nrb, ncb), in_specs=[g_spec, blk], out_specs=blk),
        out_shape=jax.ShapeDtypeStruct(recv.shape, BF16),
        compiler_params=_params("parallel", "parallel", "parallel"),
    )(core, g, recv)


def _rs_chip_exchange(p, name):
    _, r, c = p.shape

    def body(p_ref, recv_ref, send_sems, recv_sems):
        mx, my, mc = lax.axis_index("x"), lax.axis_index("y"), lax.axis_index("c")
        chips = [(1 - mx, my), (mx, 1 - my), (1 - mx, 1 - my)]
        copies = [pltpu.make_async_remote_copy(
            src_ref=p_ref.at[2 * px + py], dst_ref=recv_ref.at[j],
            send_sem=send_sems.at[j], recv_sem=recv_sems.at[j], device_id=(px, py, mc), device_id_type=MESH)
            for j, (px, py) in enumerate(chips)]
        for cp in copies:
            cp.start()
        for cp in copies:
            cp.wait()

    return pl.pallas_call(
        body, name=name,
        out_shape=jax.ShapeDtypeStruct((3, r, c), p.dtype),
        in_specs=[ANY], out_specs=ANY,
        scratch_shapes=[pltpu.SemaphoreType.DMA((3,)), pltpu.SemaphoreType.DMA((3,))],
    )(p)


def _adamw_math(w, g, m, v):
    m = ADAM_B1 * m + (1.0 - ADAM_B1) * g
    v = ADAM_B2 * v + (1.0 - ADAM_B2) * (g * g)
    m_hat = m / (1.0 - ADAM_B1 ** ADAM_STEP)
    v_hat = v / (1.0 - ADAM_B2 ** ADAM_STEP)
    delta = -ADAM_LR * (m_hat / (jnp.sqrt(v_hat) + ADAM_EPS) + ADAM_WD * w)
    return delta, m, v


def _adamw_shard(w, m, v, p, recv, chip, name):
    r, c = w.shape
    tr, tc = _ew_tiles(r, c)

    def body(chip_ref, w_ref, m_ref, v_ref, p_ref, x_ref, g_out, d_out, m_out, v_out):
        del chip_ref
        g = p_ref[...].astype(F32)
        for j in range(3):
            g = g + x_ref[j].astype(F32)
        d, mn, vn = _adamw_math(w_ref[...], g, m_ref[...], v_ref[...])
        g_out[...] = g
        d_out[...] = d
        m_out[...] = mn
        v_out[...] = vn

    blk = pl.BlockSpec((tr, tc), lambda i, j, chip: (i, j))
    out = jax.ShapeDtypeStruct((r, c), F32)
    return pl.pallas_call(
        body, name=name,
        grid_spec=pltpu.PrefetchScalarGridSpec(
            num_scalar_prefetch=1, grid=(r // tr, c // tc),
            in_specs=[blk, blk, blk, pl.BlockSpec((None, tr, tc), lambda i, j, chip: (chip[0], i, j)),
                      pl.BlockSpec((3, tr, tc), lambda i, j, chip: (0, i, j))],
            out_specs=(blk, blk, blk, blk)),
        out_shape=(out, out, out, out),
        compiler_params=_params("parallel", "parallel"),
    )(chip, w, m, v, p, recv)


def _adamw_packed(w, m, v, g, name):
    r, c = w.shape
    tr = _tile(r, (512, 256, 128, 64, 32, 16, 8))

    def body(w_ref, m_ref, v_ref, g_ref, d_out, m_out, v_out):
        d, mn, vn = _adamw_math(w_ref[...], g_ref[...], m_ref[...], v_ref[...])
        d_out[...] = d
        m_out[...] = mn
        v_out[...] = vn

    blk = pl.BlockSpec((tr, c), lambda i: (i, 0))
    out = jax.ShapeDtypeStruct((r, c), F32)
    return pl.pallas_call(
        body, grid=(r // tr,), name=name, in_specs=[blk, blk, blk, blk], out_specs=(blk, blk, blk),
        out_shape=(out, out, out), compiler_params=_params("parallel"),
    )(w, m, v, g)


def _sum_devices(x, name):
    r = x.shape[0] // N_DEV
    tr = _tile(r, (512, 256, 128, 64, 32, 16, 8))

    def body(x_ref, o_ref):
        acc = x_ref[0]
        for d in range(1, N_DEV):
            acc = acc + x_ref[d]
        o_ref[...] = acc

    return pl.pallas_call(
        body, grid=(r // tr,), name=name,
        in_specs=[pl.BlockSpec((N_DEV, tr, LANES), lambda i: (0, i, 0))],
        out_specs=pl.BlockSpec((tr, LANES), lambda i: (i, 0)),
        out_shape=jax.ShapeDtypeStruct((r, LANES), F32),
        compiler_params=_params("parallel"),
    )(x.reshape(N_DEV, r, LANES))


def _pack(arrays):
    flat = []
    for a in arrays:
        f = a.reshape(-1).astype(F32)
        flat.append(jnp.pad(f, (0, (-f.shape[0]) % LANES)))
    f = jnp.concatenate(flat)
    f = jnp.pad(f, (0, (-f.shape[0]) % (8 * LANES)))
    return f.reshape(-1, LANES)


def _unpack(packed, shapes):
    flat = packed.reshape(-1)
    out, off = [], 0
    for shp in shapes:
        n = 1
        for d in shp:
            n *= d
        out.append(flat[off:off + n].reshape(shp))
        off += n + (-n) % LANES
    return out


def kernel(x, norm_mix_pre, norm_mix_post, norm_ffn_pre, norm_ffn_post, sc_w_in, sc_conv_w, sc_w_out, sg_w_in, sg_ln_g, sg_ln_b, sg_w_s, sg_b_s, sg_w_out, sb_w_qkv, sb_w_out, ffn_w_up, ffn_conv_w, ffn_conv_b, ffn_w_down, loss_target, m_norm_mix_pre, m_norm_mix_post, m_norm_ffn_pre, m_norm_ffn_post, m_sc_w_in, m_sc_conv_w, m_sc_w_out, m_sg_w_in, m_sg_ln_g, m_sg_ln_b, m_sg_w_s, m_sg_b_s, m_sg_w_out, m_sb_w_qkv, m_sb_w_out, m_ffn_w_up, m_ffn_conv_w, m_ffn_conv_b, m_ffn_w_down, v_norm_mix_pre, v_norm_mix_post, v_norm_ffn_pre, v_norm_ffn_post, v_sc_w_in, v_sc_conv_w, v_sc_w_out, v_sg_w_in, v_sg_ln_g, v_sg_ln_b, v_sg_w_s, v_sg_b_s, v_sg_w_out, v_sb_w_qkv, v_sb_w_out, v_ffn_w_up, v_ffn_conv_w, v_ffn_conv_b, v_ffn_w_down):
    weights = dict(norm_mix_pre=norm_mix_pre, norm_mix_post=norm_mix_post, norm_ffn_pre=norm_ffn_pre,
                   norm_ffn_post=norm_ffn_post, sc_w_in=sc_w_in, sc_conv_w=sc_conv_w, sc_w_out=sc_w_out,
                   sg_w_in=sg_w_in, sg_ln_g=sg_ln_g, sg_ln_b=sg_ln_b, sg_w_s=sg_w_s, sg_b_s=sg_b_s,
                   sg_w_out=sg_w_out, sb_w_qkv=sb_w_qkv, sb_w_out=sb_w_out, ffn_w_up=ffn_w_up,
                   ffn_conv_w=ffn_conv_w, ffn_conv_b=ffn_conv_b, ffn_w_down=ffn_w_down)
    mom1 = dict(norm_mix_pre=m_norm_mix_pre, norm_mix_post=m_norm_mix_post, norm_ffn_pre=m_norm_ffn_pre,
                norm_ffn_post=m_norm_ffn_post, sc_w_in=m_sc_w_in, sc_conv_w=m_sc_conv_w, sc_w_out=m_sc_w_out,
                sg_w_in=m_sg_w_in, sg_ln_g=m_sg_ln_g, sg_ln_b=m_sg_ln_b, sg_w_s=m_sg_w_s, sg_b_s=m_sg_b_s,
                sg_w_out=m_sg_w_out, sb_w_qkv=m_sb_w_qkv, sb_w_out=m_sb_w_out, ffn_w_up=m_ffn_w_up,
                ffn_conv_w=m_ffn_conv_w, ffn_conv_b=m_ffn_conv_b, ffn_w_down=m_ffn_w_down)
    mom2 = dict(norm_mix_pre=v_norm_mix_pre, norm_mix_post=v_norm_mix_post, norm_ffn_pre=v_norm_ffn_pre,
                norm_ffn_post=v_norm_ffn_post, sc_w_in=v_sc_w_in, sc_conv_w=v_sc_conv_w, sc_w_out=v_sc_w_out,
                sg_w_in=v_sg_w_in, sg_ln_g=v_sg_ln_g, sg_ln_b=v_sg_ln_b, sg_w_s=v_sg_w_s, sg_b_s=v_sg_b_s,
                sg_w_out=v_sg_w_out, sb_w_qkv=v_sb_w_qkv, sb_w_out=v_sb_w_out, ffn_w_up=v_ffn_w_up,
                ffn_conv_w=v_ffn_conv_w, ffn_conv_b=v_ffn_conv_b, ffn_w_down=v_ffn_w_down)
    order = list(weights)

    mx, my, mc = lax.axis_index("x"), lax.axis_index("y"), lax.axis_index("c")
    dev = 4 * mx + 2 * my + mc
    core_arr = jnp.reshape(mc, (1,)).astype(jnp.int32)
    chip_arr = jnp.reshape(2 * mx + my, (1,)).astype(jnp.int32)

    h = x[0]
    target = loss_target[0]
    depth = norm_mix_pre.shape[0]
    d_model = h.shape[1]
    f_loc = ffn_w_down.shape[1]
    d_ff = f_loc * N_DEV

    def gather_rows8(a2d, name):
        rows = a2d.shape[0]
        pad = (-rows) % 8
        return _all_gather(jnp.pad(a2d, ((0, pad), (0, 0))), 1, name)[:rows]

    n_a = sc_w_in.shape[0]
    sc_conv_full = gather_rows8(sc_conv_w.reshape(n_a * 3, -1), "ag_sc_conv").reshape(n_a, 3, d_model)
    ffn_conv_full = gather_rows8(ffn_conv_w.reshape(depth * 3, -1), "ag_ffn_conv").reshape(depth, 3, 2 * d_ff)

    big = {}

    def gather_big(name, j, axis):
        big[(name, j)] = _all_gather(weights[name][j].astype(BF16), axis, f"ag_{name}")

    kinds = [i % 3 for i in range(depth)]
    for i in range(depth):
        j = i // 3
        if kinds[i] == 0:
            gather_big("sc_w_in", j, 1)
            gather_big("sc_w_out", j, 0)
        elif kinds[i] == 1:
            gather_big("sg_w_in", j, 1)
            gather_big("sg_w_out", j, 0)
        else:
            gather_big("sb_w_qkv", j, 1)
            gather_big("sb_w_out", j, 0)
        gather_big("ffn_w_up", i, 1)
        gather_big("ffn_w_down", i, 0)

    sg_ws_t = jnp.swapaxes(sg_w_s, -1, -2)
    sg_bs3 = sg_b_s[..., None]

    saved = []
    for i in range(depth):
        j = i // 3
        sv = {"h_in": h}
        hn = _rms_fwd(h, norm_mix_pre[i:i + 1], "rms_fwd")
        sv["hn"] = hn
        if kinds[i] == 0:
            p = _mm_nn(hn, big[("sc_w_in", j)], BF16, 3, "sc_in_proj")
            y = _sc_gate_fwd(p, sc_conv_full[j], "sc_gate_fwd")
            m = _mm_nn(y, big[("sc_w_out", j)], F32, 1, "mix_out_proj")
        elif kinds[i] == 1:
            p = _mm_nn(hn, big[("sg_w_in", j)], BF16, 2, "sg_in_proj")
            y = _sgu_fwd(p, sg_ln_g[j:j + 1], sg_ln_b[j:j + 1], sg_w_s[j], sg_bs3[j], "sgu_fwd")
            m = _mm_nn(y, big[("sg_w_out", j)], F32, 1, "mix_out_proj")
        else:
            p = _mm_nn(hn, big[("sb_w_qkv", j)], BF16, 3, "sb_qkv_proj")
            y = _sb_fwd(p, "sb_fwd")
            m = _mm_nn(y, big[("sb_w_out", j)], F32, 1, "mix_out_proj")
        sv.update(p=p, y=y, m=m)
        h = _resid_rms_fwd(h, m, norm_mix_post[i:i + 1], "resid_rms_fwd")
        sv["h_mid"] = h
        hn2 = _rms_fwd(h, norm_ffn_pre[i:i + 1], "rms_fwd")
        u = _mm_nn(hn2, big[("ffn_w_up", i)], BF16, 2, "ffn_up_proj")
        cw = ffn_conv_full[i].reshape(3, 2, d_ff).transpose(1, 0, 2)
        cb = ffn_conv_b[i].reshape(2, 1, d_ff)
        a = _ffn_act_fwd(u, cw, cb, "ffn_act_fwd")
        f = _mm_nn(a, big[("ffn_w_down", i)], F32, 1, "ffn_down_proj")
        sv.update(hn2=hn2, u=u, a=a, f=f, cw=cw, cb=cb)
        h = _resid_rms_fwd(h, f, norm_ffn_post[i:i + 1], "resid_rms_fwd")
        saved.append(sv)

    dh, loss_part = _loss_head(h, target, "loss_head")

    grads_big = {}
    small = {}

    def reduce_scatter(name, j, g_full, axis):
        recv1 = _rs_core_exchange(g_full, axis, f"rs1_{name}")
        part = _rs_core_add(g_full, recv1, core_arr, axis, f"rs_add_{name}")
        recv2 = _rs_chip_exchange(part, f"rs2_{name}")
        grads_big[(name, j)] = (part, recv2)

    for i in reversed(range(depth)):
        j = i // 3
        sv = saved[i]
        df, dg = _rms_bwd(sv["f"], norm_ffn_post[i:i + 1], dh, None, BF16, "rms_bwd_post")
        small.setdefault("norm_ffn_post", {})[i] = dg
        da = _mm_nt(df, big[("ffn_w_down", i)], BF16, "ffn_down_dx")
        reduce_scatter("ffn_w_down", i, _mm_tn(sv["a"], df, BF16, "ffn_down_dw"), 0)
        du, dcw, dcb = _ffn_act_bwd(sv["u"], da, sv["cw"], sv["cb"], "ffn_act_bwd")
        small.setdefault("ffn_conv_w", {})[i] = dcw.transpose(1, 0, 2).reshape(3, 2 * d_ff)
        small.setdefault("ffn_conv_b", {})[i] = dcb.reshape(2 * d_ff)
        dhn2 = _mm_nt(du, big[("ffn_w_up", i)], F32, "ffn_up_dx")
        reduce_scatter("ffn_w_up", i, _mm_tn(sv["hn2"], du, BF16, "ffn_up_dw"), 1)
        dh, dg = _rms_bwd(sv["h_mid"], norm_ffn_pre[i:i + 1], dhn2, dh, F32, "rms_bwd_pre")
        small.setdefault("norm_ffn_pre", {})[i] = dg
        dm, dg = _rms_bwd(sv["m"], norm_mix_post[i:i + 1], dh, None, BF16, "rms_bwd_post")
        small.setdefault("norm_mix_post", {})[i] = dg
        if kinds[i] == 0:
            wo, wi = "sc_w_out", "sc_w_in"
        elif kinds[i] == 1:
            wo, wi = "sg_w_out", "sg_w_in"
        else:
            wo, wi = "sb_w_out", "sb_w_qkv"
        dy = _mm_nt(dm, big[(wo, j)], BF16, "mix_out_dx")
        reduce_scatter(wo, j, _mm_tn(sv["y"], dm, BF16, "mix_out_dw"), 0)
        if kinds[i] == 0:
            dp, dcw = _sc_gate_bwd(sv["p"], dy, sc_conv_full[j], "sc_gate_bwd")
            small.setdefault("sc_conv_w", {})[j] = dcw
        elif kinds[i] == 1:
            dp, dlg, dlb, dws, dbs = _sgu_bwd(sv["p"], dy, sg_ln_g[j:j + 1], sg_ln_b[j:j + 1], sg_w_s[j], sg_ws_t[j],
                                              sg_bs3[j], "sgu_bwd")
            small.setdefault("sg_ln_g", {})[j] = dlg[0]
            small.setdefault("sg_ln_b", {})[j] = dlb[0]
            small.setdefault("sg_w_s", {})[j] = dws
            small.setdefault("sg_b_s", {})[j] = dbs[..., 0]
        else:
            dq, dkv = _sb_bwd(sv["p"], dy, "sb_bwd")
            dp = jnp.concatenate([dq[None], dkv], axis=0)
        dhn = _mm_nt(dp, big[(wi, j)], F32, "mix_in_dx")
        reduce_scatter(wi, j, _mm_tn(sv["hn"], dp, BF16, "mix_in_dw"), 1)
        dh, dg = _rms_bwd(sv["h_in"], norm_mix_pre[i:i + 1], dhn, dh, F32, "rms_bwd_pre")
        small.setdefault("norm_mix_pre", {})[i] = dg

    grad_x = dh[None]

    small_names = ["norm_mix_pre", "norm_mix_post", "norm_ffn_pre", "norm_ffn_post", "sg_ln_g", "sg_ln_b", "sg_w_s",
                   "sg_b_s", "ffn_conv_b", "sc_conv_w", "ffn_conv_w"]
    full_shapes = {n: weights[n].shape for n in small_names}
    full_shapes["sc_conv_w"] = (n_a, 3, d_model)
    full_shapes["ffn_conv_w"] = (depth, 3, 2 * d_ff)
    partial = [jnp.stack([small[n][k].reshape(full_shapes[n][1:]) for k in sorted(small[n])]) for n in small_names]
    packed = _pack(partial + [loss_part])
    gathered = _all_gather(packed, 0, "ag_small_grads")
    summed = _sum_devices(gathered, "sum_small_grads")
    pieces = _unpack(summed, [full_shapes[n] for n in small_names] + [(1, 1)])
    loss = pieces[-1].reshape(())
    small_grads = dict(zip(small_names, pieces[:-1]))
    for n in ("sc_conv_w", "ffn_conv_w"):
        c_loc = weights[n].shape[-1]
        small_grads[n] = lax.dynamic_slice_in_dim(small_grads[n], dev * c_loc, c_loc, axis=2)
    shapes = [weights[n].shape for n in small_names]
    d_p, m_p, v_p = _adamw_packed(_pack([weights[n] for n in small_names]), _pack([mom1[n] for n in small_names]),
                                  _pack([mom2[n] for n in small_names]), _pack([small_grads[n] for n in small_names]),
                                  "adamw_small")
    out_g, out_d, out_m, out_v = dict(small_grads), {}, {}, {}
    for n, d_, m_, v_ in zip(small_names, _unpack(d_p, shapes), _unpack(m_p, shapes), _unpack(v_p, shapes)):
        out_d[n], out_m[n], out_v[n] = d_, m_, v_

    for n in order:
        if n in small_names:
            continue
        res = [_adamw_shard(weights[n][j], mom1[n][j], mom2[n][j], *grads_big[(n, j)], chip_arr, f"adamw_{n}")
               for j in range(weights[n].shape[0])]
        out_g[n], out_d[n], out_m[n], out_v[n] = (jnp.stack([r_[t] for r_ in res]) for t in range(4))

    return (loss, grad_x, *[out_g[n] for n in order], *[out_d[n] for n in order],
            *[out_m[n] for n in order], *[out_v[n] for n in order])
```

```python
import functools

import jax
import jax.numpy as jnp
from jax import lax
from jax.experimental import pallas as pl
from jax.experimental.pallas import tpu as pltpu

F32 = jnp.float32
BF16 = jnp.bfloat16

EPS = 1e-6
CHUNK = 128
HEAD_DIM = 128
SG_GROUPS = 8
HALO = 16
N_DEV = 8
LANES = 128
V7X_VMEM_LIMIT = 56 * 1024 * 1024

ADAM_LR = 0.001
ADAM_B1 = 0.9
ADAM_B2 = 0.999
ADAM_EPS = 1e-08
ADAM_WD = 0.01
ADAM_STEP = 10

MESH = pl.DeviceIdType.MESH
ANY = pl.BlockSpec(memory_space=pl.ANY)


def _tile(n, prefs):
    for p in prefs:
        if p <= n and n % p == 0:
            return p
    return n


def _params(*sem):
    return pltpu.CompilerParams(dimension_semantics=sem, vmem_limit_bytes=V7X_VMEM_LIMIT)


def _shard_of(ref, dev, axis, r, c):
    if axis == 0:
        return ref.at[pl.ds(dev * r, r), :]
    return ref.at[:, pl.ds(dev * c, c)]


class _GatherJob:
    def __init__(self, x, axis):
        self.axis = axis
        self.r, self.c = x.shape
        full = (N_DEV * self.r, self.c) if axis == 0 else (self.r, N_DEV * self.c)
        self.inputs = [x]
        self.out_shapes = [jax.ShapeDtypeStruct(full, x.dtype)]
        self.scratch = [pltpu.SemaphoreType.DMA((7,)), pltpu.SemaphoreType.DMA((7,)), pltpu.SemaphoreType.DMA(())]

    def _plan(self, ins, outs, scr):
        (x_ref,), (out_ref,), (send_sems, recv_sems, local_sem) = ins, outs, scr
        mx, my, mc = lax.axis_index("x"), lax.axis_index("y"), lax.axis_index("c")
        me, sibling = (mx, my, mc), (mx, my, 1 - mc)
        chips = [(1 - mx, my), (mx, 1 - my), (1 - mx, 1 - my)]

        def rows(px, py, pc):
            return _shard_of(out_ref, 4 * px + 2 * py + pc, self.axis, self.r, self.c)

        def copy(k, block, to, src=None):
            return pltpu.make_async_remote_copy(
                src_ref=rows(*block) if src is None else src, dst_ref=rows(*block),
                send_sem=send_sems.at[k], recv_sem=recv_sems.at[k], device_id=to, device_id_type=MESH)

        mine = pltpu.make_async_copy(x_ref, rows(*me), local_sem)
        first = [copy(0, me, sibling, src=x_ref)] + [copy(1 + j, me, (*chip, mc), src=x_ref)
                                                      for j, chip in enumerate(chips)]
        passed = [copy(4 + j, (*chip, mc), sibling) for j, chip in enumerate(chips)]
        landed = [copy(1 + j, (*chip, mc), me) for j, chip in enumerate(chips)]
        from_sibling = [copy(0, sibling, me)] + [copy(4 + j, (*chip, 1 - mc), me) for j, chip in enumerate(chips)]
        return mine, first, passed, landed, from_sibling

    def start(self, ins, outs, scr):
        mine, first, _, _, _ = self._plan(ins, outs, scr)
        mine.start()
        for cp in first:
            cp.start()

    def finish(self, ins, outs, scr):
        mine, first, passed, landed, from_sibling = self._plan(ins, outs, scr)
        for j in range(3):
            landed[j].wait_recv()
            passed[j].start()
        for cp in from_sibling:
            cp.wait_recv()
        for cp in first + passed:
            cp.wait_send()
        mine.wait()


class _ChipExchangeJob:
    def __init__(self, p):
        _, r, c = p.shape
        self.inputs = [p]
        self.out_shapes = [jax.ShapeDtypeStruct((3, r, c), p.dtype)]
        self.scratch = [pltpu.SemaphoreType.DMA((3,)), pltpu.SemaphoreType.DMA((3,))]

    def _plan(self, ins, outs, scr):
        (p_ref,), (recv_ref,), (send_sems, recv_sems) = ins, outs, scr
        mx, my, mc = lax.axis_index("x"), lax.axis_index("y"), lax.axis_index("c")
        chips = [(1 - mx, my), (mx, 1 - my), (1 - mx, 1 - my)]
        return [pltpu.make_async_remote_copy(
            src_ref=p_ref.at[2 * px + py], dst_ref=recv_ref.at[j],
            send_sem=send_sems.at[j], recv_sem=recv_sems.at[j], device_id=(px, py, mc), device_id_type=MESH)
            for j, (px, py) in enumerate(chips)]

    def start(self, ins, outs, scr):
        for cp in self._plan(ins, outs, scr):
            cp.start()

    def finish(self, ins, outs, scr):
        for cp in self._plan(ins, outs, scr):
            cp.wait()


class _Hosted:
    def __init__(self, jobs):
        self.jobs = list(jobs)
        self.inputs = [x for j in self.jobs for x in j.inputs]
        self.out_shapes = [s for j in self.jobs for s in j.out_shapes]
        self.scratch = [s for j in self.jobs for s in j.scratch]
        self.in_specs = [ANY] * len(self.inputs)
        self.out_specs = [ANY] * len(self.out_shapes)

    def split(self, refs, n_in, n_out, n_scratch):
        refs = list(refs)
        ji, jo, js = len(self.inputs), len(self.out_shapes), len(self.scratch)
        ins, refs = refs[:n_in], refs[n_in:]
        jins, refs = refs[:ji], refs[ji:]
        outs, refs = refs[:n_out], refs[n_out:]
        jouts, refs = refs[:jo], refs[jo:]
        scr, jscr = refs[:n_scratch], refs[n_scratch:]
        assert len(jscr) == js
        per_job = []
        for j in self.jobs:
            a, b, c = len(j.inputs), len(j.out_shapes), len(j.scratch)
            per_job.append((jins[:a], jouts[:b], jscr[:c]))
            jins, jouts, jscr = jins[a:], jouts[b:], jscr[c:]
        return ins, outs, scr, per_job

    def start(self, per_job, when):
        if self.jobs:
            @pl.when(when)
            def _():
                for j, refs in zip(self.jobs, per_job):
                    j.start(*refs)

    def finish(self, per_job, when):
        if self.jobs:
            @pl.when(when)
            def _():
                for j, refs in zip(self.jobs, per_job):
                    j.finish(*refs)


def _grid_edges(grid):
    first = last = None
    for ax, n in enumerate(grid):
        f, l = pl.program_id(ax) == 0, pl.program_id(ax) == n - 1
        first = f if first is None else jnp.logical_and(first, f)
        last = l if last is None else jnp.logical_and(last, l)
    return first, last


def _run_jobs(jobs, name):
    hosted = _Hosted(jobs)

    def body(*refs):
        _, _, _, per_job = hosted.split(refs, 0, 0, 0)
        for j, r in zip(hosted.jobs, per_job):
            j.start(*r)
        for j, r in zip(hosted.jobs, per_job):
            j.finish(*r)

    return pl.pallas_call(
        body, name=name, out_shape=tuple(hosted.out_shapes), in_specs=hosted.in_specs,
        out_specs=tuple(hosted.out_specs), scratch_shapes=hosted.scratch,
    )(*hosted.inputs)


def _mm_nn(a, b, out_dtype, nsplit, name, jobs=()):
    m, k = a.shape
    n = b.shape[1]
    w = n // nsplit
    tm = _tile(m, (1024, 512, 256, 128)) if k <= 2048 else _tile(m, (512, 256, 128))
    tn = _tile(w, (512, 256, 128))
    per = w // tn
    grid = (m // tm, n // tn)
    hosted = _Hosted(jobs)

    def body(*refs):
        (a_ref, b_ref), (o_ref,), _, per_job = hosted.split(refs, 2, 1, 0)
        first, last = _grid_edges(grid)
        hosted.start(per_job, first)
        o_ref[...] = jnp.dot(a_ref[...], b_ref[...], preferred_element_type=F32).astype(o_ref.dtype)
        hosted.finish(per_job, last)

    if nsplit == 1:
        out_shape = jax.ShapeDtypeStruct((m, n), out_dtype)
        out_spec = pl.BlockSpec((tm, tn), lambda i, j: (i, j))
    else:
        out_shape = jax.ShapeDtypeStruct((nsplit, m, w), out_dtype)
        out_spec = pl.BlockSpec((None, tm, tn), lambda i, j: (j // per, i, j % per))
    res = pl.pallas_call(
        body, grid=grid, name=name,
        in_specs=[pl.BlockSpec((tm, k), lambda i, j: (i, 0)), pl.BlockSpec((k, tn), lambda i, j: (0, j))]
        + hosted.in_specs,
        out_specs=tuple([out_spec] + hosted.out_specs), out_shape=tuple([out_shape] + hosted.out_shapes),
        scratch_shapes=hosted.scratch,
        compiler_params=_params("arbitrary", "arbitrary"),
    )(a, b, *hosted.inputs)
    return res if jobs else res[0]


def _mm_nt(a, b, out_dtype, name, jobs=()):
    a3 = a if a.ndim == 3 else a[None]
    ns, m, w = a3.shape
    ko, n = b.shape
    assert n == ns * w
    tm = _tile(m, (512, 256, 128))
    tko = _tile(ko, (256, 128)) if n > 4096 else _tile(ko, (512, 256, 128))
    grid = (m // tm, ko // tko)
    hosted = _Hosted(jobs)

    def body(*refs):
        (a_ref, b_ref), (o_ref,), _, per_job = hosted.split(refs, 2, 1, 0)
        first, last = _grid_edges(grid)
        hosted.start(per_job, first)
        acc = None
        for s in range(ns):
            p = lax.dot_general(a_ref[s], b_ref[:, s * w:(s + 1) * w], (((1,), (1,)), ((), ())),
                                preferred_element_type=F32)
            acc = p if acc is None else acc + p
        o_ref[...] = acc.astype(o_ref.dtype)
        hosted.finish(per_job, last)

    res = pl.pallas_call(
        body, grid=grid, name=name,
        in_specs=[pl.BlockSpec((ns, tm, w), lambda i, j: (0, i, 0)), pl.BlockSpec((tko, n), lambda i, j: (j, 0))]
        + hosted.in_specs,
        out_specs=tuple([pl.BlockSpec((tm, tko), lambda i, j: (i, j))] + hosted.out_specs),
        out_shape=tuple([jax.ShapeDtypeStruct((m, ko), out_dtype)] + hosted.out_shapes),
        scratch_shapes=hosted.scratch,
        compiler_params=_params("arbitrary", "arbitrary"),
    )(a3, b, *hosted.inputs)
    return res if jobs else res[0]


def _mm_tn(a, b, out_dtype, name, jobs=()):
    b3 = b if b.ndim == 3 else b[None]
    ns, m, w = b3.shape
    k = a.shape[1]
    n = ns * w
    tk = _tile(k, (512, 256, 128))
    tn = _tile(w, (2816, 2048, 1024, 512, 256, 128))
    ts = _tile(m, (1024, 512, 256, 128))
    per = w // tn
    n_s = m // ts
    grid = (k // tk, n // tn, n_s)
    hosted = _Hosted(jobs)

    def body(*refs):
        (a_ref, b_ref), (o_ref,), (acc_ref,), per_job = hosted.split(refs, 2, 1, 1)
        s = pl.program_id(2)
        first, last = _grid_edges(grid)
        hosted.start(per_job, first)

        @pl.when(s == 0)
        def _():
            acc_ref[...] = jnp.zeros_like(acc_ref)

        acc_ref[...] += lax.dot_general(a_ref[...], b_ref[...], (((0,), (0,)), ((), ())),
                                        preferred_element_type=F32)

        @pl.when(s == n_s - 1)
        def _():
            o_ref[...] = acc_ref[...].astype(o_ref.dtype)

        hosted.finish(per_job, last)

    res = pl.pallas_call(
        body, grid=grid, name=name,
        in_specs=[pl.BlockSpec((ts, tk), lambda i, j, s: (s, i)),
                  pl.BlockSpec((None, ts, tn), lambda i, j, s: (j // per, s, j % per))] + hosted.in_specs,
        out_specs=tuple([pl.BlockSpec((tk, tn), lambda i, j, s: (i, j))] + hosted.out_specs),
        out_shape=tuple([jax.ShapeDtypeStruct((k, n), out_dtype)] + hosted.out_shapes),
        scratch_shapes=[pltpu.VMEM((tk, tn), F32)] + hosted.scratch,
        compiler_params=_params("arbitrary", "arbitrary", "arbitrary"),
    )(a, b3, *hosted.inputs)
    return res if jobs else res[0]


def _rms_rows(s):
    return _tile(s, (256, 128))


def _rms_fwd(x, g, name):
    s, d = x.shape
    ts = _rms_rows(s)

    def body(x_ref, g_ref, o_ref):
        xf = x_ref[...]
        r = lax.rsqrt(jnp.mean(xf * xf, axis=-1, keepdims=True) + EPS)
        o_ref[...] = ((xf * r) * g_ref[...]).astype(o_ref.dtype)

    return pl.pallas_call(
        body, grid=(s // ts,), name=name,
        in_specs=[pl.BlockSpec((ts, d), lambda i: (i, 0)), pl.BlockSpec((1, d), lambda i: (0, 0))],
        out_specs=pl.BlockSpec((ts, d), lambda i: (i, 0)),
        out_shape=jax.ShapeDtypeStruct((s, d), BF16),
        compiler_params=_params("parallel"),
    )(x, g)


def _resid_rms_fwd(h, m, g, name):
    s, d = h.shape
    ts = _rms_rows(s)

    def body(h_ref, m_ref, g_ref, o_ref):
        mf = m_ref[...]
        r = lax.rsqrt(jnp.mean(mf * mf, axis=-1, keepdims=True) + EPS)
        o_ref[...] = h_ref[...] + (mf * r) * g_ref[...]

    return pl.pallas_call(
        body, grid=(s // ts,), name=name,
        in_specs=[pl.BlockSpec((ts, d), lambda i: (i, 0)), pl.BlockSpec((ts, d), lambda i: (i, 0)),
                  pl.BlockSpec((1, d), lambda i: (0, 0))],
        out_specs=pl.BlockSpec((ts, d), lambda i: (i, 0)),
        out_shape=jax.ShapeDtypeStruct((s, d), F32),
        compiler_params=_params("parallel"),
    )(h, m, g)


def _rms_bwd(x, g, dy, dres, out_dtype, name):
    s, d = x.shape
    ts = _rms_rows(s)
    has_res = dres is not None

    def body(*refs):
        if has_res:
            x_ref, g_ref, dy_ref, dres_ref, dx_ref, dg_ref = refs
        else:
            x_ref, g_ref, dy_ref, dx_ref, dg_ref = refs
        i = pl.program_id(0)
        xf = x_ref[...]
        dyf = dy_ref[...].astype(F32)
        r = lax.rsqrt(jnp.mean(xf * xf, axis=-1, keepdims=True) + EPS)
        xh = xf * r
        dxh = dyf * g_ref[...]
        dx = r * (dxh - xh * jnp.mean(dxh * xh, axis=-1, keepdims=True))
        if has_res:
            dx = dx + dres_ref[...]
        dx_ref[...] = dx.astype(dx_ref.dtype)

        @pl.when(i == 0)
        def _():
            dg_ref[...] = jnp.zeros_like(dg_ref)

        dg_ref[...] += jnp.sum(dyf * xh, axis=0, keepdims=True)

    row = pl.BlockSpec((ts, d), lambda i: (i, 0))
    vec = pl.BlockSpec((1, d), lambda i: (0, 0))
    ins = [x, g, dy] + ([dres] if has_res else [])
    return pl.pallas_call(
        body, grid=(s // ts,), name=name,
        in_specs=[row, vec, row] + ([row] if has_res else []),
        out_specs=(row, vec),
        out_shape=(jax.ShapeDtypeStruct((s, d), out_dtype), jax.ShapeDtypeStruct((1, d), F32)),
        compiler_params=_params("arbitrary"),
    )(*ins)


def _loss_head(h, target, name):
    s, d = h.shape
    ts = _rms_rows(s)
    n_i = s // ts

    def body(h_ref, t_ref, dh_ref, loss_ref, acc_ref):
        i = pl.program_id(0)
        err = h_ref[...] - t_ref[...]
        dh_ref[...] = err * (1.0 / d)

        @pl.when(i == 0)
        def _():
            acc_ref[...] = jnp.zeros_like(acc_ref)

        acc_ref[...] += jnp.sum(err * err, axis=0, keepdims=True)

        @pl.when(i == n_i - 1)
        def _():
            loss_ref[...] = jnp.sum(acc_ref[...], axis=1, keepdims=True) * (0.5 / d)

    row = pl.BlockSpec((ts, d), lambda i: (i, 0))
    return pl.pallas_call(
        body, grid=(n_i,), name=name,
        in_specs=[row, row],
        out_specs=(row, pl.BlockSpec((1, 1), lambda i: (0, 0))),
        out_shape=(jax.ShapeDtypeStruct((s, d), F32), jax.ShapeDtypeStruct((1, 1), F32)),
        scratch_shapes=[pltpu.VMEM((1, d), F32)],
        compiler_params=_params("arbitrary"),
    )(h, target)


def _halo_specs(ns, s, ts, tc):
    hb = ts // HALO
    last = s // HALO - 1
    cur = pl.BlockSpec((ns, ts, tc), lambda j, i: (0, i, j))
    prev = pl.BlockSpec((ns, HALO, tc), lambda j, i: (0, jnp.maximum(i * hb - 1, 0), j))
    nxt = pl.BlockSpec((ns, HALO, tc), lambda j, i: (0, jnp.minimum((i + 1) * hb, last), j))
    return cur, prev, nxt


def _ext(prev, cur, nxt, first, last):
    p = jnp.where(first, 0.0, prev.astype(F32))
    n = jnp.where(last, 0.0, nxt.astype(F32))
    return jnp.concatenate([p, cur.astype(F32), n], axis=0)


def _dn(xe, k):
    return xe if k == 0 else pltpu.roll(xe, k, 0)


def _up(xe, k):
    return xe if k == 0 else pltpu.roll(xe, xe.shape[0] - k, 0)


def _conv_e(xe, w):
    return w[0:1, :] * _dn(xe, 2) + w[1:2, :] * _dn(xe, 1) + w[2:3, :] * xe


def _conv_t_e(de, w):
    return w[2:3, :] * de + w[1:2, :] * _up(de, 1) + w[0:1, :] * _up(de, 2)


def _conv_dw(de, xe, ts):
    return [jnp.sum((de * _dn(xe, 2 - k))[HALO:HALO + ts], axis=0, keepdims=True) for k in range(3)]


def _conv_tiles(s, c):
    return _tile(s, (256, 128)), _tile(c, (512, 256, 128))


def _sc_gate_fwd(p, w, name):
    _, s, c = p.shape
    ts, tc = _conv_tiles(s, c)
    n_i = s // ts

    def body(p_ref, pp_ref, w_ref, y_ref):
        i = pl.program_id(1)
        ze = _ext(pp_ref[1], p_ref[1], pp_ref[1], i == 0, True) * _ext(pp_ref[2], p_ref[2], pp_ref[2], i == 0, True)
        cz = _conv_e(ze, w_ref[...])[HALO:HALO + ts]
        y_ref[...] = (p_ref[0].astype(F32) * cz).astype(y_ref.dtype)

    cur, prev, _ = _halo_specs(3, s, ts, tc)
    return pl.pallas_call(
        body, grid=(c // tc, n_i), name=name,
        in_specs=[cur, prev, pl.BlockSpec((3, tc), lambda j, i: (0, j))],
        out_specs=pl.BlockSpec((ts, tc), lambda j, i: (i, j)),
        out_shape=jax.ShapeDtypeStruct((s, c), BF16),
        compiler_params=_params("parallel", "arbitrary"),
    )(p, p, w)


def _sc_gate_bwd(p, dy, w, name):
    _, s, c = p.shape
    ts, tc = _conv_tiles(s, c)
    n_i = s // ts

    def body(p_ref, pp_ref, pn_ref, dy_ref, dyn_ref, w_ref, dp_ref, dw_ref):
        i = pl.program_id(1)
        first, last = i == 0, i == n_i - 1
        wv = w_ref[...]
        gbe = _ext(pp_ref[0], p_ref[0], pn_ref[0], first, last)
        gce = _ext(pp_ref[1], p_ref[1], pn_ref[1], first, last)
        hhe = _ext(pp_ref[2], p_ref[2], pn_ref[2], first, last)
        dye = _ext(dyn_ref[...], dy_ref[...], dyn_ref[...], True, last)
        ze = gce * hhe
        cze = _conv_e(ze, wv)
        dcze = dye * gbe
        dze = _conv_t_e(dcze, wv)
        rows = slice(HALO, HALO + ts)
        dp_ref[0] = (dye * cze)[rows].astype(dp_ref.dtype)
        dp_ref[1] = (dze * hhe)[rows].astype(dp_ref.dtype)
        dp_ref[2] = (dze * gce)[rows].astype(dp_ref.dtype)

        @pl.when(first)
        def _():
            dw_ref[...] = jnp.zeros_like(dw_ref)

        dws = _conv_dw(dcze, ze, ts)
        for k in range(3):
            dw_ref[k:k + 1, :] += dws[k]

    cur, prev, nxt = _halo_specs(3, s, ts, tc)
    hb = ts // HALO
    last_h = s // HALO - 1
    return pl.pallas_call(
        body, grid=(c // tc, n_i), name=name,
        in_specs=[cur, prev, nxt,
                  pl.BlockSpec((ts, tc), lambda j, i: (i, j)),
                  pl.BlockSpec((HALO, tc), lambda j, i: (jnp.minimum((i + 1) * hb, last_h), j)),
                  pl.BlockSpec((3, tc), lambda j, i: (0, j))],
        out_specs=(cur, pl.BlockSpec((3, tc), lambda j, i: (0, j))),
        out_shape=(jax.ShapeDtypeStruct((3, s, c), BF16), jax.ShapeDtypeStruct((3, c), F32)),
        compiler_params=_params("parallel", "arbitrary"),
    )(p, p, p, dy, dy, w)


def _ffn_act_fwd(u, w, b, name):
    _, s, f = u.shape
    ts, tc = _conv_tiles(s, f)
    n_i = s // ts

    def body(u_ref, up_ref, w_ref, b_ref, a_ref):
        i = pl.program_id(1)
        rows = slice(HALO, HALO + ts)
        cg = _conv_e(_ext(up_ref[0], u_ref[0], up_ref[0], i == 0, True), w_ref[0])[rows] + b_ref[0]
        cv = _conv_e(_ext(up_ref[1], u_ref[1], up_ref[1], i == 0, True), w_ref[1])[rows] + b_ref[1]
        a_ref[...] = (cg * jax.nn.sigmoid(cg) * cv).astype(a_ref.dtype)

    cur, prev, _ = _halo_specs(2, s, ts, tc)
    return pl.pallas_call(
        body, grid=(f // tc, n_i), name=name,
        in_specs=[cur, prev, pl.BlockSpec((2, 3, tc), lambda j, i: (0, 0, j)),
                  pl.BlockSpec((2, 1, tc), lambda j, i: (0, 0, j))],
        out_specs=pl.BlockSpec((ts, tc), lambda j, i: (i, j)),
        out_shape=jax.ShapeDtypeStruct((s, f), BF16),
        compiler_params=_params("parallel", "arbitrary"),
    )(u, u, w, b)


def _ffn_act_bwd(u, da, w, b, name):
    _, s, f = u.shape
    ts, tc = _conv_tiles(s, f)
    n_i = s // ts

    def body(u_ref, up_ref, un_ref, da_ref, dan_ref, w_ref, b_ref, du_ref, dw_ref, db_ref):
        i = pl.program_id(1)
        first, last = i == 0, i == n_i - 1
        rows = slice(HALO, HALO + ts)
        uge = _ext(up_ref[0], u_ref[0], un_ref[0], first, last)
        uve = _ext(up_ref[1], u_ref[1], un_ref[1], first, last)
        dae = _ext(dan_ref[...], da_ref[...], dan_ref[...], True, last)
        cg = _conv_e(uge, w_ref[0]) + b_ref[0]
        cv = _conv_e(uve, w_ref[1]) + b_ref[1]
        sg = jax.nn.sigmoid(cg)
        dcg = dae * cv * (sg * (1.0 + cg * (1.0 - sg)))
        dcv = dae * (cg * sg)
        du_ref[0] = _conv_t_e(dcg, w_ref[0])[rows].astype(du_ref.dtype)
        du_ref[1] = _conv_t_e(dcv, w_ref[1])[rows].astype(du_ref.dtype)

        @pl.when(first)
        def _():
            dw_ref[...] = jnp.zeros_like(dw_ref)
            db_ref[...] = jnp.zeros_like(db_ref)

        for h, (de, xe) in enumerate(((dcg, uge), (dcv, uve))):
            dws = _conv_dw(de, xe, ts)
            for k in range(3):
                dw_ref[h, k:k + 1, :] += dws[k]
            db_ref[h] += jnp.sum(de[rows], axis=0, keepdims=True)

    cur, prev, nxt = _halo_specs(2, s, ts, tc)
    hb = ts // HALO
    last_h = s // HALO - 1
    wspec = pl.BlockSpec((2, 3, tc), lambda j, i: (0, 0, j))
    bspec = pl.BlockSpec((2, 1, tc), lambda j, i: (0, 0, j))
    return pl.pallas_call(
        body, grid=(f // tc, n_i), name=name,
        in_specs=[cur, prev, nxt,
                  pl.BlockSpec((ts, tc), lambda j, i: (i, j)),
                  pl.BlockSpec((HALO, tc), lambda j, i: (jnp.minimum((i + 1) * hb, last_h), j)),
                  wspec, bspec],
        out_specs=(cur, wspec, bspec),
        out_shape=(jax.ShapeDtypeStruct((2, s, f), BF16), jax.ShapeDtypeStruct((2, 3, f), F32),
                   jax.ShapeDtypeStruct((2, 1, f), F32)),
        compiler_params=_params("parallel", "arbitrary"),
    )(u, u, u, da, da, w, b)


_INV_SQRT2 = 0.7071067811865476
_INV_SQRT_2PI = 0.3989422804014327


def _gelu(x):
    return 0.5 * x * (1.0 + lax.erf(x * _INV_SQRT2))


def _gelu_grad(x):
    return 0.5 * (1.0 + lax.erf(x * _INV_SQRT2)) + x * (_INV_SQRT_2PI * jnp.exp(-0.5 * x * x))


def _tril_bf16(w):
    t = lax.broadcasted_iota(jnp.int32, w.shape, 0)
    s = lax.broadcasted_iota(jnp.int32, w.shape, 1)
    return jnp.where(s <= t, w, 0.0).astype(BF16)


def _sgu_fwd(p, ln_g, ln_b, ws, bs, name):
    _, s, c = p.shape
    g_n = ws.shape[0]
    cg = c // g_n

    def body(p_ref, lg_ref, lb_ref, ws_ref, bs_ref, y_ref):
        u = _gelu(p_ref[0].astype(F32))
        v = _gelu(p_ref[1].astype(F32))
        mu = jnp.mean(v, axis=-1, keepdims=True)
        vc = v - mu
        rstd = lax.rsqrt(jnp.mean(vc * vc, axis=-1, keepdims=True) + EPS)
        vn = ((vc * rstd) * lg_ref[...] + lb_ref[...]).astype(BF16)
        for g in range(g_n):
            cols = slice(g * cg, (g + 1) * cg)
            mixed = jnp.dot(_tril_bf16(ws_ref[g]), vn[:, cols], preferred_element_type=F32) + bs_ref[g]
            y_ref[:, cols] = (u[:, cols] * mixed).astype(y_ref.dtype)

    vec = pl.BlockSpec((1, c), lambda i: (0, 0))
    return pl.pallas_call(
        body, grid=(s // CHUNK,), name=name,
        in_specs=[pl.BlockSpec((2, CHUNK, c), lambda i: (0, i, 0)), vec, vec,
                  pl.BlockSpec((g_n, CHUNK, CHUNK), lambda i: (0, 0, 0)),
                  pl.BlockSpec((g_n, CHUNK, 1), lambda i: (0, 0, 0))],
        out_specs=pl.BlockSpec((CHUNK, c), lambda i: (i, 0)),
        out_shape=jax.ShapeDtypeStruct((s, c), BF16),
        compiler_params=_params("parallel"),
    )(p, ln_g, ln_b, ws, bs)


def _sgu_bwd(p, dy, ln_g, ln_b, ws, ws_t, bs, name):
    _, s, c = p.shape
    g_n = ws.shape[0]
    cg = c // g_n

    def body(p_ref, dy_ref, lg_ref, lb_ref, ws_ref, wst_ref, bs_ref, dp_ref, dlg_ref, dlb_ref, dws_ref, dbs_ref,
             dvn_ref):
        i = pl.program_id(0)
        pu = p_ref[0].astype(F32)
        pv = p_ref[1].astype(F32)
        dyf = dy_ref[...].astype(F32)
        u = _gelu(pu)
        v = _gelu(pv)
        mu = jnp.mean(v, axis=-1, keepdims=True)
        vc = v - mu
        rstd = lax.rsqrt(jnp.mean(vc * vc, axis=-1, keepdims=True) + EPS)
        vhat = vc * rstd
        vn = (vhat * lg_ref[...] + lb_ref[...]).astype(BF16)

        @pl.when(i == 0)
        def _():
            dlg_ref[...] = jnp.zeros_like(dlg_ref)
            dlb_ref[...] = jnp.zeros_like(dlb_ref)
            dws_ref[...] = jnp.zeros_like(dws_ref)
            dbs_ref[...] = jnp.zeros_like(dbs_ref)

        t_i = lax.broadcasted_iota(jnp.int32, (CHUNK, CHUNK), 0)
        s_i = lax.broadcasted_iota(jnp.int32, (CHUNK, CHUNK), 1)
        for g in range(g_n):
            cols = slice(g * cg, (g + 1) * cg)
            mixed = jnp.dot(_tril_bf16(ws_ref[g]), vn[:, cols], preferred_element_type=F32) + bs_ref[g]
            dp_ref[0, :, cols] = (dyf[:, cols] * mixed * _gelu_grad(pu[:, cols])).astype(dp_ref.dtype)
            dmixed = dyf[:, cols] * u[:, cols]
            dmb = dmixed.astype(BF16)
            wt = jnp.where(t_i <= s_i, wst_ref[g], 0.0).astype(BF16)
            dvn_ref[:, cols] = jnp.dot(wt, dmb, preferred_element_type=F32)
            dwg = lax.dot_general(dmb, vn[:, cols], (((1,), (1,)), ((), ())), preferred_element_type=F32)
            dws_ref[g] += jnp.where(s_i <= t_i, dwg, 0.0)
            dbs_ref[g] += jnp.sum(dmixed, axis=1, keepdims=True)
        dvn = dvn_ref[...]
        dlg_ref[...] += jnp.sum(dvn * vhat, axis=0, keepdims=True)
        dlb_ref[...] += jnp.sum(dvn, axis=0, keepdims=True)
        dvh = dvn * lg_ref[...]
        dv = rstd * (dvh - jnp.mean(dvh, axis=-1, keepdims=True) - vhat * jnp.mean(dvh * vhat, axis=-1, keepdims=True))
        dp_ref[1] = (dv * _gelu_grad(pv)).astype(dp_ref.dtype)

    vec = pl.BlockSpec((1, c), lambda i: (0, 0))
    wspec = pl.BlockSpec((g_n, CHUNK, CHUNK), lambda i: (0, 0, 0))
    bspec = pl.BlockSpec((g_n, CHUNK, 1), lambda i: (0, 0, 0))
    pspec = pl.BlockSpec((2, CHUNK, c), lambda i: (0, i, 0))
    return pl.pallas_call(
        body, grid=(s // CHUNK,), name=name,
        in_specs=[pspec, pl.BlockSpec((CHUNK, c), lambda i: (i, 0)), vec, vec, wspec, wspec, bspec],
        out_specs=(pspec, vec, vec, wspec, bspec),
        out_shape=(jax.ShapeDtypeStruct((2, s, c), BF16), jax.ShapeDtypeStruct((1, c), F32),
                   jax.ShapeDtypeStruct((1, c), F32), jax.ShapeDtypeStruct((g_n, CHUNK, CHUNK), F32),
                   jax.ShapeDtypeStruct((g_n, CHUNK, 1), F32)),
        scratch_shapes=[pltpu.VMEM((CHUNK, c), F32)],
        compiler_params=_params("arbitrary"),
    )(p, dy, ln_g, ln_b, ws, ws_t, bs)


SB_Q = 512


def _sb_cumsum(x, tri, edge, reverse):
    n = x.shape[0]
    hi = x.astype(BF16)
    lo = (x - hi.astype(F32)).astype(BF16)
    chunks = x.shape[1] // CHUNK
    outs = [None] * chunks
    for c in (reversed(range(chunks)) if reverse else range(chunks)):
        cols = slice(c * CHUNK, (c + 1) * CHUNK)
        both = jnp.dot(jnp.concatenate([hi[:, cols], lo[:, cols]], axis=0), tri, preferred_element_type=F32)
        outs[c] = both[:n] + both[n:] + edge
        edge = edge + jnp.sum(x[:, cols], axis=1, keepdims=True)
    return jnp.concatenate(outs, axis=1), edge


def _sb_block(q, k, offset):
    scale = HEAD_DIM ** -0.5
    z = lax.dot_general(q, k, (((1,), (1,)), ((), ())), preferred_element_type=F32) * scale
    e = jnp.exp(-jnp.abs(z))
    lp = jnp.log(1.0 + e)
    lb = jnp.minimum(z, 0.0) - lp
    l1 = jnp.minimum(-z, 0.0) - lp
    if offset is not None:
        t_i = lax.broadcasted_iota(jnp.int32, z.shape, 0)
        s_i = lax.broadcasted_iota(jnp.int32, z.shape, 1)
        mask = (t_i - s_i) > offset
        l1 = jnp.where(mask, l1, 0.0)
    else:
        mask = None
    return z, e, lb, l1, mask


def _tri(cond):
    j = lax.broadcasted_iota(jnp.int32, (CHUNK, CHUNK), 0)
    s = lax.broadcasted_iota(jnp.int32, (CHUNK, CHUNK), 1)
    return jnp.where(cond(j, s), 1.0, 0.0).astype(BF16)


def _sb_fwd(qkv, name):
    _, s, c = qkv.shape
    heads = c // HEAD_DIM
    bq = _tile(s, (SB_Q, 256, CHUNK))
    nq = s // bq

    def body(q_ref, k_ref, v_ref, o_ref):
        i = pl.program_id(1)
        q = q_ref[...]
        after = _tri(lambda j, s_: j > s_)

        def tile(g, carry, offset):
            o_acc, run = carry
            rows = pl.ds(pl.multiple_of(g * bq, bq), bq)
            _, _, lb, l1, mask = _sb_block(q, k_ref[rows, :], offset)
            acc, run = _sb_cumsum(l1, after, run, True)
            a = jnp.exp(lb + acc)
            if mask is not None:
                a = jnp.where(mask, a, 0.0)
            o_acc = o_acc + jnp.dot(a.astype(BF16), v_ref[rows, :], preferred_element_type=F32)
            return o_acc, run

        carry = tile(i, (jnp.zeros((bq, HEAD_DIM), F32), jnp.zeros((bq, 1), F32)), 0)
        o_acc, _ = lax.fori_loop(0, i, lambda n, cr: tile(i - 1 - n, cr, None), carry)
        o_ref[...] = o_acc.astype(o_ref.dtype)

    return pl.pallas_call(
        body, grid=(heads, nq), name=name,
        in_specs=[pl.BlockSpec((None, bq, HEAD_DIM), lambda h, i: (0, i, h)),
                  pl.BlockSpec((None, s, HEAD_DIM), lambda h, i: (1, 0, h)),
                  pl.BlockSpec((None, s, HEAD_DIM), lambda h, i: (2, 0, h))],
        out_specs=pl.BlockSpec((bq, HEAD_DIM), lambda h, i: (i, h)),
        out_shape=jax.ShapeDtypeStruct((s, c), BF16),
        compiler_params=_params("parallel", "arbitrary"),
    )(qkv, qkv, qkv)


def _sb_bwd(qkv, do, name):
    _, s, c = qkv.shape
    heads = c // HEAD_DIM
    bq = _tile(s, (SB_Q, 256, CHUNK))
    nq = s // bq
    scale = HEAD_DIM ** -0.5
    tn_dims = (((0,), (0,)), ((), ()))

    def body(q_ref, k_ref, v_ref, do_ref, dq_ref, dkv_ref, run_ref, dk_acc, dv_acc):
        i = pl.program_id(1)
        q = q_ref[...]
        dob = do_ref[...]
        after = _tri(lambda j, s_: j > s_)
        before = _tri(lambda j, s_: j < s_)

        @pl.when(i == 0)
        def _():
            dk_acc[...] = jnp.zeros_like(dk_acc)
            dv_acc[...] = jnp.zeros_like(dv_acc)

        def pre(g, run, offset):
            rows = pl.ds(pl.multiple_of(g * bq, bq), bq)
            run_ref[g] = run
            _, _, _, l1, _ = _sb_block(q, k_ref[rows, :], offset)
            return run + jnp.sum(l1, axis=1, keepdims=True)

        run0 = pre(i, jnp.zeros((bq, 1), F32), 0)
        lax.fori_loop(0, i, lambda n, r: pre(i - 1 - n, r, None), run0)

        def main(g, carry, offset):
            dq_acc, csum = carry
            rows = pl.ds(pl.multiple_of(g * bq, bq), bq)
            kblk = k_ref[rows, :]
            vblk = v_ref[rows, :]
            z, e, lb, l1, mask = _sb_block(q, kblk, offset)
            acc, _ = _sb_cumsum(l1, after, run_ref[g], True)
            a = jnp.exp(lb + acc)
            if mask is not None:
                a = jnp.where(mask, a, 0.0)
            d_a = lax.dot_general(dob, vblk, (((1,), (1,)), ((), ())), preferred_element_type=F32)
            ee = a * d_a
            cc, csum = _sb_cumsum(ee, before, csum, False)
            beta = jnp.where(z >= 0.0, 1.0, e) / (1.0 + e)
            dz = (ee - (ee + cc) * beta) * scale
            if mask is not None:
                dz = jnp.where(mask, dz, 0.0)
            dzb = dz.astype(BF16)
            dq_acc = dq_acc + jnp.dot(dzb, kblk, preferred_element_type=F32)
            dk_acc[rows, :] += lax.dot_general(dzb, q, tn_dims, preferred_element_type=F32)
            dv_acc[rows, :] += lax.dot_general(a.astype(BF16), dob, tn_dims, preferred_element_type=F32)
            return dq_acc, csum

        carry = (jnp.zeros((bq, HEAD_DIM), F32), jnp.zeros((bq, 1), F32))
        carry = lax.fori_loop(0, i, lambda g, cr: main(g, cr, None), carry)
        dq_acc, _ = main(i, carry, 0)
        dq_ref[...] = dq_acc.astype(dq_ref.dtype)

        @pl.when(i == nq - 1)
        def _():
            dkv_ref[0] = dk_acc[...].astype(dkv_ref.dtype)
            dkv_ref[1] = dv_acc[...].astype(dkv_ref.dtype)

    blk = pl.BlockSpec((bq, HEAD_DIM), lambda h, i: (i, h))
    return pl.pallas_call(
        body, grid=(heads, nq), name=name,
        in_specs=[pl.BlockSpec((None, bq, HEAD_DIM), lambda h, i: (0, i, h)),
                  pl.BlockSpec((None, s, HEAD_DIM), lambda h, i: (1, 0, h)),
                  pl.BlockSpec((None, s, HEAD_DIM), lambda h, i: (2, 0, h)),
                  blk],
        out_specs=(blk, pl.BlockSpec((2, s, HEAD_DIM), lambda h, i: (0, 0, h))),
        out_shape=(jax.ShapeDtypeStruct((s, c), BF16), jax.ShapeDtypeStruct((2, s, c), BF16)),
        scratch_shapes=[pltpu.VMEM((nq, bq, 1), F32), pltpu.VMEM((s, HEAD_DIM), F32),
                        pltpu.VMEM((s, HEAD_DIM), F32)],
        compiler_params=_params("parallel", "arbitrary"),
    )(qkv, qkv, qkv, do)


def _all_gather(x, axis, name):
    return _run_jobs([_GatherJob(x, axis)], name)[0]


def _rs_core_exchange(g, axis, name):
    r, c = (g.shape[0] // N_DEV, g.shape[1]) if axis == 0 else (g.shape[0], g.shape[1] // N_DEV)

    def body(g_ref, recv_ref, send_sems, recv_sems):
        mx, my, mc = lax.axis_index("x"), lax.axis_index("y"), lax.axis_index("c")
        copies = [pltpu.make_async_remote_copy(
            src_ref=_shard_of(g_ref, 2 * k + (1 - mc), axis, r, c), dst_ref=recv_ref.at[k],
            send_sem=send_sems.at[k], recv_sem=recv_sems.at[k], device_id=(mx, my, 1 - mc), device_id_type=MESH)
            for k in range(4)]
        for cp in copies:
            cp.start()
        for cp in copies:
            cp.wait()

    return pl.pallas_call(
        body, name=name,
        out_shape=jax.ShapeDtypeStruct((4, r, c), g.dtype),
        in_specs=[ANY], out_specs=ANY,
        scratch_shapes=[pltpu.SemaphoreType.DMA((4,)), pltpu.SemaphoreType.DMA((4,))],
    )(g)


def _ew_tiles(r, c):
    tc = c if c <= 2048 else _tile(c, (2048, 1024, 512, 256, 128))
    return _tile(r, [p for p in (512, 256, 128, 64, 32, 16, 8) if p * tc <= 256 * 1024]), tc


def _rs_core_add(g, recv, core, axis, name):
    _, r, c = recv.shape
    tr, tc = _ew_tiles(r, c)
    nrb, ncb = r // tr, c // tc

    def body(core_ref, g_ref, x_ref, o_ref):
        del core_ref
        o_ref[...] = (g_ref[...].astype(F32) + x_ref[...].astype(F32)).astype(o_ref.dtype)

    if axis == 0:
        g_spec = pl.BlockSpec((tr, tc), lambda k, i, j, core: ((2 * k + core[0]) * nrb + i, j))
    else:
        g_spec = pl.BlockSpec((tr, tc), lambda k, i, j, core: (i, (2 * k + core[0]) * ncb + j))
    blk = pl.BlockSpec((None, tr, tc), lambda k, i, j, core: (k, i, j))
    return pl.pallas_call(
        body, name=name,
        grid_spec=pltpu.PrefetchScalarGridSpec(
            num_scalar_prefetch=1, grid=(4, nrb, ncb), in_specs=[g_spec, blk], out_specs=blk),
        out_shape=jax.ShapeDtypeStruct(recv.shape, BF16),
        compiler_params=_params("parallel", "parallel", "parallel"),
    )(core, g, recv)


def _rs_chip_exchange(parts, name):
    return _run_jobs([_ChipExchangeJob(p) for p in parts], name)


def _adamw_math(w, g, m, v):
    m = ADAM_B1 * m + (1.0 - ADAM_B1) * g
    v = ADAM_B2 * v + (1.0 - ADAM_B2) * (g * g)
    m_hat = m / (1.0 - ADAM_B1 ** ADAM_STEP)
    v_hat = v / (1.0 - ADAM_B2 ** ADAM_STEP)
    delta = -ADAM_LR * (m_hat / (jnp.sqrt(v_hat) + ADAM_EPS) + ADAM_WD * w)
    return delta, m, v


def _adamw_shard(w, m, v, p, recv, chip, name):
    r, c = w.shape
    tr, tc = _ew_tiles(r, c)

    def body(chip_ref, w_ref, m_ref, v_ref, p_ref, x_ref, g_out, d_out, m_out, v_out):
        del chip_ref
        g = p_ref[...].astype(F32)
        for j in range(3):
            g = g + x_ref[j].astype(F32)
        d, mn, vn = _adamw_math(w_ref[...], g, m_ref[...], v_ref[...])
        g_out[...] = g
        d_out[...] = d
        m_out[...] = mn
        v_out[...] = vn

    blk = pl.BlockSpec((tr, tc), lambda i, j, chip: (i, j))
    out = jax.ShapeDtypeStruct((r, c), F32)
    return pl.pallas_call(
        body, name=name,
        grid_spec=pltpu.PrefetchScalarGridSpec(
            num_scalar_prefetch=1, grid=(r // tr, c // tc),
            in_specs=[blk, blk, blk, pl.BlockSpec((None, tr, tc), lambda i, j, chip: (chip[0], i, j)),
                      pl.BlockSpec((3, tr, tc), lambda i, j, chip: (0, i, j))],
            out_specs=(blk, blk, blk, blk)),
        out_shape=(out, out, out, out),
        compiler_params=_params("parallel", "parallel"),
    )(chip, w, m, v, p, recv)


def _adamw_packed(w, m, v, g, name):
    r, c = w.shape
    tr = _tile(r, (512, 256, 128, 64, 32, 16, 8))

    def body(w_ref, m_ref, v_ref, g_ref, d_out, m_out, v_out):
        d, mn, vn = _adamw_math(w_ref[...], g_ref[...], m_ref[...], v_ref[...])
        d_out[...] = d
        m_out[...] = mn
        v_out[...] = vn

    blk = pl.BlockSpec((tr, c), lambda i: (i, 0))
    out = jax.ShapeDtypeStruct((r, c), F32)
    return pl.pallas_call(
        body, grid=(r // tr,), name=name, in_specs=[blk, blk, blk, blk], out_specs=(blk, blk, blk),
        out_shape=(out, out, out), compiler_params=_params("parallel"),
    )(w, m, v, g)


def _sum_devices(x, name):
    r = x.shape[0] // N_DEV
    tr = _tile(r, (512, 256, 128, 64, 32, 16, 8))

    def body(x_ref, o_ref):
        acc = x_ref[0]
        for d in range(1, N_DEV):
            acc = acc + x_ref[d]
        o_ref[...] = acc

    return pl.pallas_call(
        body, grid=(r // tr,), name=name,
        in_specs=[pl.BlockSpec((N_DEV, tr, LANES), lambda i: (0, i, 0))],
        out_specs=pl.BlockSpec((tr, LANES), lambda i: (i, 0)),
        out_shape=jax.ShapeDtypeStruct((r, LANES), F32),
        compiler_params=_params("parallel"),
    )(x.reshape(N_DEV, r, LANES))


def _pack(arrays):
    flat = []
    for a in arrays:
        f = a.reshape(-1).astype(F32)
        flat.append(jnp.pad(f, (0, (-f.shape[0]) % LANES)))
    f = jnp.concatenate(flat)
    f = jnp.pad(f, (0, (-f.shape[0]) % (8 * LANES)))
    return f.reshape(-1, LANES)


def _unpack(packed, shapes):
    flat = packed.reshape(-1)
    out, off = [], 0
    for shp in shapes:
        n = 1
        for d in shp:
            n *= d
        out.append(flat[off:off + n].reshape(shp))
        off += n + (-n) % LANES
    return out


def kernel(x, norm_mix_pre, norm_mix_post, norm_ffn_pre, norm_ffn_post, sc_w_in, sc_conv_w, sc_w_out, sg_w_in, sg_ln_g, sg_ln_b, sg_w_s, sg_b_s, sg_w_out, sb_w_qkv, sb_w_out, ffn_w_up, ffn_conv_w, ffn_conv_b, ffn_w_down, loss_target, m_norm_mix_pre, m_norm_mix_post, m_norm_ffn_pre, m_norm_ffn_post, m_sc_w_in, m_sc_conv_w, m_sc_w_out, m_sg_w_in, m_sg_ln_g, m_sg_ln_b, m_sg_w_s, m_sg_b_s, m_sg_w_out, m_sb_w_qkv, m_sb_w_out, m_ffn_w_up, m_ffn_conv_w, m_ffn_conv_b, m_ffn_w_down, v_norm_mix_pre, v_norm_mix_post, v_norm_ffn_pre, v_norm_ffn_post, v_sc_w_in, v_sc_conv_w, v_sc_w_out, v_sg_w_in, v_sg_ln_g, v_sg_ln_b, v_sg_w_s, v_sg_b_s, v_sg_w_out, v_sb_w_qkv, v_sb_w_out, v_ffn_w_up, v_ffn_conv_w, v_ffn_conv_b, v_ffn_w_down):
    weights = dict(norm_mix_pre=norm_mix_pre, norm_mix_post=norm_mix_post, norm_ffn_pre=norm_ffn_pre,
                   norm_ffn_post=norm_ffn_post, sc_w_in=sc_w_in, sc_conv_w=sc_conv_w, sc_w_out=sc_w_out,
                   sg_w_in=sg_w_in, sg_ln_g=sg_ln_g, sg_ln_b=sg_ln_b, sg_w_s=sg_w_s, sg_b_s=sg_b_s,
                   sg_w_out=sg_w_out, sb_w_qkv=sb_w_qkv, sb_w_out=sb_w_out, ffn_w_up=ffn_w_up,
                   ffn_conv_w=ffn_conv_w, ffn_conv_b=ffn_conv_b, ffn_w_down=ffn_w_down)
    mom1 = dict(norm_mix_pre=m_norm_mix_pre, norm_mix_post=m_norm_mix_post, norm_ffn_pre=m_norm_ffn_pre,
                norm_ffn_post=m_norm_ffn_post, sc_w_in=m_sc_w_in, sc_conv_w=m_sc_conv_w, sc_w_out=m_sc_w_out,
                sg_w_in=m_sg_w_in, sg_ln_g=m_sg_ln_g, sg_ln_b=m_sg_ln_b, sg_w_s=m_sg_w_s, sg_b_s=m_sg_b_s,
                sg_w_out=m_sg_w_out, sb_w_qkv=m_sb_w_qkv, sb_w_out=m_sb_w_out, ffn_w_up=m_ffn_w_up,
                ffn_conv_w=m_ffn_conv_w, ffn_conv_b=m_ffn_conv_b, ffn_w_down=m_ffn_w_down)
    mom2 = dict(norm_mix_pre=v_norm_mix_pre, norm_mix_post=v_norm_mix_post, norm_ffn_pre=v_norm_ffn_pre,
                norm_ffn_post=v_norm_ffn_post, sc_w_in=v_sc_w_in, sc_conv_w=v_sc_conv_w, sc_w_out=v_sc_w_out,
                sg_w_in=v_sg_w_in, sg_ln_g=v_sg_ln_g, sg_ln_b=v_sg_ln_b, sg_w_s=v_sg_w_s, sg_b_s=v_sg_b_s,
                sg_w_out=v_sg_w_out, sb_w_qkv=v_sb_w_qkv, sb_w_out=v_sb_w_out, ffn_w_up=v_ffn_w_up,
                ffn_conv_w=v_ffn_conv_w, ffn_conv_b=v_ffn_conv_b, ffn_w_down=v_ffn_w_down)
    order = list(weights)

    mx, my, mc = lax.axis_index("x"), lax.axis_index("y"), lax.axis_index("c")
    dev = 4 * mx + 2 * my + mc
    core_arr = jnp.reshape(mc, (1,)).astype(jnp.int32)
    chip_arr = jnp.reshape(2 * mx + my, (1,)).astype(jnp.int32)

    h = x[0]
    target = loss_target[0]
    depth = norm_mix_pre.shape[0]
    d_model = h.shape[1]
    f_loc = ffn_w_down.shape[1]
    d_ff = f_loc * N_DEV

    def gather_rows8(a2d, name):
        rows = a2d.shape[0]
        pad = (-rows) % 8
        return _all_gather(jnp.pad(a2d, ((0, pad), (0, 0))), 1, name)[:rows]

    n_a = sc_w_in.shape[0]
    sc_conv_full = gather_rows8(sc_conv_w.reshape(n_a * 3, -1), "ag_sc_conv").reshape(n_a, 3, d_model)
    ffn_conv_full = gather_rows8(ffn_conv_w.reshape(depth * 3, -1), "ag_ffn_conv").reshape(depth, 3, 2 * d_ff)

    big = {}

    kinds = [i % 3 for i in range(depth)]
    mixer_names = [("sc_w_in", "sc_w_out"), ("sg_w_in", "sg_w_out"), ("sb_w_qkv", "sb_w_out")]
    mixer_split = [3, 2, 3]
    mixer_proj = ["sc_in_proj", "sg_in_proj", "sb_qkv_proj"]

    def layer_keys(i):
        wi, wo = mixer_names[kinds[i]]
        return (wi, i // 3, 1), (wo, i // 3, 0), ("ffn_w_up", i, 1), ("ffn_w_down", i, 0)

    def gather_job(key):
        name, j, axis = key
        return _GatherJob(weights[name][j].astype(BF16), axis)

    def hosted_mm(a, key, out_dtype, nsplit, name, gather_keys):
        keys = [k for k in gather_keys if k is not None]
        res = _mm_nn(a, big[key[:2]], out_dtype, nsplit, name, jobs=[gather_job(k) for k in keys])
        if not keys:
            return res
        for k, full in zip(keys, res[1:]):
            big[k[:2]] = full
        return res[0]

    sg_ws_t = jnp.swapaxes(sg_w_s, -1, -2)
    sg_bs3 = sg_b_s[..., None]

    k_in0 = layer_keys(0)[0]
    big[k_in0[:2]] = _all_gather(weights[k_in0[0]][k_in0[1]].astype(BF16), k_in0[2], f"ag_{k_in0[0]}")
    saved = []
    for i in range(depth):
        j = i // 3
        k_in, k_out, k_up, k_down = layer_keys(i)
        nxt = layer_keys(i + 1) if i + 1 < depth else (None, None, None, None)
        sv = {"h_in": h}
        hn = _rms_fwd(h, norm_mix_pre[i:i + 1], "rms_fwd")
        sv["hn"] = hn
        p = hosted_mm(hn, k_in, BF16, mixer_split[kinds[i]], mixer_proj[kinds[i]],
                      [k_out, k_up] if i == 0 else [])
        if kinds[i] == 0:
            y = _sc_gate_fwd(p, sc_conv_full[j], "sc_gate_fwd")
        elif kinds[i] == 1:
            y = _sgu_fwd(p, sg_ln_g[j:j + 1], sg_ln_b[j:j + 1], sg_w_s[j], sg_bs3[j], "sgu_fwd")
        else:
            y = _sb_fwd(p, "sb_fwd")
        m = hosted_mm(y, k_out, F32, 1, "mix_out_proj", [])
        sv.update(p=p, y=y, m=m)
        h = _resid_rms_fwd(h, m, norm_mix_post[i:i + 1], "resid_rms_fwd")
        sv["h_mid"] = h
        hn2 = _rms_fwd(h, norm_ffn_pre[i:i + 1], "rms_fwd")
        u = hosted_mm(hn2, k_up, BF16, 2, "ffn_up_proj", [k_down, nxt[0], nxt[1]])
        cw = ffn_conv_full[i].reshape(3, 2, d_ff).transpose(1, 0, 2)
        cb = ffn_conv_b[i].reshape(2, 1, d_ff)
        a = _ffn_act_fwd(u, cw, cb, "ffn_act_fwd")
        f = hosted_mm(a, k_down, F32, 1, "ffn_down_proj", [nxt[2]])
        sv.update(hn2=hn2, u=u, a=a, f=f, cw=cw, cb=cb)
        h = _resid_rms_fwd(h, f, norm_ffn_post[i:i + 1], "resid_rms_fwd")
        saved.append(sv)

    dh, loss_part = _loss_head(h, target, "loss_head")

    grads_big = {}
    small = {}

    parts = {}
    queue = []

    def reduce_scatter(name, j, g_full, axis):
        recv1 = _rs_core_exchange(g_full, axis, f"rs1_{name}")
        parts[(name, j)] = _rs_core_add(g_full, recv1, core_arr, axis, f"rs_add_{name}")
        queue.append((name, j))

    def hosting(mm, a, b, out_dtype, name, n_jobs):
        keys = [queue.pop(0) for _ in range(min(n_jobs, len(queue)))]
        res = mm(a, b, out_dtype, name, jobs=[_ChipExchangeJob(parts[k]) for k in keys])
        if not keys:
            return res
        for k, recv in zip(keys, res[1:]):
            grads_big[k] = (parts[k], recv)
        return res[0]

    for i in reversed(range(depth)):
        j = i // 3
        sv = saved[i]
        df, dg = _rms_bwd(sv["f"], norm_ffn_post[i:i + 1], dh, None, BF16, "rms_bwd_post")
        small.setdefault("norm_ffn_post", {})[i] = dg
        da = hosting(_mm_nt, df, big[("ffn_w_down", i)], BF16, "ffn_down_dx", 1)
        reduce_scatter("ffn_w_down", i, hosting(_mm_tn, sv["a"], df, BF16, "ffn_down_dw", 2), 0)
        du, dcw, dcb = _ffn_act_bwd(sv["u"], da, sv["cw"], sv["cb"], "ffn_act_bwd")
        small.setdefault("ffn_conv_w", {})[i] = dcw.transpose(1, 0, 2).reshape(3, 2 * d_ff)
        small.setdefault("ffn_conv_b", {})[i] = dcb.reshape(2 * d_ff)
        dhn2 = _mm_nt(du, big[("ffn_w_up", i)], F32, "ffn_up_dx")
        reduce_scatter("ffn_w_up", i, _mm_tn(sv["hn2"], du, BF16, "ffn_up_dw"), 1)
        dh, dg = _rms_bwd(sv["h_mid"], norm_ffn_pre[i:i + 1], dhn2, dh, F32, "rms_bwd_pre")
        small.setdefault("norm_ffn_pre", {})[i] = dg
        dm, dg = _rms_bwd(sv["m"], norm_mix_post[i:i + 1], dh, None, BF16, "rms_bwd_post")
        small.setdefault("norm_mix_post", {})[i] = dg
        if kinds[i] == 0:
            wo, wi = "sc_w_out", "sc_w_in"
        elif kinds[i] == 1:
            wo, wi = "sg_w_out", "sg_w_in"
        else:
            wo, wi = "sb_w_out", "sb_w_qkv"
        dy = hosting(_mm_nt, dm, big[(wo, j)], BF16, "mix_out_dx", 1)
        reduce_scatter(wo, j, _mm_tn(sv["y"], dm, BF16, "mix_out_dw"), 0)
        if kinds[i] == 0:
            dp, dcw = _sc_gate_bwd(sv["p"], dy, sc_conv_full[j], "sc_gate_bwd")
            small.setdefault("sc_conv_w", {})[j] = dcw
        elif kinds[i] == 1:
            dp, dlg, dlb, dws, dbs = _sgu_bwd(sv["p"], dy, sg_ln_g[j:j + 1], sg_ln_b[j:j + 1], sg_w_s[j], sg_ws_t[j],
                                              sg_bs3[j], "sgu_bwd")
            small.setdefault("sg_ln_g", {})[j] = dlg[0]
            small.setdefault("sg_ln_b", {})[j] = dlb[0]
            small.setdefault("sg_w_s", {})[j] = dws
            small.setdefault("sg_b_s", {})[j] = dbs[..., 0]
        else:
            dq, dkv = _sb_bwd(sv["p"], dy, "sb_bwd")
            dp = jnp.concatenate([dq[None], dkv], axis=0)
        dhn = _mm_nt(dp, big[(wi, j)], F32, "mix_in_dx")
        reduce_scatter(wi, j, _mm_tn(sv["hn"], dp, BF16, "mix_in_dw"), 1)
        dh, dg = _rms_bwd(sv["h_in"], norm_mix_pre[i:i + 1], dhn, dh, F32, "rms_bwd_pre")
        small.setdefault("norm_mix_pre", {})[i] = dg

    grad_x = dh[None]
    for k, recv in zip(queue, _rs_chip_exchange([parts[k] for k in queue], "rs2_tail")):
        grads_big[k] = (parts[k], recv)

    small_names = ["norm_mix_pre", "norm_mix_post", "norm_ffn_pre", "norm_ffn_post", "sg_ln_g", "sg_ln_b", "sg_w_s",
                   "sg_b_s", "ffn_conv_b", "sc_conv_w", "ffn_conv_w"]
    full_shapes = {n: weights[n].shape for n in small_names}
    full_shapes["sc_conv_w"] = (n_a, 3, d_model)
    full_shapes["ffn_conv_w"] = (depth, 3, 2 * d_ff)
    partial = [jnp.stack([small[n][k].reshape(full_shapes[n][1:]) for k in sorted(small[n])]) for n in small_names]
    packed = _pack(partial + [loss_part])
    gathered = _all_gather(packed, 0, "ag_small_grads")
    summed = _sum_devices(gathered, "sum_small_grads")
    pieces = _unpack(summed, [full_shapes[n] for n in small_names] + [(1, 1)])
    loss = pieces[-1].reshape(())
    small_grads = dict(zip(small_names, pieces[:-1]))
    for n in ("sc_conv_w", "ffn_conv_w"):
        c_loc = weights[n].shape[-1]
        small_grads[n] = lax.dynamic_slice_in_dim(small_grads[n], dev * c_loc, c_loc, axis=2)
    shapes = [weights[n].shape for n in small_names]
    d_p, m_p, v_p = _adamw_packed(_pack([weights[n] for n in small_names]), _pack([mom1[n] for n in small_names]),
                                  _pack([mom2[n] for n in small_names]), _pack([small_grads[n] for n in small_names]),
                                  "adamw_small")
    out_g, out_d, out_m, out_v = dict(small_grads), {}, {}, {}
    for n, d_, m_, v_ in zip(small_names, _unpack(d_p, shapes), _unpack(m_p, shapes), _unpack(v_p, shapes)):
        out_d[n], out_m[n], out_v[n] = d_, m_, v_

    for n in order:
        if n in small_names:
            continue
        res = [_adamw_shard(weights[n][j], mom1[n][j], mom2[n][j], *grads_big[(n, j)], chip_arr, f"adamw_{n}")
               for j in range(weights[n].shape[0])]
        out_g[n], out_d[n], out_m[n], out_v[n] = (jnp.stack([r_[t] for r_ in res]) for t in range(4))

    return (loss, grad_x, *[out_g[n] for n in order], *[out_d[n] for n in order],
            *[out_m[n] for n in order], *[out_v[n] for n in order])
```

```python
import functools

import jax
import jax.numpy as jnp
from jax import lax
from jax.experimental import pallas as pl
from jax.experimental.pallas import tpu as pltpu

F32 = jnp.float32
BF16 = jnp.bfloat16

EPS = 1e-6
CHUNK = 128
HEAD_DIM = 128
SG_GROUPS = 8
HALO = 16
N_DEV = 8
LANES = 128
V7X_VMEM_LIMIT = 56 * 1024 * 1024

ADAM_LR = 0.001
ADAM_B1 = 0.9
ADAM_B2 = 0.999
ADAM_EPS = 1e-08
ADAM_WD = 0.01
ADAM_STEP = 10

MESH = pl.DeviceIdType.MESH
ANY = pl.BlockSpec(memory_space=pl.ANY)


def _tile(n, prefs):
    for p in prefs:
        if p <= n and n % p == 0:
            return p
    return n


def _params(*sem):
    return pltpu.CompilerParams(dimension_semantics=sem, vmem_limit_bytes=V7X_VMEM_LIMIT)


def _shard_of(ref, dev, axis, r, c):
    if axis == 0:
        return ref.at[pl.ds(dev * r, r), :]
    return ref.at[:, pl.ds(dev * c, c)]


class _GatherJob:
    def __init__(self, x, axis):
        self.axis = axis
        self.r, self.c = x.shape
        full = (N_DEV * self.r, self.c) if axis == 0 else (self.r, N_DEV * self.c)
        self.inputs = [x]
        self.out_shapes = [jax.ShapeDtypeStruct(full, x.dtype)]
        self.scratch = [pltpu.SemaphoreType.DMA((7,)), pltpu.SemaphoreType.DMA((7,)), pltpu.SemaphoreType.DMA(())]

    def _plan(self, ins, outs, scr):
        (x_ref,), (out_ref,), (send_sems, recv_sems, local_sem) = ins, outs, scr
        mx, my, mc = lax.axis_index("x"), lax.axis_index("y"), lax.axis_index("c")
        me, sibling = (mx, my, mc), (mx, my, 1 - mc)
        chips = [(1 - mx, my), (mx, 1 - my), (1 - mx, 1 - my)]

        def rows(px, py, pc):
            return _shard_of(out_ref, 4 * px + 2 * py + pc, self.axis, self.r, self.c)

        def copy(k, block, to, src=None):
            return pltpu.make_async_remote_copy(
                src_ref=rows(*block) if src is None else src, dst_ref=rows(*block),
                send_sem=send_sems.at[k], recv_sem=recv_sems.at[k], device_id=to, device_id_type=MESH)

        mine = pltpu.make_async_copy(x_ref, rows(*me), local_sem)
        first = [copy(0, me, sibling, src=x_ref)] + [copy(1 + j, me, (*chip, mc), src=x_ref)
                                                      for j, chip in enumerate(chips)]
        passed = [copy(4 + j, (*chip, mc), sibling) for j, chip in enumerate(chips)]
        landed = [copy(1 + j, (*chip, mc), me) for j, chip in enumerate(chips)]
        from_sibling = [copy(0, sibling, me)] + [copy(4 + j, (*chip, 1 - mc), me) for j, chip in enumerate(chips)]
        return mine, first, passed, landed, from_sibling

    def start(self, ins, outs, scr):
        mine, first, _, _, _ = self._plan(ins, outs, scr)
        mine.start()
        for cp in first:
            cp.start()

    def finish(self, ins, outs, scr):
        mine, first, passed, landed, from_sibling = self._plan(ins, outs, scr)
        for j in range(3):
            landed[j].wait_recv()
            passed[j].start()
        for cp in from_sibling:
            cp.wait_recv()
        for cp in first + passed:
            cp.wait_send()
        mine.wait()


class _ChipExchangeJob:
    def __init__(self, p):
        _, r, c = p.shape
        self.inputs = [p]
        self.out_shapes = [jax.ShapeDtypeStruct((3, r, c), p.dtype)]
        self.scratch = [pltpu.SemaphoreType.DMA((3,)), pltpu.SemaphoreType.DMA((3,))]

    def _plan(self, ins, outs, scr):
        (p_ref,), (recv_ref,), (send_sems, recv_sems) = ins, outs, scr
        mx, my, mc = lax.axis_index("x"), lax.axis_index("y"), lax.axis_index("c")
        chips = [(1 - mx, my), (mx, 1 - my), (1 - mx, 1 - my)]
        return [pltpu.make_async_remote_copy(
            src_ref=p_ref.at[2 * px + py], dst_ref=recv_ref.at[j],
            send_sem=send_sems.at[j], recv_sem=recv_sems.at[j], device_id=(px, py, mc), device_id_type=MESH)
            for j, (px, py) in enumerate(chips)]

    def start(self, ins, outs, scr):
        for cp in self._plan(ins, outs, scr):
            cp.start()

    def finish(self, ins, outs, scr):
        for cp in self._plan(ins, outs, scr):
            cp.wait()


class _Hosted:
    def __init__(self, jobs):
        self.jobs = list(jobs)
        self.inputs = [x for j in self.jobs for x in j.inputs]
        self.out_shapes = [s for j in self.jobs for s in j.out_shapes]
        self.scratch = [s for j in self.jobs for s in j.scratch]
        self.in_specs = [ANY] * len(self.inputs)
        self.out_specs = [ANY] * len(self.out_shapes)

    def split(self, refs, n_in, n_out, n_scratch):
        refs = list(refs)
        ji, jo, js = len(self.inputs), len(self.out_shapes), len(self.scratch)
        ins, refs = refs[:n_in], refs[n_in:]
        jins, refs = refs[:ji], refs[ji:]
        outs, refs = refs[:n_out], refs[n_out:]
        jouts, refs = refs[:jo], refs[jo:]
        scr, jscr = refs[:n_scratch], refs[n_scratch:]
        assert len(jscr) == js
        per_job = []
        for j in self.jobs:
            a, b, c = len(j.inputs), len(j.out_shapes), len(j.scratch)
            per_job.append((jins[:a], jouts[:b], jscr[:c]))
            jins, jouts, jscr = jins[a:], jouts[b:], jscr[c:]
        return ins, outs, scr, per_job

    def start(self, per_job, when):
        if self.jobs:
            @pl.when(when)
            def _():
                for j, refs in zip(self.jobs, per_job):
                    j.start(*refs)

    def finish(self, per_job, when):
        if self.jobs:
            @pl.when(when)
            def _():
                for j, refs in zip(self.jobs, per_job):
                    j.finish(*refs)


def _grid_edges(grid):
    first = last = None
    for ax, n in enumerate(grid):
        f, l = pl.program_id(ax) == 0, pl.program_id(ax) == n - 1
        first = f if first is None else jnp.logical_and(first, f)
        last = l if last is None else jnp.logical_and(last, l)
    return first, last


def _run_jobs(jobs, name):
    hosted = _Hosted(jobs)

    def body(*refs):
        _, _, _, per_job = hosted.split(refs, 0, 0, 0)
        for j, r in zip(hosted.jobs, per_job):
            j.start(*r)
        for j, r in zip(hosted.jobs, per_job):
            j.finish(*r)

    return pl.pallas_call(
        body, name=name, out_shape=tuple(hosted.out_shapes), in_specs=hosted.in_specs,
        out_specs=tuple(hosted.out_specs), scratch_shapes=hosted.scratch,
    )(*hosted.inputs)


def _mm_nn(a, b, out_dtype, nsplit, name, jobs=()):
    m, k = a.shape
    n = b.shape[1]
    w = n // nsplit
    tm = _tile(m, (1024, 512, 256, 128)) if k <= 2048 else _tile(m, (512, 256, 128))
    tn = _tile(w, (512, 256, 128))
    per = w // tn
    grid = (m // tm, n // tn)
    hosted = _Hosted(jobs)

    def body(*refs):
        (a_ref, b_ref), (o_ref,), _, per_job = hosted.split(refs, 2, 1, 0)
        first, last = _grid_edges(grid)
        hosted.start(per_job, first)
        o_ref[...] = jnp.dot(a_ref[...], b_ref[...], preferred_element_type=F32).astype(o_ref.dtype)
        hosted.finish(per_job, last)

    if nsplit == 1:
        out_shape = jax.ShapeDtypeStruct((m, n), out_dtype)
        out_spec = pl.BlockSpec((tm, tn), lambda i, j: (i, j))
    else:
        out_shape = jax.ShapeDtypeStruct((nsplit, m, w), out_dtype)
        out_spec = pl.BlockSpec((None, tm, tn), lambda i, j: (j // per, i, j % per))
    res = pl.pallas_call(
        body, grid=grid, name=name,
        in_specs=[pl.BlockSpec((tm, k), lambda i, j: (i, 0)), pl.BlockSpec((k, tn), lambda i, j: (0, j))]
        + hosted.in_specs,
        out_specs=tuple([out_spec] + hosted.out_specs), out_shape=tuple([out_shape] + hosted.out_shapes),
        scratch_shapes=hosted.scratch,
        compiler_params=_params("arbitrary", "arbitrary"),
    )(a, b, *hosted.inputs)
    return res if jobs else res[0]


def _mm_nt(a, b, out_dtype, name, jobs=()):
    a3 = a if a.ndim == 3 else a[None]
    ns, m, w = a3.shape
    ko, n = b.shape
    assert n == ns * w
    tm = _tile(m, (1024, 512, 256, 128)) if n <= 2048 else _tile(m, (512, 256, 128))
    tko = _tile(ko, (256, 128)) if n > 4096 else _tile(ko, (512, 256, 128))
    grid = (m // tm, ko // tko)
    hosted = _Hosted(jobs)

    def body(*refs):
        (a_ref, b_ref), (o_ref,), _, per_job = hosted.split(refs, 2, 1, 0)
        first, last = _grid_edges(grid)
        hosted.start(per_job, first)
        acc = None
        for s in range(ns):
            p = lax.dot_general(a_ref[s], b_ref[:, s * w:(s + 1) * w], (((1,), (1,)), ((), ())),
                                preferred_element_type=F32)
            acc = p if acc is None else acc + p
        o_ref[...] = acc.astype(o_ref.dtype)
        hosted.finish(per_job, last)

    res = pl.pallas_call(
        body, grid=grid, name=name,
        in_specs=[pl.BlockSpec((ns, tm, w), lambda i, j: (0, i, 0)), pl.BlockSpec((tko, n), lambda i, j: (j, 0))]
        + hosted.in_specs,
        out_specs=tuple([pl.BlockSpec((tm, tko), lambda i, j: (i, j))] + hosted.out_specs),
        out_shape=tuple([jax.ShapeDtypeStruct((m, ko), out_dtype)] + hosted.out_shapes),
        scratch_shapes=hosted.scratch,
        compiler_params=_params("arbitrary", "arbitrary"),
    )(a3, b, *hosted.inputs)
    return res if jobs else res[0]


def _mm_tn(a, b, out_dtype, name, jobs=()):
    b3 = b if b.ndim == 3 else b[None]
    ns, m, w = b3.shape
    k = a.shape[1]
    n = ns * w
    tk = _tile(k, (512, 256, 128))
    tn = _tile(w, (2816, 2048, 1024, 512, 256, 128))
    ts = _tile(m, (1024, 512, 256, 128))
    per = w // tn
    n_s = m // ts
    grid = (k // tk, n // tn, n_s)
    hosted = _Hosted(jobs)

    def body(*refs):
        (a_ref, b_ref), (o_ref,), (acc_ref,), per_job = hosted.split(refs, 2, 1, 1)
        s = pl.program_id(2)
        first, last = _grid_edges(grid)
        hosted.start(per_job, first)

        @pl.when(s == 0)
        def _():
            acc_ref[...] = jnp.zeros_like(acc_ref)

        acc_ref[...] += lax.dot_general(a_ref[...], b_ref[...], (((0,), (0,)), ((), ())),
                                        preferred_element_type=F32)

        @pl.when(s == n_s - 1)
        def _():
            o_ref[...] = acc_ref[...].astype(o_ref.dtype)

        hosted.finish(per_job, last)

    res = pl.pallas_call(
        body, grid=grid, name=name,
        in_specs=[pl.BlockSpec((ts, tk), lambda i, j, s: (s, i)),
                  pl.BlockSpec((None, ts, tn), lambda i, j, s: (j // per, s, j % per))] + hosted.in_specs,
        out_specs=tuple([pl.BlockSpec((tk, tn), lambda i, j, s: (i, j))] + hosted.out_specs),
        out_shape=tuple([jax.ShapeDtypeStruct((k, n), out_dtype)] + hosted.out_shapes),
        scratch_shapes=[pltpu.VMEM((tk, tn), F32)] + hosted.scratch,
        compiler_params=_params("arbitrary", "arbitrary", "arbitrary"),
    )(a, b3, *hosted.inputs)
    return res if jobs else res[0]


def _rms_rows(s):
    return _tile(s, (256, 128))


def _rms_fwd(x, g, name):
    s, d = x.shape
    ts = _rms_rows(s)

    def body(x_ref, g_ref, o_ref):
        xf = x_ref[...]
        r = lax.rsqrt(jnp.mean(xf * xf, axis=-1, keepdims=True) + EPS)
        o_ref[...] = ((xf * r) * g_ref[...]).astype(o_ref.dtype)

    return pl.pallas_call(
        body, grid=(s // ts,), name=name,
        in_specs=[pl.BlockSpec((ts, d), lambda i: (i, 0)), pl.BlockSpec((1, d), lambda i: (0, 0))],
        out_specs=pl.BlockSpec((ts, d), lambda i: (i, 0)),
        out_shape=jax.ShapeDtypeStruct((s, d), BF16),
        compiler_params=_params("parallel"),
    )(x, g)


def _resid_rms_fwd(h, m, g, name):
    s, d = h.shape
    ts = _rms_rows(s)

    def body(h_ref, m_ref, g_ref, o_ref):
        mf = m_ref[...]
        r = lax.rsqrt(jnp.mean(mf * mf, axis=-1, keepdims=True) + EPS)
        o_ref[...] = h_ref[...] + (mf * r) * g_ref[...]

    return pl.pallas_call(
        body, grid=(s // ts,), name=name,
        in_specs=[pl.BlockSpec((ts, d), lambda i: (i, 0)), pl.BlockSpec((ts, d), lambda i: (i, 0)),
                  pl.BlockSpec((1, d), lambda i: (0, 0))],
        out_specs=pl.BlockSpec((ts, d), lambda i: (i, 0)),
        out_shape=jax.ShapeDtypeStruct((s, d), F32),
        compiler_params=_params("parallel"),
    )(h, m, g)


def _rms_bwd(x, g, dy, dres, out_dtype, name):
    s, d = x.shape
    ts = _rms_rows(s)
    has_res = dres is not None

    def body(*refs):
        if has_res:
            x_ref, g_ref, dy_ref, dres_ref, dx_ref, dg_ref = refs
        else:
            x_ref, g_ref, dy_ref, dx_ref, dg_ref = refs
        i = pl.program_id(0)
        xf = x_ref[...]
        dyf = dy_ref[...].astype(F32)
        r = lax.rsqrt(jnp.mean(xf * xf, axis=-1, keepdims=True) + EPS)
        xh = xf * r
        dxh = dyf * g_ref[...]
        dx = r * (dxh - xh * jnp.mean(dxh * xh, axis=-1, keepdims=True))
        if has_res:
            dx = dx + dres_ref[...]
        dx_ref[...] = dx.astype(dx_ref.dtype)

        @pl.when(i == 0)
        def _():
            dg_ref[...] = jnp.zeros_like(dg_ref)

        dg_ref[...] += jnp.sum(dyf * xh, axis=0, keepdims=True)

    row = pl.BlockSpec((ts, d), lambda i: (i, 0))
    vec = pl.BlockSpec((1, d), lambda i: (0, 0))
    ins = [x, g, dy] + ([dres] if has_res else [])
    return pl.pallas_call(
        body, grid=(s // ts,), name=name,
        in_specs=[row, vec, row] + ([row] if has_res else []),
        out_specs=(row, vec),
        out_shape=(jax.ShapeDtypeStruct((s, d), out_dtype), jax.ShapeDtypeStruct((1, d), F32)),
        compiler_params=_params("arbitrary"),
    )(*ins)


def _loss_head(h, target, name):
    s, d = h.shape
    ts = _rms_rows(s)
    n_i = s // ts

    def body(h_ref, t_ref, dh_ref, loss_ref, acc_ref):
        i = pl.program_id(0)
        err = h_ref[...] - t_ref[...]
        dh_ref[...] = err * (1.0 / d)

        @pl.when(i == 0)
        def _():
            acc_ref[...] = jnp.zeros_like(acc_ref)

        acc_ref[...] += jnp.sum(err * err, axis=0, keepdims=True)

        @pl.when(i == n_i - 1)
        def _():
            loss_ref[...] = jnp.sum(acc_ref[...], axis=1, keepdims=True) * (0.5 / d)

    row = pl.BlockSpec((ts, d), lambda i: (i, 0))
    return pl.pallas_call(
        body, grid=(n_i,), name=name,
        in_specs=[row, row],
        out_specs=(row, pl.BlockSpec((1, 1), lambda i: (0, 0))),
        out_shape=(jax.ShapeDtypeStruct((s, d), F32), jax.ShapeDtypeStruct((1, 1), F32)),
        scratch_shapes=[pltpu.VMEM((1, d), F32)],
        compiler_params=_params("arbitrary"),
    )(h, target)


def _halo_specs(ns, s, ts, tc):
    hb = ts // HALO
    last = s // HALO - 1
    cur = pl.BlockSpec((ns, ts, tc), lambda j, i: (0, i, j))
    prev = pl.BlockSpec((ns, HALO, tc), lambda j, i: (0, jnp.maximum(i * hb - 1, 0), j))
    nxt = pl.BlockSpec((ns, HALO, tc), lambda j, i: (0, jnp.minimum((i + 1) * hb, last), j))
    return cur, prev, nxt


def _ext(prev, cur, nxt, first, last):
    p = jnp.where(first, 0.0, prev.astype(F32))
    n = jnp.where(last, 0.0, nxt.astype(F32))
    return jnp.concatenate([p, cur.astype(F32), n], axis=0)


def _dn(xe, k):
    return xe if k == 0 else pltpu.roll(xe, k, 0)


def _up(xe, k):
    return xe if k == 0 else pltpu.roll(xe, xe.shape[0] - k, 0)


def _conv_e(xe, w):
    return w[0:1, :] * _dn(xe, 2) + w[1:2, :] * _dn(xe, 1) + w[2:3, :] * xe


def _conv_t_e(de, w):
    return w[2:3, :] * de + w[1:2, :] * _up(de, 1) + w[0:1, :] * _up(de, 2)


def _conv_dw(de, xe, ts):
    return [jnp.sum((de * _dn(xe, 2 - k))[HALO:HALO + ts], axis=0, keepdims=True) for k in range(3)]


def _conv_tiles(s, c):
    return _tile(s, (512, 256, 128)), _tile(c, (512, 256, 128))


def _sigmoid(x):
    return pl.reciprocal(1.0 + jnp.exp(-x), approx=True)


def _sc_gate_fwd(p, w, name):
    _, s, c = p.shape
    ts, tc = _conv_tiles(s, c)
    n_i = s // ts

    def body(p_ref, pp_ref, w_ref, y_ref):
        i = pl.program_id(1)
        ze = _ext(pp_ref[1], p_ref[1], pp_ref[1], i == 0, True) * _ext(pp_ref[2], p_ref[2], pp_ref[2], i == 0, True)
        cz = _conv_e(ze, w_ref[...])[HALO:HALO + ts]
        y_ref[...] = (p_ref[0].astype(F32) * cz).astype(y_ref.dtype)

    cur, prev, _ = _halo_specs(3, s, ts, tc)
    return pl.pallas_call(
        body, grid=(c // tc, n_i), name=name,
        in_specs=[cur, prev, pl.BlockSpec((3, tc), lambda j, i: (0, j))],
        out_specs=pl.BlockSpec((ts, tc), lambda j, i: (i, j)),
        out_shape=jax.ShapeDtypeStruct((s, c), BF16),
        compiler_params=_params("parallel", "arbitrary"),
    )(p, p, w)


def _sc_gate_bwd(p, dy, w, name):
    _, s, c = p.shape
    ts, tc = _conv_tiles(s, c)
    n_i = s // ts

    def body(p_ref, pp_ref, pn_ref, dy_ref, dyn_ref, w_ref, dp_ref, dw_ref):
        i = pl.program_id(1)
        first, last = i == 0, i == n_i - 1
        wv = w_ref[...]
        gbe = _ext(pp_ref[0], p_ref[0], pn_ref[0], first, last)
        gce = _ext(pp_ref[1], p_ref[1], pn_ref[1], first, last)
        hhe = _ext(pp_ref[2], p_ref[2], pn_ref[2], first, last)
        dye = _ext(dyn_ref[...], dy_ref[...], dyn_ref[...], True, last)
        ze = gce * hhe
        cze = _conv_e(ze, wv)
        dcze = dye * gbe
        dze = _conv_t_e(dcze, wv)
        rows = slice(HALO, HALO + ts)
        dp_ref[0] = (dye * cze)[rows].astype(dp_ref.dtype)
        dp_ref[1] = (dze * hhe)[rows].astype(dp_ref.dtype)
        dp_ref[2] = (dze * gce)[rows].astype(dp_ref.dtype)

        @pl.when(first)
        def _():
            dw_ref[...] = jnp.zeros_like(dw_ref)

        dws = _conv_dw(dcze, ze, ts)
        for k in range(3):
            dw_ref[k:k + 1, :] += dws[k]

    cur, prev, nxt = _halo_specs(3, s, ts, tc)
    hb = ts // HALO
    last_h = s // HALO - 1
    return pl.pallas_call(
        body, grid=(c // tc, n_i), name=name,
        in_specs=[cur, prev, nxt,
                  pl.BlockSpec((ts, tc), lambda j, i: (i, j)),
                  pl.BlockSpec((HALO, tc), lambda j, i: (jnp.minimum((i + 1) * hb, last_h), j)),
                  pl.BlockSpec((3, tc), lambda j, i: (0, j))],
        out_specs=(cur, pl.BlockSpec((3, tc), lambda j, i: (0, j))),
        out_shape=(jax.ShapeDtypeStruct((3, s, c), BF16), jax.ShapeDtypeStruct((3, c), F32)),
        compiler_params=_params("parallel", "arbitrary"),
    )(p, p, p, dy, dy, w)


def _ffn_act_fwd(u, w, b, name):
    _, s, f = u.shape
    ts, tc = _conv_tiles(s, f)
    n_i = s // ts

    def body(u_ref, up_ref, w_ref, b_ref, a_ref):
        i = pl.program_id(1)
        rows = slice(HALO, HALO + ts)
        cg = _conv_e(_ext(up_ref[0], u_ref[0], up_ref[0], i == 0, True), w_ref[0])[rows] + b_ref[0]
        cv = _conv_e(_ext(up_ref[1], u_ref[1], up_ref[1], i == 0, True), w_ref[1])[rows] + b_ref[1]
        a_ref[...] = (cg * _sigmoid(cg) * cv).astype(a_ref.dtype)

    cur, prev, _ = _halo_specs(2, s, ts, tc)
    return pl.pallas_call(
        body, grid=(f // tc, n_i), name=name,
        in_specs=[cur, prev, pl.BlockSpec((2, 3, tc), lambda j, i: (0, 0, j)),
                  pl.BlockSpec((2, 1, tc), lambda j, i: (0, 0, j))],
        out_specs=pl.BlockSpec((ts, tc), lambda j, i: (i, j)),
        out_shape=jax.ShapeDtypeStruct((s, f), BF16),
        compiler_params=_params("parallel", "arbitrary"),
    )(u, u, w, b)


def _ffn_act_bwd(u, da, w, b, name):
    _, s, f = u.shape
    ts, tc = _conv_tiles(s, f)
    n_i = s // ts

    def body(u_ref, up_ref, un_ref, da_ref, dan_ref, w_ref, b_ref, du_ref, dw_ref, db_ref):
        i = pl.program_id(1)
        first, last = i == 0, i == n_i - 1
        rows = slice(HALO, HALO + ts)
        uge = _ext(up_ref[0], u_ref[0], un_ref[0], first, last)
        uve = _ext(up_ref[1], u_ref[1], un_ref[1], first, last)
        dae = _ext(dan_ref[...], da_ref[...], dan_ref[...], True, last)
        cg = _conv_e(uge, w_ref[0]) + b_ref[0]
        cv = _conv_e(uve, w_ref[1]) + b_ref[1]
        sg = _sigmoid(cg)
        dcg = dae * cv * (sg * (1.0 + cg * (1.0 - sg)))
        dcv = dae * (cg * sg)
        du_ref[0] = _conv_t_e(dcg, w_ref[0])[rows].astype(du_ref.dtype)
        du_ref[1] = _conv_t_e(dcv, w_ref[1])[rows].astype(du_ref.dtype)

        @pl.when(first)
        def _():
            dw_ref[...] = jnp.zeros_like(dw_ref)
            db_ref[...] = jnp.zeros_like(db_ref)

        for h, (de, xe) in enumerate(((dcg, uge), (dcv, uve))):
            dws = _conv_dw(de, xe, ts)
            for k in range(3):
                dw_ref[h, k:k + 1, :] += dws[k]
            db_ref[h] += jnp.sum(de[rows], axis=0, keepdims=True)

    cur, prev, nxt = _halo_specs(2, s, ts, tc)
    hb = ts // HALO
    last_h = s // HALO - 1
    wspec = pl.BlockSpec((2, 3, tc), lambda j, i: (0, 0, j))
    bspec = pl.BlockSpec((2, 1, tc), lambda j, i: (0, 0, j))
    return pl.pallas_call(
        body, grid=(f // tc, n_i), name=name,
        in_specs=[cur, prev, nxt,
                  pl.BlockSpec((ts, tc), lambda j, i: (i, j)),
                  pl.BlockSpec((HALO, tc), lambda j, i: (jnp.minimum((i + 1) * hb, last_h), j)),
                  wspec, bspec],
        out_specs=(cur, wspec, bspec),
        out_shape=(jax.ShapeDtypeStruct((2, s, f), BF16), jax.ShapeDtypeStruct((2, 3, f), F32),
                   jax.ShapeDtypeStruct((2, 1, f), F32)),
        compiler_params=_params("parallel", "arbitrary"),
    )(u, u, u, da, da, w, b)


_INV_SQRT2 = 0.7071067811865476
_INV_SQRT_2PI = 0.3989422804014327


def _gelu(x):
    return 0.5 * x * (1.0 + lax.erf(x * _INV_SQRT2))


def _gelu_grad(x):
    return 0.5 * (1.0 + lax.erf(x * _INV_SQRT2)) + x * (_INV_SQRT_2PI * jnp.exp(-0.5 * x * x))


def _tril_bf16(w):
    t = lax.broadcasted_iota(jnp.int32, w.shape, 0)
    s = lax.broadcasted_iota(jnp.int32, w.shape, 1)
    return jnp.where(s <= t, w, 0.0).astype(BF16)


def _sgu_fwd(p, ln_g, ln_b, ws, bs, name):
    _, s, c = p.shape
    g_n = ws.shape[0]
    cg = c // g_n

    def body(p_ref, lg_ref, lb_ref, ws_ref, bs_ref, y_ref):
        u = _gelu(p_ref[0].astype(F32))
        v = _gelu(p_ref[1].astype(F32))
        mu = jnp.mean(v, axis=-1, keepdims=True)
        vc = v - mu
        rstd = lax.rsqrt(jnp.mean(vc * vc, axis=-1, keepdims=True) + EPS)
        vn = ((vc * rstd) * lg_ref[...] + lb_ref[...]).astype(BF16)
        for g in range(g_n):
            cols = slice(g * cg, (g + 1) * cg)
            mixed = jnp.dot(_tril_bf16(ws_ref[g]), vn[:, cols], preferred_element_type=F32) + bs_ref[g]
            y_ref[:, cols] = (u[:, cols] * mixed).astype(y_ref.dtype)

    vec = pl.BlockSpec((1, c), lambda i: (0, 0))
    return pl.pallas_call(
        body, grid=(s // CHUNK,), name=name,
        in_specs=[pl.BlockSpec((2, CHUNK, c), lambda i: (0, i, 0)), vec, vec,
                  pl.BlockSpec((g_n, CHUNK, CHUNK), lambda i: (0, 0, 0)),
                  pl.BlockSpec((g_n, CHUNK, 1), lambda i: (0, 0, 0))],
        out_specs=pl.BlockSpec((CHUNK, c), lambda i: (i, 0)),
        out_shape=jax.ShapeDtypeStruct((s, c), BF16),
        compiler_params=_params("parallel"),
    )(p, ln_g, ln_b, ws, bs)


def _sgu_bwd(p, dy, ln_g, ln_b, ws, ws_t, bs, name):
    _, s, c = p.shape
    g_n = ws.shape[0]
    cg = c // g_n

    def body(p_ref, dy_ref, lg_ref, lb_ref, ws_ref, wst_ref, bs_ref, dp_ref, dlg_ref, dlb_ref, dws_ref, dbs_ref,
             dvn_ref):
        i = pl.program_id(0)
        pu = p_ref[0].astype(F32)
        pv = p_ref[1].astype(F32)
        dyf = dy_ref[...].astype(F32)
        u = _gelu(pu)
        v = _gelu(pv)
        mu = jnp.mean(v, axis=-1, keepdims=True)
        vc = v - mu
        rstd = lax.rsqrt(jnp.mean(vc * vc, axis=-1, keepdims=True) + EPS)
        vhat = vc * rstd
        vn = (vhat * lg_ref[...] + lb_ref[...]).astype(BF16)

        @pl.when(i == 0)
        def _():
            dlg_ref[...] = jnp.zeros_like(dlg_ref)
            dlb_ref[...] = jnp.zeros_like(dlb_ref)
            dws_ref[...] = jnp.zeros_like(dws_ref)
            dbs_ref[...] = jnp.zeros_like(dbs_ref)

        t_i = lax.broadcasted_iota(jnp.int32, (CHUNK, CHUNK), 0)
        s_i = lax.broadcasted_iota(jnp.int32, (CHUNK, CHUNK), 1)
        for g in range(g_n):
            cols = slice(g * cg, (g + 1) * cg)
            mixed = jnp.dot(_tril_bf16(ws_ref[g]), vn[:, cols], preferred_element_type=F32) + bs_ref[g]
            dp_ref[0, :, cols] = (dyf[:, cols] * mixed * _gelu_grad(pu[:, cols])).astype(dp_ref.dtype)
            dmixed = dyf[:, cols] * u[:, cols]
            dmb = dmixed.astype(BF16)
            wt = jnp.where(t_i <= s_i, wst_ref[g], 0.0).astype(BF16)
            dvn_ref[:, cols] = jnp.dot(wt, dmb, preferred_element_type=F32)
            dwg = lax.dot_general(dmb, vn[:, cols], (((1,), (1,)), ((), ())), preferred_element_type=F32)
            dws_ref[g] += jnp.where(s_i <= t_i, dwg, 0.0)
            dbs_ref[g] += jnp.sum(dmixed, axis=1, keepdims=True)
        dvn = dvn_ref[...]
        dlg_ref[...] += jnp.sum(dvn * vhat, axis=0, keepdims=True)
        dlb_ref[...] += jnp.sum(dvn, axis=0, keepdims=True)
        dvh = dvn * lg_ref[...]
        dv = rstd * (dvh - jnp.mean(dvh, axis=-1, keepdims=True) - vhat * jnp.mean(dvh * vhat, axis=-1, keepdims=True))
        dp_ref[1] = (dv * _gelu_grad(pv)).astype(dp_ref.dtype)

    vec = pl.BlockSpec((1, c), lambda i: (0, 0))
    wspec = pl.BlockSpec((g_n, CHUNK, CHUNK), lambda i: (0, 0, 0))
    bspec = pl.BlockSpec((g_n, CHUNK, 1), lambda i: (0, 0, 0))
    pspec = pl.BlockSpec((2, CHUNK, c), lambda i: (0, i, 0))
    return pl.pallas_call(
        body, grid=(s // CHUNK,), name=name,
        in_specs=[pspec, pl.BlockSpec((CHUNK, c), lambda i: (i, 0)), vec, vec, wspec, wspec, bspec],
        out_specs=(pspec, vec, vec, wspec, bspec),
        out_shape=(jax.ShapeDtypeStruct((2, s, c), BF16), jax.ShapeDtypeStruct((1, c), F32),
                   jax.ShapeDtypeStruct((1, c), F32), jax.ShapeDtypeStruct((g_n, CHUNK, CHUNK), F32),
                   jax.ShapeDtypeStruct((g_n, CHUNK, 1), F32)),
        scratch_shapes=[pltpu.VMEM((CHUNK, c), F32)],
        compiler_params=_params("arbitrary"),
    )(p, dy, ln_g, ln_b, ws, ws_t, bs)


SB_Q = 512


def _sb_cumsum(x, tri, edge, reverse):
    n = x.shape[0]
    hi = x.astype(BF16)
    lo = (x - hi.astype(F32)).astype(BF16)
    chunks = x.shape[1] // CHUNK
    outs = [None] * chunks
    for c in (reversed(range(chunks)) if reverse else range(chunks)):
        cols = slice(c * CHUNK, (c + 1) * CHUNK)
        both = jnp.dot(jnp.concatenate([hi[:, cols], lo[:, cols]], axis=0), tri, preferred_element_type=F32)
        outs[c] = both[:n] + both[n:] + edge
        edge = edge + jnp.sum(x[:, cols], axis=1, keepdims=True)
    return jnp.concatenate(outs, axis=1), edge


def _sb_block(q, k, offset):
    scale = HEAD_DIM ** -0.5
    z = lax.dot_general(q, k, (((1,), (1,)), ((), ())), preferred_element_type=F32) * scale
    e = jnp.exp(-jnp.abs(z))
    lp = jnp.log(1.0 + e)
    lb = jnp.minimum(z, 0.0) - lp
    l1 = jnp.minimum(-z, 0.0) - lp
    if offset is not None:
        t_i = lax.broadcasted_iota(jnp.int32, z.shape, 0)
        s_i = lax.broadcasted_iota(jnp.int32, z.shape, 1)
        mask = (t_i - s_i) > offset
        l1 = jnp.where(mask, l1, 0.0)
    else:
        mask = None
    return z, e, lb, l1, mask


def _tri(cond):
    j = lax.broadcasted_iota(jnp.int32, (CHUNK, CHUNK), 0)
    s = lax.broadcasted_iota(jnp.int32, (CHUNK, CHUNK), 1)
    return jnp.where(cond(j, s), 1.0, 0.0).astype(BF16)


def _sb_fwd(qkv, name):
    _, s, c = qkv.shape
    heads = c // HEAD_DIM
    bq = _tile(s, (SB_Q, 256, CHUNK))
    nq = s // bq

    def body(q_ref, k_ref, v_ref, o_ref, runs_ref):
        i = pl.program_id(1)
        q = q_ref[...]
        after = _tri(lambda j, s_: j > s_)
        runs_ref[...] = jnp.zeros_like(runs_ref)

        def tile(g, carry, offset):
            o_acc, run = carry
            rows = pl.ds(pl.multiple_of(g * bq, bq), bq)
            runs_ref[pl.ds(pl.multiple_of(g * 8, 8), 8), :] = jnp.transpose(jnp.broadcast_to(run, (bq, LANES)))[:8]
            _, _, lb, l1, mask = _sb_block(q, k_ref[rows, :], offset)
            acc, run = _sb_cumsum(l1, after, run, True)
            a = jnp.exp(lb + acc)
            if mask is not None:
                a = jnp.where(mask, a, 0.0)
            o_acc = o_acc + jnp.dot(a.astype(BF16), v_ref[rows, :], preferred_element_type=F32)
            return o_acc, run

        carry = tile(i, (jnp.zeros((bq, HEAD_DIM), F32), jnp.zeros((bq, 1), F32)), 0)
        o_acc, _ = lax.fori_loop(0, i, lambda n, cr: tile(i - 1 - n, cr, None), carry)
        o_ref[...] = o_acc.astype(o_ref.dtype)

    return pl.pallas_call(
        body, grid=(heads, nq), name=name,
        in_specs=[pl.BlockSpec((None, bq, HEAD_DIM), lambda h, i: (0, i, h)),
                  pl.BlockSpec((None, s, HEAD_DIM), lambda h, i: (1, 0, h)),
                  pl.BlockSpec((None, s, HEAD_DIM), lambda h, i: (2, 0, h))],
        out_specs=(pl.BlockSpec((bq, HEAD_DIM), lambda h, i: (i, h)),
                   pl.BlockSpec((None, None, 8 * nq, bq), lambda h, i: (h, i, 0, 0))),
        out_shape=(jax.ShapeDtypeStruct((s, c), BF16), jax.ShapeDtypeStruct((heads, nq, 8 * nq, bq), F32)),
        compiler_params=_params("parallel", "arbitrary"),
    )(qkv, qkv, qkv)


def _sb_bwd(qkv, do, runs, name):
    _, s, c = qkv.shape
    heads = c // HEAD_DIM
    bq = _tile(s, (SB_Q, 256, CHUNK))
    nq = s // bq
    scale = HEAD_DIM ** -0.5
    tn_dims = (((0,), (0,)), ((), ()))

    def body(q_ref, k_ref, v_ref, do_ref, runs_ref, dq_ref, dkv_ref, dk_acc, dv_acc):
        i = pl.program_id(1)
        q = q_ref[...]
        dob = do_ref[...]
        after = _tri(lambda j, s_: j > s_)
        before = _tri(lambda j, s_: j < s_)

        @pl.when(i == 0)
        def _():
            dk_acc[...] = jnp.zeros_like(dk_acc)
            dv_acc[...] = jnp.zeros_like(dv_acc)

        def main(g, carry, offset):
            dq_acc, csum = carry
            rows = pl.ds(pl.multiple_of(g * bq, bq), bq)
            kblk = k_ref[rows, :]
            vblk = v_ref[rows, :]
            z, e, lb, l1, mask = _sb_block(q, kblk, offset)
            run = jnp.transpose(jnp.broadcast_to(runs_ref[pl.ds(pl.multiple_of(g * 8, 8), 1), :], (LANES, bq)))
            acc, _ = _sb_cumsum(l1, after, run, True)
            a = jnp.exp(lb + acc)
            if mask is not None:
                a = jnp.where(mask, a, 0.0)
            d_a = lax.dot_general(dob, vblk, (((1,), (1,)), ((), ())), preferred_element_type=F32)
            ee = a * d_a
            cc, csum = _sb_cumsum(ee, before, csum, False)
            beta = jnp.where(z >= 0.0, 1.0, e) * pl.reciprocal(1.0 + e, approx=True)
            dz = (ee - (ee + cc) * beta) * scale
            if mask is not None:
                dz = jnp.where(mask, dz, 0.0)
            dzb = dz.astype(BF16)
            dq_acc = dq_acc + jnp.dot(dzb, kblk, preferred_element_type=F32)
            dk_acc[rows, :] += lax.dot_general(dzb, q, tn_dims, preferred_element_type=F32)
            dv_acc[rows, :] += lax.dot_general(a.astype(BF16), dob, tn_dims, preferred_element_type=F32)
            return dq_acc, csum

        carry = (jnp.zeros((bq, HEAD_DIM), F32), jnp.zeros((bq, 1), F32))
        carry = lax.fori_loop(0, i, lambda g, cr: main(g, cr, None), carry)
        dq_acc, _ = main(i, carry, 0)
        dq_ref[...] = dq_acc.astype(dq_ref.dtype)

        @pl.when(i == nq - 1)
        def _():
            dkv_ref[0] = dk_acc[...].astype(dkv_ref.dtype)
            dkv_ref[1] = dv_acc[...].astype(dkv_ref.dtype)

    blk = pl.BlockSpec((bq, HEAD_DIM), lambda h, i: (i, h))
    return pl.pallas_call(
        body, grid=(heads, nq), name=name,
        in_specs=[pl.BlockSpec((None, bq, HEAD_DIM), lambda h, i: (0, i, h)),
                  pl.BlockSpec((None, s, HEAD_DIM), lambda h, i: (1, 0, h)),
                  pl.BlockSpec((None, s, HEAD_DIM), lambda h, i: (2, 0, h)),
                  blk,
                  pl.BlockSpec((None, None, 8 * nq, bq), lambda h, i: (h, i, 0, 0))],
        out_specs=(blk, pl.BlockSpec((2, s, HEAD_DIM), lambda h, i: (0, 0, h))),
        out_shape=(jax.ShapeDtypeStruct((s, c), BF16), jax.ShapeDtypeStruct((2, s, c), BF16)),
        scratch_shapes=[pltpu.VMEM((s, HEAD_DIM), F32), pltpu.VMEM((s, HEAD_DIM), F32)],
        compiler_params=_params("parallel", "arbitrary"),
    )(qkv, qkv, qkv, do, runs)


def _all_gather(x, axis, name):
    return _run_jobs([_GatherJob(x, axis)], name)[0]


def _rs_core_exchange(g, axis, name):
    r, c = (g.shape[0] // N_DEV, g.shape[1]) if axis == 0 else (g.shape[0], g.shape[1] // N_DEV)

    def body(g_ref, recv_ref, send_sems, recv_sems):
        mx, my, mc = lax.axis_index("x"), lax.axis_index("y"), lax.axis_index("c")
        copies = [pltpu.make_async_remote_copy(
            src_ref=_shard_of(g_ref, 2 * k + (1 - mc), axis, r, c), dst_ref=recv_ref.at[k],
            send_sem=send_sems.at[k], recv_sem=recv_sems.at[k], device_id=(mx, my, 1 - mc), device_id_type=MESH)
            for k in range(4)]
        for cp in copies:
            cp.start()
        for cp in copies:
            cp.wait()

    return pl.pallas_call(
        body, name=name,
        out_shape=jax.ShapeDtypeStruct((4, r, c), g.dtype),
        in_specs=[ANY], out_specs=ANY,
        scratch_shapes=[pltpu.SemaphoreType.DMA((4,)), pltpu.SemaphoreType.DMA((4,))],
    )(g)


def _ew_tiles(r, c):
    tc = c if c <= 2048 else _tile(c, (2048, 1024, 512, 256, 128))
    return _tile(r, [p for p in (512, 256, 128, 64, 32, 16, 8) if p * tc <= 256 * 1024]), tc


def _rs_core_add(g, recv, core, axis, name):
    _, r, c = recv.shape
    tr, tc = _ew_tiles(r, c)
    nrb, ncb = r // tr, c // tc

    def body(core_ref, g_ref, x_ref, o_ref):
        del core_ref
        o_ref[...] = (g_ref[...].astype(F32) + x_ref[...].astype(F32)).astype(o_ref.dtype)

    if axis == 0:
        g_spec = pl.BlockSpec((tr, tc), lambda k, i, j, core: ((2 * k + core[0]) * nrb + i, j))
    else:
        g_spec = pl.BlockSpec((tr, tc), lambda k, i, j, core: (i, (2 * k + core[0]) * ncb + j))
    blk = pl.BlockSpec((None, tr, tc), lambda k, i, j, core: (k, i, j))
    return pl.pallas_call(
        body, name=name,
        grid_spec=pltpu.PrefetchScalarGridSpec(
            num_scalar_prefetch=1, grid=(4, nrb, ncb), in_specs=[g_spec, blk], out_specs=blk),
        out_shape=jax.ShapeDtypeStruct(recv.shape, BF16),
        compiler_params=_params("parallel", "parallel", "parallel"),
    )(core, g, recv)


def _rs_chip_exchange(parts, name):
    return _run_jobs([_ChipExchangeJob(p) for p in parts], name)


def _adamw_math(w, g, m, v):
    m = ADAM_B1 * m + (1.0 - ADAM_B1) * g
    v = ADAM_B2 * v + (1.0 - ADAM_B2) * (g * g)
    m_hat = m / (1.0 - ADAM_B1 ** ADAM_STEP)
    v_hat = v / (1.0 - ADAM_B2 ** ADAM_STEP)
    delta = -ADAM_LR * (m_hat / (jnp.sqrt(v_hat) + ADAM_EPS) + ADAM_WD * w)
    return delta, m, v


def _adamw_shard(w, m, v, layer, p, recv, chip, name):
    _, r, c = w.shape
    tr, tc = _ew_tiles(r, c)

    def body(chip_ref, w_ref, m_ref, v_ref, p_ref, x_ref, g_out, d_out, m_out, v_out):
        del chip_ref
        g = p_ref[...].astype(F32)
        for j in range(3):
            g = g + x_ref[j].astype(F32)
        d, mn, vn = _adamw_math(w_ref[...], g, m_ref[...], v_ref[...])
        g_out[...] = g
        d_out[...] = d
        m_out[...] = mn
        v_out[...] = vn

    blk = pl.BlockSpec((tr, tc), lambda i, j, chip: (i, j))
    lay = pl.BlockSpec((None, tr, tc), lambda i, j, chip: (layer, i, j))
    out = jax.ShapeDtypeStruct((r, c), F32)
    return pl.pallas_call(
        body, name=name,
        grid_spec=pltpu.PrefetchScalarGridSpec(
            num_scalar_prefetch=1, grid=(r // tr, c // tc),
            in_specs=[lay, lay, lay, pl.BlockSpec((None, tr, tc), lambda i, j, chip: (chip[0], i, j)),
                      pl.BlockSpec((3, tr, tc), lambda i, j, chip: (0, i, j))],
            out_specs=(blk, blk, blk, blk)),
        out_shape=(out, out, out, out),
        compiler_params=_params("parallel", "parallel"),
    )(chip, w, m, v, p, recv)


def _adamw_packed(w, m, v, g, name):
    r, c = w.shape
    tr = _tile(r, (512, 256, 128, 64, 32, 16, 8))

    def body(w_ref, m_ref, v_ref, g_ref, d_out, m_out, v_out):
        d, mn, vn = _adamw_math(w_ref[...], g_ref[...], m_ref[...], v_ref[...])
        d_out[...] = d
        m_out[...] = mn
        v_out[...] = vn

    blk = pl.BlockSpec((tr, c), lambda i: (i, 0))
    out = jax.ShapeDtypeStruct((r, c), F32)
    return pl.pallas_call(
        body, grid=(r // tr,), name=name, in_specs=[blk, blk, blk, blk], out_specs=(blk, blk, blk),
        out_shape=(out, out, out), compiler_params=_params("parallel"),
    )(w, m, v, g)


def _sum_devices(x, name):
    r = x.shape[0] // N_DEV
    tr = _tile(r, (512, 256, 128, 64, 32, 16, 8))

    def body(x_ref, o_ref):
        acc = x_ref[0]
        for d in range(1, N_DEV):
            acc = acc + x_ref[d]
        o_ref[...] = acc

    return pl.pallas_call(
        body, grid=(r // tr,), name=name,
        in_specs=[pl.BlockSpec((N_DEV, tr, LANES), lambda i: (0, i, 0))],
        out_specs=pl.BlockSpec((tr, LANES), lambda i: (i, 0)),
        out_shape=jax.ShapeDtypeStruct((r, LANES), F32),
        compiler_params=_params("parallel"),
    )(x.reshape(N_DEV, r, LANES))


def _pack(arrays):
    flat = []
    for a in arrays:
        f = a.reshape(-1).astype(F32)
        flat.append(jnp.pad(f, (0, (-f.shape[0]) % LANES)))
    f = jnp.concatenate(flat)
    f = jnp.pad(f, (0, (-f.shape[0]) % (8 * LANES)))
    return f.reshape(-1, LANES)


def _unpack(packed, shapes):
    flat = packed.reshape(-1)
    out, off = [], 0
    for shp in shapes:
        n = 1
        for d in shp:
            n *= d
        out.append(flat[off:off + n].reshape(shp))
        off += n + (-n) % LANES
    return out


def kernel(x, norm_mix_pre, norm_mix_post, norm_ffn_pre, norm_ffn_post, sc_w_in, sc_conv_w, sc_w_out, sg_w_in, sg_ln_g, sg_ln_b, sg_w_s, sg_b_s, sg_w_out, sb_w_qkv, sb_w_out, ffn_w_up, ffn_conv_w, ffn_conv_b, ffn_w_down, loss_target, m_norm_mix_pre, m_norm_mix_post, m_norm_ffn_pre, m_norm_ffn_post, m_sc_w_in, m_sc_conv_w, m_sc_w_out, m_sg_w_in, m_sg_ln_g, m_sg_ln_b, m_sg_w_s, m_sg_b_s, m_sg_w_out, m_sb_w_qkv, m_sb_w_out, m_ffn_w_up, m_ffn_conv_w, m_ffn_conv_b, m_ffn_w_down, v_norm_mix_pre, v_norm_mix_post, v_norm_ffn_pre, v_norm_ffn_post, v_sc_w_in, v_sc_conv_w, v_sc_w_out, v_sg_w_in, v_sg_ln_g, v_sg_ln_b, v_sg_w_s, v_sg_b_s, v_sg_w_out, v_sb_w_qkv, v_sb_w_out, v_ffn_w_up, v_ffn_conv_w, v_ffn_conv_b, v_ffn_w_down):
    weights = dict(norm_mix_pre=norm_mix_pre, norm_mix_post=norm_mix_post, norm_ffn_pre=norm_ffn_pre,
                   norm_ffn_post=norm_ffn_post, sc_w_in=sc_w_in, sc_conv_w=sc_conv_w, sc_w_out=sc_w_out,
                   sg_w_in=sg_w_in, sg_ln_g=sg_ln_g, sg_ln_b=sg_ln_b, sg_w_s=sg_w_s, sg_b_s=sg_b_s,
                   sg_w_out=sg_w_out, sb_w_qkv=sb_w_qkv, sb_w_out=sb_w_out, ffn_w_up=ffn_w_up,
                   ffn_conv_w=ffn_conv_w, ffn_conv_b=ffn_conv_b, ffn_w_down=ffn_w_down)
    mom1 = dict(norm_mix_pre=m_norm_mix_pre, norm_mix_post=m_norm_mix_post, norm_ffn_pre=m_norm_ffn_pre,
                norm_ffn_post=m_norm_ffn_post, sc_w_in=m_sc_w_in, sc_conv_w=m_sc_conv_w, sc_w_out=m_sc_w_out,
                sg_w_in=m_sg_w_in, sg_ln_g=m_sg_ln_g, sg_ln_b=m_sg_ln_b, sg_w_s=m_sg_w_s, sg_b_s=m_sg_b_s,
                sg_w_out=m_sg_w_out, sb_w_qkv=m_sb_w_qkv, sb_w_out=m_sb_w_out, ffn_w_up=m_ffn_w_up,
                ffn_conv_w=m_ffn_conv_w, ffn_conv_b=m_ffn_conv_b, ffn_w_down=m_ffn_w_down)
    mom2 = dict(norm_mix_pre=v_norm_mix_pre, norm_mix_post=v_norm_mix_post, norm_ffn_pre=v_norm_ffn_pre,
                norm_ffn_post=v_norm_ffn_post, sc_w_in=v_sc_w_in, sc_conv_w=v_sc_conv_w, sc_w_out=v_sc_w_out,
                sg_w_in=v_sg_w_in, sg_ln_g=v_sg_ln_g, sg_ln_b=v_sg_ln_b, sg_w_s=v_sg_w_s, sg_b_s=v_sg_b_s,
                sg_w_out=v_sg_w_out, sb_w_qkv=v_sb_w_qkv, sb_w_out=v_sb_w_out, ffn_w_up=v_ffn_w_up,
                ffn_conv_w=v_ffn_conv_w, ffn_conv_b=v_ffn_conv_b, ffn_w_down=v_ffn_w_down)
    order = list(weights)

    mx, my, mc = lax.axis_index("x"), lax.axis_index("y"), lax.axis_index("c")
    dev = 4 * mx + 2 * my + mc
    core_arr = jnp.reshape(mc, (1,)).astype(jnp.int32)
    chip_arr = jnp.reshape(2 * mx + my, (1,)).astype(jnp.int32)

    h = x[0]
    target = loss_target[0]
    depth = norm_mix_pre.shape[0]
    d_model = h.shape[1]
    f_loc = ffn_w_down.shape[1]
    d_ff = f_loc * N_DEV

    def gather_rows8(a2d, name):
        rows = a2d.shape[0]
        pad = (-rows) % 8
        return _all_gather(jnp.pad(a2d, ((0, pad), (0, 0))), 1, name)[:rows]

    n_a = sc_w_in.shape[0]
    sc_conv_full = gather_rows8(sc_conv_w.reshape(n_a * 3, -1), "ag_sc_conv").reshape(n_a, 3, d_model)
    ffn_conv_full = gather_rows8(ffn_conv_w.reshape(depth * 3, -1), "ag_ffn_conv").reshape(depth, 3, 2 * d_ff)

    big = {}

    kinds = [i % 3 for i in range(depth)]
    mixer_names = [("sc_w_in", "sc_w_out"), ("sg_w_in", "sg_w_out"), ("sb_w_qkv", "sb_w_out")]
    mixer_split = [3, 2, 3]
    mixer_proj = ["sc_in_proj", "sg_in_proj", "sb_qkv_proj"]

    def layer_keys(i):
        wi, wo = mixer_names[kinds[i]]
        return (wi, i // 3, 1), (wo, i // 3, 0), ("ffn_w_up", i, 1), ("ffn_w_down", i, 0)

    def gather_job(key):
        name, j, axis = key
        return _GatherJob(weights[name][j].astype(BF16), axis)

    def hosted_mm(a, key, out_dtype, nsplit, name, gather_keys):
        keys = [k for k in gather_keys if k is not None]
        res = _mm_nn(a, big[key[:2]], out_dtype, nsplit, name, jobs=[gather_job(k) for k in keys])
        if not keys:
            return res
        for k, full in zip(keys, res[1:]):
            big[k[:2]] = full
        return res[0]

    sg_ws_t = jnp.swapaxes(sg_w_s, -1, -2)
    sg_bs3 = sg_b_s[..., None]

    k_in0 = layer_keys(0)[0]
    big[k_in0[:2]] = _all_gather(weights[k_in0[0]][k_in0[1]].astype(BF16), k_in0[2], f"ag_{k_in0[0]}")
    saved = []
    for i in range(depth):
        j = i // 3
        k_in, k_out, k_up, k_down = layer_keys(i)
        nxt = layer_keys(i + 1) if i + 1 < depth else (None, None, None, None)
        sv = {"h_in": h}
        hn = _rms_fwd(h, norm_mix_pre[i:i + 1], "rms_fwd")
        sv["hn"] = hn
        p = hosted_mm(hn, k_in, BF16, mixer_split[kinds[i]], mixer_proj[kinds[i]],
                      [k_out, k_up] if i == 0 else [])
        if kinds[i] == 0:
            y = _sc_gate_fwd(p, sc_conv_full[j], "sc_gate_fwd")
        elif kinds[i] == 1:
            y = _sgu_fwd(p, sg_ln_g[j:j + 1], sg_ln_b[j:j + 1], sg_w_s[j], sg_bs3[j], "sgu_fwd")
        else:
            y, sv["runs"] = _sb_fwd(p, "sb_fwd")
        m = hosted_mm(y, k_out, F32, 1, "mix_out_proj", [])
        sv.update(p=p, y=y, m=m)
        h = _resid_rms_fwd(h, m, norm_mix_post[i:i + 1], "resid_rms_fwd")
        sv["h_mid"] = h
        hn2 = _rms_fwd(h, norm_ffn_pre[i:i + 1], "rms_fwd")
        u = hosted_mm(hn2, k_up, BF16, 2, "ffn_up_proj", [k_down, nxt[0], nxt[1]])
        cw = ffn_conv_full[i].reshape(3, 2, d_ff).transpose(1, 0, 2)
        cb = ffn_conv_b[i].reshape(2, 1, d_ff)
        a = _ffn_act_fwd(u, cw, cb, "ffn_act_fwd")
        f = hosted_mm(a, k_down, F32, 1, "ffn_down_proj", [nxt[2]])
        sv.update(hn2=hn2, u=u, a=a, f=f, cw=cw, cb=cb)
        h = _resid_rms_fwd(h, f, norm_ffn_post[i:i + 1], "resid_rms_fwd")
        saved.append(sv)

    dh, loss_part = _loss_head(h, target, "loss_head")

    grads_big = {}
    small = {}

    parts = {}
    queue = []

    def reduce_scatter(name, j, g_full, axis):
        recv1 = _rs_core_exchange(g_full, axis, f"rs1_{name}")
        parts[(name, j)] = _rs_core_add(g_full, recv1, core_arr, axis, f"rs_add_{name}")
        queue.append((name, j))

    def hosting(mm, a, b, out_dtype, name, n_jobs):
        keys = [queue.pop(0) for _ in range(min(n_jobs, len(queue)))]
        res = mm(a, b, out_dtype, name, jobs=[_ChipExchangeJob(parts[k]) for k in keys])
        if not keys:
            return res
        for k, recv in zip(keys, res[1:]):
            grads_big[k] = (parts[k], recv)
        return res[0]

    for i in reversed(range(depth)):
        j = i // 3
        sv = saved[i]
        df, dg = _rms_bwd(sv["f"], norm_ffn_post[i:i + 1], dh, None, BF16, "rms_bwd_post")
        small.setdefault("norm_ffn_post", {})[i] = dg
        da = hosting(_mm_nt, df, big[("ffn_w_down", i)], BF16, "ffn_down_dx", 1)
        reduce_scatter("ffn_w_down", i, hosting(_mm_tn, sv["a"], df, BF16, "ffn_down_dw", 2), 0)
        du, dcw, dcb = _ffn_act_bwd(sv["u"], da, sv["cw"], sv["cb"], "ffn_act_bwd")
        small.setdefault("ffn_conv_w", {})[i] = dcw.transpose(1, 0, 2).reshape(3, 2 * d_ff)
        small.setdefault("ffn_conv_b", {})[i] = dcb.reshape(2 * d_ff)
        dhn2 = _mm_nt(du, big[("ffn_w_up", i)], F32, "ffn_up_dx")
        reduce_scatter("ffn_w_up", i, _mm_tn(sv["hn2"], du, BF16, "ffn_up_dw"), 1)
        dh, dg = _rms_bwd(sv["h_mid"], norm_ffn_pre[i:i + 1], dhn2, dh, F32, "rms_bwd_pre")
        small.setdefault("norm_ffn_pre", {})[i] = dg
        dm, dg = _rms_bwd(sv["m"], norm_mix_post[i:i + 1], dh, None, BF16, "rms_bwd_post")
        small.setdefault("norm_mix_post", {})[i] = dg
        if kinds[i] == 0:
            wo, wi = "sc_w_out", "sc_w_in"
        elif kinds[i] == 1:
            wo, wi = "sg_w_out", "sg_w_in"
        else:
            wo, wi = "sb_w_out", "sb_w_qkv"
        dy = hosting(_mm_nt, dm, big[(wo, j)], BF16, "mix_out_dx", 1)
        reduce_scatter(wo, j, _mm_tn(sv["y"], dm, BF16, "mix_out_dw"), 0)
        if kinds[i] == 0:
            dp, dcw = _sc_gate_bwd(sv["p"], dy, sc_conv_full[j], "sc_gate_bwd")
            small.setdefault("sc_conv_w", {})[j] = dcw
        elif kinds[i] == 1:
            dp, dlg, dlb, dws, dbs = _sgu_bwd(sv["p"], dy, sg_ln_g[j:j + 1], sg_ln_b[j:j + 1], sg_w_s[j], sg_ws_t[j],
                                              sg_bs3[j], "sgu_bwd")
            small.setdefault("sg_ln_g", {})[j] = dlg[0]
            small.setdefault("sg_ln_b", {})[j] = dlb[0]
            small.setdefault("sg_w_s", {})[j] = dws
            small.setdefault("sg_b_s", {})[j] = dbs[..., 0]
        else:
            dq, dkv = _sb_bwd(sv["p"], dy, sv["runs"], "sb_bwd")
            dp = jnp.concatenate([dq[None], dkv], axis=0)
        dhn = _mm_nt(dp, big[(wi, j)], F32, "mix_in_dx")
        reduce_scatter(wi, j, _mm_tn(sv["hn"], dp, BF16, "mix_in_dw"), 1)
        dh, dg = _rms_bwd(sv["h_in"], norm_mix_pre[i:i + 1], dhn, dh, F32, "rms_bwd_pre")
        small.setdefault("norm_mix_pre", {})[i] = dg

    grad_x = dh[None]
    for k, recv in zip(queue, _rs_chip_exchange([parts[k] for k in queue], "rs2_tail")):
        grads_big[k] = (parts[k], recv)

    small_names = ["norm_mix_pre", "norm_mix_post", "norm_ffn_pre", "norm_ffn_post", "sg_ln_g", "sg_ln_b", "sg_w_s",
                   "sg_b_s", "ffn_conv_b", "sc_conv_w", "ffn_conv_w"]
    full_shapes = {n: weights[n].shape for n in small_names}
    full_shapes["sc_conv_w"] = (n_a, 3, d_model)
    full_shapes["ffn_conv_w"] = (depth, 3, 2 * d_ff)
    partial = [jnp.stack([small[n][k].reshape(full_shapes[n][1:]) for k in sorted(small[n])]) for n in small_names]
    packed = _pack(partial + [loss_part])
    gathered = _all_gather(packed, 0, "ag_small_grads")
    summed = _sum_devices(gathered, "sum_small_grads")
    pieces = _unpack(summed, [full_shapes[n] for n in small_names] + [(1, 1)])
    loss = pieces[-1].reshape(())
    small_grads = dict(zip(small_names, pieces[:-1]))
    for n in ("sc_conv_w", "ffn_conv_w"):
        c_loc = weights[n].shape[-1]
        small_grads[n] = lax.dynamic_slice_in_dim(small_grads[n], dev * c_loc, c_loc, axis=2)
    shapes = [weights[n].shape for n in small_names]
    d_p, m_p, v_p = _adamw_packed(_pack([weights[n] for n in small_names]), _pack([mom1[n] for n in small_names]),
                                  _pack([mom2[n] for n in small_names]), _pack([small_grads[n] for n in small_names]),
                                  "adamw_small")
    out_g, out_d, out_m, out_v = dict(small_grads), {}, {}, {}
    for n, d_, m_, v_ in zip(small_names, _unpack(d_p, shapes), _unpack(m_p, shapes), _unpack(v_p, shapes)):
        out_d[n], out_m[n], out_v[n] = d_, m_, v_

    for n in order:
        if n in small_names:
            continue
        res = [_adamw_shard(weights[n], mom1[n], mom2[n], j, *grads_big[(n, j)], chip_arr, f"adamw_{n}")
               for j in range(weights[n].shape[0])]
        out_g[n], out_d[n], out_m[n], out_v[n] = (jnp.stack([r_[t] for r_ in res]) for t in range(4))

    return (loss, grad_x, *[out_g[n] for n in order], *[out_d[n] for n in order],
            *[out_m[n] for n in order], *[out_v[n] for n in order])
```

```python
import functools

import jax
import jax.numpy as jnp
from jax import lax
from jax.experimental import pallas as pl
from jax.experimental.pallas import tpu as pltpu

F32 = jnp.float32
BF16 = jnp.bfloat16

EPS = 1e-6
CHUNK = 128
HEAD_DIM = 128
SG_GROUPS = 8
HALO = 16
N_DEV = 8
LANES = 128
V7X_VMEM_LIMIT = 56 * 1024 * 1024

ADAM_LR = 0.001
ADAM_B1 = 0.9
ADAM_B2 = 0.999
ADAM_EPS = 1e-08
ADAM_WD = 0.01
ADAM_STEP = 10

MESH = pl.DeviceIdType.MESH
ANY = pl.BlockSpec(memory_space=pl.ANY)


def _tile(n, prefs):
    for p in prefs:
        if p <= n and n % p == 0:
            return p
    return n


def _params(*sem):
    return pltpu.CompilerParams(dimension_semantics=sem, vmem_limit_bytes=V7X_VMEM_LIMIT)


def _shard_of(ref, dev, axis, r, c):
    if axis == 0:
        return ref.at[pl.ds(dev * r, r), :]
    return ref.at[:, pl.ds(dev * c, c)]


class _GatherJob:
    def __init__(self, x, axis):
        self.axis = axis
        self.r, self.c = x.shape
        full = (N_DEV * self.r, self.c) if axis == 0 else (self.r, N_DEV * self.c)
        self.inputs = [x]
        self.out_shapes = [jax.ShapeDtypeStruct(full, x.dtype)]
        self.scratch = [pltpu.SemaphoreType.DMA((7,)), pltpu.SemaphoreType.DMA((7,)), pltpu.SemaphoreType.DMA(())]

    def _plan(self, ins, outs, scr):
        (x_ref,), (out_ref,), (send_sems, recv_sems, local_sem) = ins, outs, scr
        mx, my, mc = lax.axis_index("x"), lax.axis_index("y"), lax.axis_index("c")
        me, sibling = (mx, my, mc), (mx, my, 1 - mc)
        chips = [(1 - mx, my), (mx, 1 - my), (1 - mx, 1 - my)]

        def rows(px, py, pc):
            return _shard_of(out_ref, 4 * px + 2 * py + pc, self.axis, self.r, self.c)

        def copy(k, block, to, src=None):
            return pltpu.make_async_remote_copy(
                src_ref=rows(*block) if src is None else src, dst_ref=rows(*block),
                send_sem=send_sems.at[k], recv_sem=recv_sems.at[k], device_id=to, device_id_type=MESH)

        mine = pltpu.make_async_copy(x_ref, rows(*me), local_sem)
        first = [copy(0, me, sibling, src=x_ref)] + [copy(1 + j, me, (*chip, mc), src=x_ref)
                                                      for j, chip in enumerate(chips)]
        passed = [copy(4 + j, (*chip, mc), sibling) for j, chip in enumerate(chips)]
        landed = [copy(1 + j, (*chip, mc), me) for j, chip in enumerate(chips)]
        from_sibling = [copy(0, sibling, me)] + [copy(4 + j, (*chip, 1 - mc), me) for j, chip in enumerate(chips)]
        return mine, first, passed, landed, from_sibling

    def start(self, ins, outs, scr):
        mine, first, _, _, _ = self._plan(ins, outs, scr)
        mine.start()
        for cp in first:
            cp.start()

    def finish(self, ins, outs, scr):
        mine, first, passed, landed, from_sibling = self._plan(ins, outs, scr)
        for j in range(3):
            landed[j].wait_recv()
            passed[j].start()
        for cp in from_sibling:
            cp.wait_recv()
        for cp in first + passed:
            cp.wait_send()
        mine.wait()


class _ChipExchangeJob:
    def __init__(self, p):
        _, r, c = p.shape
        self.inputs = [p]
        self.out_shapes = [jax.ShapeDtypeStruct((3, r, c), p.dtype)]
        self.scratch = [pltpu.SemaphoreType.DMA((3,)), pltpu.SemaphoreType.DMA((3,))]

    def _plan(self, ins, outs, scr):
        (p_ref,), (recv_ref,), (send_sems, recv_sems) = ins, outs, scr
        mx, my, mc = lax.axis_index("x"), lax.axis_index("y"), lax.axis_index("c")
        chips = [(1 - mx, my), (mx, 1 - my), (1 - mx, 1 - my)]
        return [pltpu.make_async_remote_copy(
            src_ref=p_ref.at[2 * px + py], dst_ref=recv_ref.at[j],
            send_sem=send_sems.at[j], recv_sem=recv_sems.at[j], device_id=(px, py, mc), device_id_type=MESH)
            for j, (px, py) in enumerate(chips)]

    def start(self, ins, outs, scr):
        for cp in self._plan(ins, outs, scr):
            cp.start()

    def finish(self, ins, outs, scr):
        for cp in self._plan(ins, outs, scr):
            cp.wait()


class _Hosted:
    def __init__(self, jobs):
        self.jobs = list(jobs)
        self.inputs = [x for j in self.jobs for x in j.inputs]
        self.out_shapes = [s for j in self.jobs for s in j.out_shapes]
        self.scratch = [s for j in self.jobs for s in j.scratch]
        self.in_specs = [ANY] * len(self.inputs)
        self.out_specs = [ANY] * len(self.out_shapes)

    def split(self, refs, n_in, n_out, n_scratch):
        refs = list(refs)
        ji, jo, js = len(self.inputs), len(self.out_shapes), len(self.scratch)
        ins, refs = refs[:n_in], refs[n_in:]
        jins, refs = refs[:ji], refs[ji:]
        outs, refs = refs[:n_out], refs[n_out:]
        jouts, refs = refs[:jo], refs[jo:]
        scr, jscr = refs[:n_scratch], refs[n_scratch:]
        assert len(jscr) == js
        per_job = []
        for j in self.jobs:
            a, b, c = len(j.inputs), len(j.out_shapes), len(j.scratch)
            per_job.append((jins[:a], jouts[:b], jscr[:c]))
            jins, jouts, jscr = jins[a:], jouts[b:], jscr[c:]
        return ins, outs, scr, per_job

    def start(self, per_job, when):
        if self.jobs:
            @pl.when(when)
            def _():
                for j, refs in zip(self.jobs, per_job):
                    j.start(*refs)

    def finish(self, per_job, when):
        if self.jobs:
            @pl.when(when)
            def _():
                for j, refs in zip(self.jobs, per_job):
                    j.finish(*refs)


def _grid_edges(grid):
    first = last = None
    for ax, n in enumerate(grid):
        f, l = pl.program_id(ax) == 0, pl.program_id(ax) == n - 1
        first = f if first is None else jnp.logical_and(first, f)
        last = l if last is None else jnp.logical_and(last, l)
    return first, last


def _run_jobs(jobs, name):
    hosted = _Hosted(jobs)

    def body(*refs):
        _, _, _, per_job = hosted.split(refs, 0, 0, 0)
        for j, r in zip(hosted.jobs, per_job):
            j.start(*r)
        for j, r in zip(hosted.jobs, per_job):
            j.finish(*r)

    return pl.pallas_call(
        body, name=name, out_shape=tuple(hosted.out_shapes), in_specs=hosted.in_specs,
        out_specs=tuple(hosted.out_specs), scratch_shapes=hosted.scratch,
    )(*hosted.inputs)


def _mm_nn(a, b, out_dtype, nsplit, name, jobs=()):
    m, k = a.shape
    n = b.shape[1]
    w = n // nsplit
    tm = _tile(m, (1024, 512, 256, 128)) if k <= 2048 else _tile(m, (512, 256, 128))
    tn = _tile(w, (512, 256, 128))
    per = w // tn
    grid = (m // tm, n // tn)
    hosted = _Hosted(jobs)

    def body(*refs):
        (a_ref, b_ref), (o_ref,), _, per_job = hosted.split(refs, 2, 1, 0)
        first, last = _grid_edges(grid)
        hosted.start(per_job, first)
        o_ref[...] = jnp.dot(a_ref[...], b_ref[...], preferred_element_type=F32).astype(o_ref.dtype)
        hosted.finish(per_job, last)

    if nsplit == 1:
        out_shape = jax.ShapeDtypeStruct((m, n), out_dtype)
        out_spec = pl.BlockSpec((tm, tn), lambda i, j: (i, j))
    else:
        out_shape = jax.ShapeDtypeStruct((nsplit, m, w), out_dtype)
        out_spec = pl.BlockSpec((None, tm, tn), lambda i, j: (j // per, i, j % per))
    res = pl.pallas_call(
        body, grid=grid, name=name,
        in_specs=[pl.BlockSpec((tm, k), lambda i, j: (i, 0)), pl.BlockSpec((k, tn), lambda i, j: (0, j))]
        + hosted.in_specs,
        out_specs=tuple([out_spec] + hosted.out_specs), out_shape=tuple([out_shape] + hosted.out_shapes),
        scratch_shapes=hosted.scratch,
        compiler_params=_params("arbitrary", "arbitrary"),
    )(a, b, *hosted.inputs)
    return res if jobs else res[0]


def _mm_nt(a, b, out_dtype, name, jobs=()):
    a3 = a if a.ndim == 3 else a[None]
    ns, m, w = a3.shape
    ko, n = b.shape
    assert n == ns * w
    tm = _tile(m, (1024, 512, 256, 128)) if n <= 2048 else _tile(m, (512, 256, 128))
    tko = _tile(ko, (256, 128)) if n > 8192 else _tile(ko, (512, 256, 128))
    grid = (m // tm, ko // tko)
    hosted = _Hosted(jobs)

    def body(*refs):
        (a_ref, b_ref), (o_ref,), _, per_job = hosted.split(refs, 2, 1, 0)
        first, last = _grid_edges(grid)
        hosted.start(per_job, first)
        acc = None
        for s in range(ns):
            p = lax.dot_general(a_ref[s], b_ref[:, s * w:(s + 1) * w], (((1,), (1,)), ((), ())),
                                preferred_element_type=F32)
            acc = p if acc is None else acc + p
        o_ref[...] = acc.astype(o_ref.dtype)
        hosted.finish(per_job, last)

    res = pl.pallas_call(
        body, grid=grid, name=name,
        in_specs=[pl.BlockSpec((ns, tm, w), lambda i, j: (0, i, 0)), pl.BlockSpec((tko, n), lambda i, j: (j, 0))]
        + hosted.in_specs,
        out_specs=tuple([pl.BlockSpec((tm, tko), lambda i, j: (i, j))] + hosted.out_specs),
        out_shape=tuple([jax.ShapeDtypeStruct((m, ko), out_dtype)] + hosted.out_shapes),
        scratch_shapes=hosted.scratch,
        compiler_params=_params("arbitrary", "arbitrary"),
    )(a3, b, *hosted.inputs)
    return res if jobs else res[0]


def _mm_tn(a, b, out_dtype, name, jobs=()):
    b3 = b if b.ndim == 3 else b[None]
    ns, m, w = b3.shape
    k = a.shape[1]
    n = ns * w
    tk = _tile(k, (512, 256, 128))
    tn = _tile(w, (2816, 2048, 1024, 512, 256, 128))
    ts = _tile(m, (1024, 512, 256, 128))
    per = w // tn
    n_s = m // ts
    grid = (k // tk, n // tn, n_s)
    hosted = _Hosted(jobs)

    def body(*refs):
        (a_ref, b_ref), (o_ref,), (acc_ref,), per_job = hosted.split(refs, 2, 1, 1)
        s = pl.program_id(2)
        first, last = _grid_edges(grid)
        hosted.start(per_job, first)

        @pl.when(s == 0)
        def _():
            acc_ref[...] = jnp.zeros_like(acc_ref)

        acc_ref[...] += lax.dot_general(a_ref[...], b_ref[...], (((0,), (0,)), ((), ())),
                                        preferred_element_type=F32)

        @pl.when(s == n_s - 1)
        def _():
            o_ref[...] = acc_ref[...].astype(o_ref.dtype)

        hosted.finish(per_job, last)

    res = pl.pallas_call(
        body, grid=grid, name=name,
        in_specs=[pl.BlockSpec((ts, tk), lambda i, j, s: (s, i)),
                  pl.BlockSpec((None, ts, tn), lambda i, j, s: (j // per, s, j % per))] + hosted.in_specs,
        out_specs=tuple([pl.BlockSpec((tk, tn), lambda i, j, s: (i, j))] + hosted.out_specs),
        out_shape=tuple([jax.ShapeDtypeStruct((k, n), out_dtype)] + hosted.out_shapes),
        scratch_shapes=[pltpu.VMEM((tk, tn), F32)] + hosted.scratch,
        compiler_params=_params("arbitrary", "arbitrary", "arbitrary"),
    )(a, b3, *hosted.inputs)
    return res if jobs else res[0]


def _rms_rows(s):
    return _tile(s, (256, 128))


def _rms_fwd(x, g, name):
    s, d = x.shape
    ts = _rms_rows(s)

    def body(x_ref, g_ref, o_ref):
        xf = x_ref[...]
        r = lax.rsqrt(jnp.mean(xf * xf, axis=-1, keepdims=True) + EPS)
        o_ref[...] = ((xf * r) * g_ref[...]).astype(o_ref.dtype)

    return pl.pallas_call(
        body, grid=(s // ts,), name=name,
        in_specs=[pl.BlockSpec((ts, d), lambda i: (i, 0)), pl.BlockSpec((1, d), lambda i: (0, 0))],
        out_specs=pl.BlockSpec((ts, d), lambda i: (i, 0)),
        out_shape=jax.ShapeDtypeStruct((s, d), BF16),
        compiler_params=_params("parallel"),
    )(x, g)


def _resid_rms_fwd(h, m, g, name):
    s, d = h.shape
    ts = _rms_rows(s)

    def body(h_ref, m_ref, g_ref, o_ref):
        mf = m_ref[...]
        r = lax.rsqrt(jnp.mean(mf * mf, axis=-1, keepdims=True) + EPS)
        o_ref[...] = h_ref[...] + (mf * r) * g_ref[...]

    return pl.pallas_call(
        body, grid=(s // ts,), name=name,
        in_specs=[pl.BlockSpec((ts, d), lambda i: (i, 0)), pl.BlockSpec((ts, d), lambda i: (i, 0)),
                  pl.BlockSpec((1, d), lambda i: (0, 0))],
        out_specs=pl.BlockSpec((ts, d), lambda i: (i, 0)),
        out_shape=jax.ShapeDtypeStruct((s, d), F32),
        compiler_params=_params("parallel"),
    )(h, m, g)


def _rms_bwd(x, g, dy, dres, out_dtype, name):
    s, d = x.shape
    ts = _rms_rows(s)
    has_res = dres is not None

    def body(*refs):
        if has_res:
            x_ref, g_ref, dy_ref, dres_ref, dx_ref, dg_ref = refs
        else:
            x_ref, g_ref, dy_ref, dx_ref, dg_ref = refs
        i = pl.program_id(0)
        xf = x_ref[...]
        dyf = dy_ref[...].astype(F32)
        r = lax.rsqrt(jnp.mean(xf * xf, axis=-1, keepdims=True) + EPS)
        xh = xf * r
        dxh = dyf * g_ref[...]
        dx = r * (dxh - xh * jnp.mean(dxh * xh, axis=-1, keepdims=True))
        if has_res:
            dx = dx + dres_ref[...]
        dx_ref[...] = dx.astype(dx_ref.dtype)

        @pl.when(i == 0)
        def _():
            dg_ref[...] = jnp.zeros_like(dg_ref)

        dg_ref[...] += jnp.sum(dyf * xh, axis=0, keepdims=True)

    row = pl.BlockSpec((ts, d), lambda i: (i, 0))
    vec = pl.BlockSpec((1, d), lambda i: (0, 0))
    ins = [x, g, dy] + ([dres] if has_res else [])
    return pl.pallas_call(
        body, grid=(s // ts,), name=name,
        in_specs=[row, vec, row] + ([row] if has_res else []),
        out_specs=(row, vec),
        out_shape=(jax.ShapeDtypeStruct((s, d), out_dtype), jax.ShapeDtypeStruct((1, d), F32)),
        compiler_params=_params("arbitrary"),
    )(*ins)


def _loss_head(h, target, name):
    s, d = h.shape
    ts = _rms_rows(s)
    n_i = s // ts

    def body(h_ref, t_ref, dh_ref, loss_ref, acc_ref):
        i = pl.program_id(0)
        err = h_ref[...] - t_ref[...]
        dh_ref[...] = err * (1.0 / d)

        @pl.when(i == 0)
        def _():
            acc_ref[...] = jnp.zeros_like(acc_ref)

        acc_ref[...] += jnp.sum(err * err, axis=0, keepdims=True)

        @pl.when(i == n_i - 1)
        def _():
            loss_ref[...] = jnp.sum(acc_ref[...], axis=1, keepdims=True) * (0.5 / d)

    row = pl.BlockSpec((ts, d), lambda i: (i, 0))
    return pl.pallas_call(
        body, grid=(n_i,), name=name,
        in_specs=[row, row],
        out_specs=(row, pl.BlockSpec((1, 1), lambda i: (0, 0))),
        out_shape=(jax.ShapeDtypeStruct((s, d), F32), jax.ShapeDtypeStruct((1, 1), F32)),
        scratch_shapes=[pltpu.VMEM((1, d), F32)],
        compiler_params=_params("arbitrary"),
    )(h, target)


def _halo_specs(ns, s, ts, tc):
    hb = ts // HALO
    last = s // HALO - 1
    cur = pl.BlockSpec((ns, ts, tc), lambda j, i: (0, i, j))
    prev = pl.BlockSpec((ns, HALO, tc), lambda j, i: (0, jnp.maximum(i * hb - 1, 0), j))
    nxt = pl.BlockSpec((ns, HALO, tc), lambda j, i: (0, jnp.minimum((i + 1) * hb, last), j))
    return cur, prev, nxt


def _ext(prev, cur, nxt, first, last):
    p = jnp.where(first, 0.0, prev.astype(F32))
    n = jnp.where(last, 0.0, nxt.astype(F32))
    return jnp.concatenate([p, cur.astype(F32), n], axis=0)


def _dn(xe, k):
    return xe if k == 0 else pltpu.roll(xe, k, 0)


def _up(xe, k):
    return xe if k == 0 else pltpu.roll(xe, xe.shape[0] - k, 0)


def _conv_e(xe, w):
    return w[0:1, :] * _dn(xe, 2) + w[1:2, :] * _dn(xe, 1) + w[2:3, :] * xe


def _conv_t_e(de, w):
    return w[2:3, :] * de + w[1:2, :] * _up(de, 1) + w[0:1, :] * _up(de, 2)


def _conv_dw(de, xe, ts):
    return [jnp.sum((de * _dn(xe, 2 - k))[HALO:HALO + ts], axis=0, keepdims=True) for k in range(3)]


def _conv_tiles(s, c):
    return _tile(s, (512, 256, 128)), _tile(c, (512, 256, 128))


def _sigmoid(x):
    return pl.reciprocal(1.0 + jnp.exp(-x), approx=True)


def _sc_gate_fwd(p, w, name):
    _, s, c = p.shape
    ts, tc = _conv_tiles(s, c)
    n_i = s // ts

    def body(p_ref, pp_ref, w_ref, y_ref):
        i = pl.program_id(1)
        ze = _ext(pp_ref[1], p_ref[1], pp_ref[1], i == 0, True) * _ext(pp_ref[2], p_ref[2], pp_ref[2], i == 0, True)
        cz = _conv_e(ze, w_ref[...])[HALO:HALO + ts]
        y_ref[...] = (p_ref[0].astype(F32) * cz).astype(y_ref.dtype)

    cur, prev, _ = _halo_specs(3, s, ts, tc)
    return pl.pallas_call(
        body, grid=(c // tc, n_i), name=name,
        in_specs=[cur, prev, pl.BlockSpec((3, tc), lambda j, i: (0, j))],
        out_specs=pl.BlockSpec((ts, tc), lambda j, i: (i, j)),
        out_shape=jax.ShapeDtypeStruct((s, c), BF16),
        compiler_params=_params("parallel", "arbitrary"),
    )(p, p, w)


def _sc_gate_bwd(p, dy, w, name):
    _, s, c = p.shape
    ts, tc = _conv_tiles(s, c)
    n_i = s // ts

    def body(p_ref, pp_ref, pn_ref, dy_ref, dyn_ref, w_ref, dp_ref, dw_ref):
        i = pl.program_id(1)
        first, last = i == 0, i == n_i - 1
        wv = w_ref[...]
        gbe = _ext(pp_ref[0], p_ref[0], pn_ref[0], first, last)
        gce = _ext(pp_ref[1], p_ref[1], pn_ref[1], first, last)
        hhe = _ext(pp_ref[2], p_ref[2], pn_ref[2], first, last)
        dye = _ext(dyn_ref[...], dy_ref[...], dyn_ref[...], True, last)
        ze = gce * hhe
        cze = _conv_e(ze, wv)
        dcze = dye * gbe
        dze = _conv_t_e(dcze, wv)
        rows = slice(HALO, HALO + ts)
        dp_ref[0] = (dye * cze)[rows].astype(dp_ref.dtype)
        dp_ref[1] = (dze * hhe)[rows].astype(dp_ref.dtype)
        dp_ref[2] = (dze * gce)[rows].astype(dp_ref.dtype)

        @pl.when(first)
        def _():
            dw_ref[...] = jnp.zeros_like(dw_ref)

        dws = _conv_dw(dcze, ze, ts)
        for k in range(3):
            dw_ref[k:k + 1, :] += dws[k]

    cur, prev, nxt = _halo_specs(3, s, ts, tc)
    hb = ts // HALO
    last_h = s // HALO - 1
    return pl.pallas_call(
        body, grid=(c // tc, n_i), name=name,
        in_specs=[cur, prev, nxt,
                  pl.BlockSpec((ts, tc), lambda j, i: (i, j)),
                  pl.BlockSpec((HALO, tc), lambda j, i: (jnp.minimum((i + 1) * hb, last_h), j)),
                  pl.BlockSpec((3, tc), lambda j, i: (0, j))],
        out_specs=(cur, pl.BlockSpec((3, tc), lambda j, i: (0, j))),
        out_shape=(jax.ShapeDtypeStruct((3, s, c), BF16), jax.ShapeDtypeStruct((3, c), F32)),
        compiler_params=_params("parallel", "arbitrary"),
    )(p, p, p, dy, dy, w)


def _ffn_act_fwd(u, w, b, name):
    _, s, f = u.shape
    ts, tc = _conv_tiles(s, f)
    n_i = s // ts

    def body(u_ref, up_ref, w_ref, b_ref, a_ref):
        i = pl.program_id(1)
        rows = slice(HALO, HALO + ts)
        cg = _conv_e(_ext(up_ref[0], u_ref[0], up_ref[0], i == 0, True), w_ref[0])[rows] + b_ref[0]
        cv = _conv_e(_ext(up_ref[1], u_ref[1], up_ref[1], i == 0, True), w_ref[1])[rows] + b_ref[1]
        a_ref[...] = (cg * _sigmoid(cg) * cv).astype(a_ref.dtype)

    cur, prev, _ = _halo_specs(2, s, ts, tc)
    return pl.pallas_call(
        body, grid=(f // tc, n_i), name=name,
        in_specs=[cur, prev, pl.BlockSpec((2, 3, tc), lambda j, i: (0, 0, j)),
                  pl.BlockSpec((2, 1, tc), lambda j, i: (0, 0, j))],
        out_specs=pl.BlockSpec((ts, tc), lambda j, i: (i, j)),
        out_shape=jax.ShapeDtypeStruct((s, f), BF16),
        compiler_params=_params("parallel", "arbitrary"),
    )(u, u, w, b)


def _ffn_act_bwd(u, da, w, b, name):
    _, s, f = u.shape
    ts, tc = _conv_tiles(s, f)
    n_i = s // ts

    def body(u_ref, up_ref, un_ref, da_ref, dan_ref, w_ref, b_ref, du_ref, dw_ref, db_ref):
        i = pl.program_id(1)
        first, last = i == 0, i == n_i - 1
        rows = slice(HALO, HALO + ts)
        uge = _ext(up_ref[0], u_ref[0], un_ref[0], first, last)
        uve = _ext(up_ref[1], u_ref[1], un_ref[1], first, last)
        dae = _ext(dan_ref[...], da_ref[...], dan_ref[...], True, last)
        cg = _conv_e(uge, w_ref[0]) + b_ref[0]
        cv = _conv_e(uve, w_ref[1]) + b_ref[1]
        sg = _sigmoid(cg)
        dcg = dae * cv * (sg * (1.0 + cg * (1.0 - sg)))
        dcv = dae * (cg * sg)
        du_ref[0] = _conv_t_e(dcg, w_ref[0])[rows].astype(du_ref.dtype)
        du_ref[1] = _conv_t_e(dcv, w_ref[1])[rows].astype(du_ref.dtype)

        @pl.when(first)
        def _():
            dw_ref[...] = jnp.zeros_like(dw_ref)
            db_ref[...] = jnp.zeros_like(db_ref)

        for h, (de, xe) in enumerate(((dcg, uge), (dcv, uve))):
            dws = _conv_dw(de, xe, ts)
            for k in range(3):
                dw_ref[h, k:k + 1, :] += dws[k]
            db_ref[h] += jnp.sum(de[rows], axis=0, keepdims=True)

    cur, prev, nxt = _halo_specs(2, s, ts, tc)
    hb = ts // HALO
    last_h = s // HALO - 1
    wspec = pl.BlockSpec((2, 3, tc), lambda j, i: (0, 0, j))
    bspec = pl.BlockSpec((2, 1, tc), lambda j, i: (0, 0, j))
    return pl.pallas_call(
        body, grid=(f // tc, n_i), name=name,
        in_specs=[cur, prev, nxt,
                  pl.BlockSpec((ts, tc), lambda j, i: (i, j)),
                  pl.BlockSpec((HALO, tc), lambda j, i: (jnp.minimum((i + 1) * hb, last_h), j)),
                  wspec, bspec],
        out_specs=(cur, wspec, bspec),
        out_shape=(jax.ShapeDtypeStruct((2, s, f), BF16), jax.ShapeDtypeStruct((2, 3, f), F32),
                   jax.ShapeDtypeStruct((2, 1, f), F32)),
        compiler_params=_params("parallel", "arbitrary"),
    )(u, u, u, da, da, w, b)


_INV_SQRT2 = 0.7071067811865476
_INV_SQRT_2PI = 0.3989422804014327


def _gelu(x):
    return 0.5 * x * (1.0 + lax.erf(x * _INV_SQRT2))


def _gelu_grad(x):
    return 0.5 * (1.0 + lax.erf(x * _INV_SQRT2)) + x * (_INV_SQRT_2PI * jnp.exp(-0.5 * x * x))


def _tril_bf16(w):
    t = lax.broadcasted_iota(jnp.int32, w.shape, 0)
    s = lax.broadcasted_iota(jnp.int32, w.shape, 1)
    return jnp.where(s <= t, w, 0.0).astype(BF16)


def _sgu_fwd(p, ln_g, ln_b, ws, bs, name):
    _, s, c = p.shape
    g_n = ws.shape[0]
    cg = c // g_n

    def body(p_ref, lg_ref, lb_ref, ws_ref, bs_ref, y_ref):
        u = _gelu(p_ref[0].astype(F32))
        v = _gelu(p_ref[1].astype(F32))
        mu = jnp.mean(v, axis=-1, keepdims=True)
        vc = v - mu
        rstd = lax.rsqrt(jnp.mean(vc * vc, axis=-1, keepdims=True) + EPS)
        vn = ((vc * rstd) * lg_ref[...] + lb_ref[...]).astype(BF16)
        for g in range(g_n):
            cols = slice(g * cg, (g + 1) * cg)
            mixed = jnp.dot(_tril_bf16(ws_ref[g]), vn[:, cols], preferred_element_type=F32) + bs_ref[g]
            y_ref[:, cols] = (u[:, cols] * mixed).astype(y_ref.dtype)

    vec = pl.BlockSpec((1, c), lambda i: (0, 0))
    return pl.pallas_call(
        body, grid=(s // CHUNK,), name=name,
        in_specs=[pl.BlockSpec((2, CHUNK, c), lambda i: (0, i, 0)), vec, vec,
                  pl.BlockSpec((g_n, CHUNK, CHUNK), lambda i: (0, 0, 0)),
                  pl.BlockSpec((g_n, CHUNK, 1), lambda i: (0, 0, 0))],
        out_specs=pl.BlockSpec((CHUNK, c), lambda i: (i, 0)),
        out_shape=jax.ShapeDtypeStruct((s, c), BF16),
        compiler_params=_params("parallel"),
    )(p, ln_g, ln_b, ws, bs)


def _sgu_bwd(p, dy, ln_g, ln_b, ws, ws_t, bs, name):
    _, s, c = p.shape
    g_n = ws.shape[0]
    cg = c // g_n

    def body(p_ref, dy_ref, lg_ref, lb_ref, ws_ref, wst_ref, bs_ref, dp_ref, dlg_ref, dlb_ref, dws_ref, dbs_ref,
             dvn_ref):
        i = pl.program_id(0)
        pu = p_ref[0].astype(F32)
        pv = p_ref[1].astype(F32)
        dyf = dy_ref[...].astype(F32)
        u = _gelu(pu)
        v = _gelu(pv)
        mu = jnp.mean(v, axis=-1, keepdims=True)
        vc = v - mu
        rstd = lax.rsqrt(jnp.mean(vc * vc, axis=-1, keepdims=True) + EPS)
        vhat = vc * rstd
        vn = (vhat * lg_ref[...] + lb_ref[...]).astype(BF16)

        @pl.when(i == 0)
        def _():
            dlg_ref[...] = jnp.zeros_like(dlg_ref)
            dlb_ref[...] = jnp.zeros_like(dlb_ref)
            dws_ref[...] = jnp.zeros_like(dws_ref)
            dbs_ref[...] = jnp.zeros_like(dbs_ref)

        t_i = lax.broadcasted_iota(jnp.int32, (CHUNK, CHUNK), 0)
        s_i = lax.broadcasted_iota(jnp.int32, (CHUNK, CHUNK), 1)
        for g in range(g_n):
            cols = slice(g * cg, (g + 1) * cg)
            mixed = jnp.dot(_tril_bf16(ws_ref[g]), vn[:, cols], preferred_element_type=F32) + bs_ref[g]
            dp_ref[0, :, cols] = (dyf[:, cols] * mixed * _gelu_grad(pu[:, cols])).astype(dp_ref.dtype)
            dmixed = dyf[:, cols] * u[:, cols]
            dmb = dmixed.astype(BF16)
            wt = jnp.where(t_i <= s_i, wst_ref[g], 0.0).astype(BF16)
            dvn_ref[:, cols] = jnp.dot(wt, dmb, preferred_element_type=F32)
            dwg = lax.dot_general(dmb, vn[:, cols], (((1,), (1,)), ((), ())), preferred_element_type=F32)
            dws_ref[g] += jnp.where(s_i <= t_i, dwg, 0.0)
            dbs_ref[g] += jnp.sum(dmixed, axis=1, keepdims=True)
        dvn = dvn_ref[...]
        dlg_ref[...] += jnp.sum(dvn * vhat, axis=0, keepdims=True)
        dlb_ref[...] += jnp.sum(dvn, axis=0, keepdims=True)
        dvh = dvn * lg_ref[...]
        dv = rstd * (dvh - jnp.mean(dvh, axis=-1, keepdims=True) - vhat * jnp.mean(dvh * vhat, axis=-1, keepdims=True))
        dp_ref[1] = (dv * _gelu_grad(pv)).astype(dp_ref.dtype)

    vec = pl.BlockSpec((1, c), lambda i: (0, 0))
    wspec = pl.BlockSpec((g_n, CHUNK, CHUNK), lambda i: (0, 0, 0))
    bspec = pl.BlockSpec((g_n, CHUNK, 1), lambda i: (0, 0, 0))
    pspec = pl.BlockSpec((2, CHUNK, c), lambda i: (0, i, 0))
    return pl.pallas_call(
        body, grid=(s // CHUNK,), name=name,
        in_specs=[pspec, pl.BlockSpec((CHUNK, c), lambda i: (i, 0)), vec, vec, wspec, wspec, bspec],
        out_specs=(pspec, vec, vec, wspec, bspec),
        out_shape=(jax.ShapeDtypeStruct((2, s, c), BF16), jax.ShapeDtypeStruct((1, c), F32),
                   jax.ShapeDtypeStruct((1, c), F32), jax.ShapeDtypeStruct((g_n, CHUNK, CHUNK), F32),
                   jax.ShapeDtypeStruct((g_n, CHUNK, 1), F32)),
        scratch_shapes=[pltpu.VMEM((CHUNK, c), F32)],
        compiler_params=_params("arbitrary"),
    )(p, dy, ln_g, ln_b, ws, ws_t, bs)


SB_Q = 512


def _sb_cumsum(x, tri, edge, reverse):
    n = x.shape[0]
    hi = x.astype(BF16)
    lo = (x - hi.astype(F32)).astype(BF16)
    chunks = x.shape[1] // CHUNK
    outs = [None] * chunks
    for c in (reversed(range(chunks)) if reverse else range(chunks)):
        cols = slice(c * CHUNK, (c + 1) * CHUNK)
        both = jnp.dot(jnp.concatenate([hi[:, cols], lo[:, cols]], axis=0), tri, preferred_element_type=F32)
        outs[c] = both[:n] + both[n:] + edge
        edge = edge + jnp.sum(x[:, cols], axis=1, keepdims=True)
    return jnp.concatenate(outs, axis=1), edge


def _sb_block(q, k, offset):
    scale = HEAD_DIM ** -0.5
    z = lax.dot_general(q, k, (((1,), (1,)), ((), ())), preferred_element_type=F32) * scale
    e = jnp.exp(-jnp.abs(z))
    lp = jnp.log(1.0 + e)
    lb = jnp.minimum(z, 0.0) - lp
    l1 = lb - z
    if offset is not None:
        t_i = lax.broadcasted_iota(jnp.int32, z.shape, 0)
        s_i = lax.broadcasted_iota(jnp.int32, z.shape, 1)
        mask = (t_i - s_i) > offset
        l1 = jnp.where(mask, l1, 0.0)
    else:
        mask = None
    return z, e, lb, l1, mask


def _tri(cond):
    j = lax.broadcasted_iota(jnp.int32, (CHUNK, CHUNK), 0)
    s = lax.broadcasted_iota(jnp.int32, (CHUNK, CHUNK), 1)
    return jnp.where(cond(j, s), 1.0, 0.0).astype(BF16)


def _sb_fwd(qkv, name):
    _, s, c = qkv.shape
    heads = c // HEAD_DIM
    bq = _tile(s, (SB_Q, 256, CHUNK))
    nq = s // bq

    def body(q_ref, k_ref, v_ref, o_ref, runs_ref):
        i = pl.program_id(1)
        q = q_ref[...]
        after = _tri(lambda j, s_: j > s_)
        runs_ref[...] = jnp.zeros_like(runs_ref)

        def tile(g, carry, offset):
            o_acc, run = carry
            rows = pl.ds(pl.multiple_of(g * bq, bq), bq)
            runs_ref[pl.ds(pl.multiple_of(g * 8, 8), 8), :] = jnp.transpose(jnp.broadcast_to(run, (bq, LANES)))[:8]
            _, _, lb, l1, mask = _sb_block(q, k_ref[rows, :], offset)
            acc, run = _sb_cumsum(l1, after, run, True)
            a = jnp.exp(lb + acc)
            if mask is not None:
                a = jnp.where(mask, a, 0.0)
            o_acc = o_acc + jnp.dot(a.astype(BF16), v_ref[rows, :], preferred_element_type=F32)
            return o_acc, run

        carry = tile(i, (jnp.zeros((bq, HEAD_DIM), F32), jnp.zeros((bq, 1), F32)), 0)
        o_acc, _ = lax.fori_loop(0, i, lambda n, cr: tile(i - 1 - n, cr, None), carry)
        o_ref[...] = o_acc.astype(o_ref.dtype)

    return pl.pallas_call(
        body, grid=(heads, nq), name=name,
        in_specs=[pl.BlockSpec((None, bq, HEAD_DIM), lambda h, i: (0, i, h)),
                  pl.BlockSpec((None, s, HEAD_DIM), lambda h, i: (1, 0, h)),
                  pl.BlockSpec((None, s, HEAD_DIM), lambda h, i: (2, 0, h))],
        out_specs=(pl.BlockSpec((bq, HEAD_DIM), lambda h, i: (i, h)),
                   pl.BlockSpec((None, None, 8 * nq, bq), lambda h, i: (h, i, 0, 0))),
        out_shape=(jax.ShapeDtypeStruct((s, c), BF16), jax.ShapeDtypeStruct((heads, nq, 8 * nq, bq), F32)),
        compiler_params=_params("parallel", "arbitrary"),
    )(qkv, qkv, qkv)


def _sb_bwd(qkv, do, runs, name):
    _, s, c = qkv.shape
    heads = c // HEAD_DIM
    bq = _tile(s, (SB_Q, 256, CHUNK))
    nq = s // bq
    scale = HEAD_DIM ** -0.5
    tn_dims = (((0,), (0,)), ((), ()))

    def body(q_ref, k_ref, v_ref, do_ref, runs_ref, dq_ref, dkv_ref, dk_acc, dv_acc):
        i = pl.program_id(1)
        q = q_ref[...]
        dob = do_ref[...]
        after = _tri(lambda j, s_: j > s_)
        before = _tri(lambda j, s_: j < s_)

        @pl.when(i == 0)
        def _():
            dk_acc[...] = jnp.zeros_like(dk_acc)
            dv_acc[...] = jnp.zeros_like(dv_acc)

        def main(g, carry, offset):
            dq_acc, csum = carry
            rows = pl.ds(pl.multiple_of(g * bq, bq), bq)
            kblk = k_ref[rows, :]
            vblk = v_ref[rows, :]
            z, e, lb, l1, mask = _sb_block(q, kblk, offset)
            run = jnp.transpose(jnp.broadcast_to(runs_ref[pl.ds(pl.multiple_of(g * 8, 8), 1), :], (LANES, bq)))
            acc, _ = _sb_cumsum(l1, after, run, True)
            a = jnp.exp(lb + acc)
            if mask is not None:
                a = jnp.where(mask, a, 0.0)
            d_a = lax.dot_general(dob, vblk, (((1,), (1,)), ((), ())), preferred_element_type=F32)
            ee = a * d_a
            cc, csum = _sb_cumsum(ee, before, csum, False)
            beta = jnp.exp(lb)
            dz = (ee - (ee + cc) * beta) * scale
            if mask is not None:
                dz = jnp.where(mask, dz, 0.0)
            dzb = dz.astype(BF16)
            dq_acc = dq_acc + jnp.dot(dzb, kblk, preferred_element_type=F32)
            dk_acc[rows, :] += lax.dot_general(dzb, q, tn_dims, preferred_element_type=F32)
            dv_acc[rows, :] += lax.dot_general(a.astype(BF16), dob, tn_dims, preferred_element_type=F32)
            return dq_acc, csum

        carry = (jnp.zeros((bq, HEAD_DIM), F32), jnp.zeros((bq, 1), F32))
        carry = lax.fori_loop(0, i, lambda g, cr: main(g, cr, None), carry)
        dq_acc, _ = main(i, carry, 0)
        dq_ref[...] = dq_acc.astype(dq_ref.dtype)

        @pl.when(i == nq - 1)
        def _():
            dkv_ref[0] = dk_acc[...].astype(dkv_ref.dtype)
            dkv_ref[1] = dv_acc[...].astype(dkv_ref.dtype)

    blk = pl.BlockSpec((bq, HEAD_DIM), lambda h, i: (i, h))
    return pl.pallas_call(
        body, grid=(heads, nq), name=name,
        in_specs=[pl.BlockSpec((None, bq, HEAD_DIM), lambda h, i: (0, i, h)),
                  pl.BlockSpec((None, s, HEAD_DIM), lambda h, i: (1, 0, h)),
                  pl.BlockSpec((None, s, HEAD_DIM), lambda h, i: (2, 0, h)),
                  blk,
                  pl.BlockSpec((None, None, 8 * nq, bq), lambda h, i: (h, i, 0, 0))],
        out_specs=(blk, pl.BlockSpec((2, s, HEAD_DIM), lambda h, i: (0, 0, h))),
        out_shape=(jax.ShapeDtypeStruct((s, c), BF16), jax.ShapeDtypeStruct((2, s, c), BF16)),
        scratch_shapes=[pltpu.VMEM((s, HEAD_DIM), F32), pltpu.VMEM((s, HEAD_DIM), F32)],
        compiler_params=_params("parallel", "arbitrary"),
    )(qkv, qkv, qkv, do, runs)


def _all_gather(x, axis, name):
    return _run_jobs([_GatherJob(x, axis)], name)[0]


def _rs_core_exchange(g, axis, name):
    r, c = (g.shape[0] // N_DEV, g.shape[1]) if axis == 0 else (g.shape[0], g.shape[1] // N_DEV)

    def body(g_ref, recv_ref, send_sems, recv_sems):
        mx, my, mc = lax.axis_index("x"), lax.axis_index("y"), lax.axis_index("c")
        copies = [pltpu.make_async_remote_copy(
            src_ref=_shard_of(g_ref, 2 * k + (1 - mc), axis, r, c), dst_ref=recv_ref.at[k],
            send_sem=send_sems.at[k], recv_sem=recv_sems.at[k], device_id=(mx, my, 1 - mc), device_id_type=MESH)
            for k in range(4)]
        for cp in copies:
            cp.start()
        for cp in copies:
            cp.wait()

    return pl.pallas_call(
        body, name=name,
        out_shape=jax.ShapeDtypeStruct((4, r, c), g.dtype),
        in_specs=[ANY], out_specs=ANY,
        scratch_shapes=[pltpu.SemaphoreType.DMA((4,)), pltpu.SemaphoreType.DMA((4,))],
    )(g)


def _ew_tiles(r, c):
    tc = c if c <= 2048 else _tile(c, (2048, 1024, 512, 256, 128))
    return _tile(r, [p for p in (512, 256, 128, 64, 32, 16, 8) if p * tc <= 256 * 1024]), tc


def _rs_core_add(g, recv, core, axis, name):
    _, r, c = recv.shape
    tr, tc = _ew_tiles(r, c)
    nrb, ncb = r // tr, c // tc

    def body(core_ref, g_ref, x_ref, o_ref):
        del core_ref
        o_ref[...] = (g_ref[...].astype(F32) + x_ref[...].astype(F32)).astype(o_ref.dtype)

    if axis == 0:
        g_spec = pl.BlockSpec((tr, tc), lambda k, i, j, core: ((2 * k + core[0]) * nrb + i, j))
    else:
        g_spec = pl.BlockSpec((tr, tc), lambda k, i, j, core: (i, (2 * k + core[0]) * ncb + j))
    blk = pl.BlockSpec((None, tr, tc), lambda k, i, j, core: (k, i, j))
    return pl.pallas_call(
        body, name=name,
        grid_spec=pltpu.PrefetchScalarGridSpec(
            num_scalar_prefetch=1, grid=(4, nrb, ncb), in_specs=[g_spec, blk], out_specs=blk),
        out_shape=jax.ShapeDtypeStruct(recv.shape, BF16),
        compiler_params=_params("parallel", "parallel", "parallel"),
    )(core, g, recv)


def _rs_chip_exchange(parts, name):
    return _run_jobs([_ChipExchangeJob(p) for p in parts], name)


def _adamw_math(w, g, m, v):
    m = ADAM_B1 * m + (1.0 - ADAM_B1) * g
    v = ADAM_B2 * v + (1.0 - ADAM_B2) * (g * g)
    m_hat = m / (1.0 - ADAM_B1 ** ADAM_STEP)
    v_hat = v / (1.0 - ADAM_B2 ** ADAM_STEP)
    delta = -ADAM_LR * (m_hat / (jnp.sqrt(v_hat) + ADAM_EPS) + ADAM_WD * w)
    return delta, m, v


def _adamw_shard(w, m, v, layer, p, recv, chip, name):
    _, r, c = w.shape
    tr, tc = _ew_tiles(r, c)

    def body(chip_ref, w_ref, m_ref, v_ref, p_ref, x_ref, g_out, d_out, m_out, v_out):
        del chip_ref
        g = p_ref[...].astype(F32)
        for j in range(3):
            g = g + x_ref[j].astype(F32)
        d, mn, vn = _adamw_math(w_ref[...], g, m_ref[...], v_ref[...])
        g_out[...] = g
        d_out[...] = d
        m_out[...] = mn
        v_out[...] = vn

    blk = pl.BlockSpec((tr, tc), lambda i, j, chip: (i, j))
    lay = pl.BlockSpec((None, tr, tc), lambda i, j, chip: (layer, i, j))
    out = jax.ShapeDtypeStruct((r, c), F32)
    return pl.pallas_call(
        body, name=name,
        grid_spec=pltpu.PrefetchScalarGridSpec(
            num_scalar_prefetch=1, grid=(r // tr, c // tc),
            in_specs=[lay, lay, lay, pl.BlockSpec((None, tr, tc), lambda i, j, chip: (chip[0], i, j)),
                      pl.BlockSpec((3, tr, tc), lambda i, j, chip: (0, i, j))],
            out_specs=(blk, blk, blk, blk)),
        out_shape=(out, out, out, out),
        compiler_params=_params("parallel", "parallel"),
    )(chip, w, m, v, p, recv)


def _adamw_packed(w, m, v, g, name):
    r, c = w.shape
    tr = _tile(r, (512, 256, 128, 64, 32, 16, 8))

    def body(w_ref, m_ref, v_ref, g_ref, d_out, m_out, v_out):
        d, mn, vn = _adamw_math(w_ref[...], g_ref[...], m_ref[...], v_ref[...])
        d_out[...] = d
        m_out[...] = mn
        v_out[...] = vn

    blk = pl.BlockSpec((tr, c), lambda i: (i, 0))
    out = jax.ShapeDtypeStruct((r, c), F32)
    return pl.pallas_call(
        body, grid=(r // tr,), name=name, in_specs=[blk, blk, blk, blk], out_specs=(blk, blk, blk),
        out_shape=(out, out, out), compiler_params=_params("parallel"),
    )(w, m, v, g)


def _sum_devices(x, name):
    r = x.shape[0] // N_DEV
    tr = _tile(r, (512, 256, 128, 64, 32, 16, 8))

    def body(x_ref, o_ref):
        acc = x_ref[0]
        for d in range(1, N_DEV):
            acc = acc + x_ref[d]
        o_ref[...] = acc

    return pl.pallas_call(
        body, grid=(r // tr,), name=name,
        in_specs=[pl.BlockSpec((N_DEV, tr, LANES), lambda i: (0, i, 0))],
        out_specs=pl.BlockSpec((tr, LANES), lambda i: (i, 0)),
        out_shape=jax.ShapeDtypeStruct((r, LANES), F32),
        compiler_params=_params("parallel"),
    )(x.reshape(N_DEV, r, LANES))


def _pack(arrays):
    flat = []
    for a in arrays:
        f = a.reshape(-1).astype(F32)
        flat.append(jnp.pad(f, (0, (-f.shape[0]) % LANES)))
    f = jnp.concatenate(flat)
    f = jnp.pad(f, (0, (-f.shape[0]) % (8 * LANES)))
    return f.reshape(-1, LANES)


def _unpack(packed, shapes):
    flat = packed.reshape(-1)
    out, off = [], 0
    for shp in shapes:
        n = 1
        for d in shp:
            n *= d
        out.append(flat[off:off + n].reshape(shp))
        off += n + (-n) % LANES
    return out


def kernel(x, norm_mix_pre, norm_mix_post, norm_ffn_pre, norm_ffn_post, sc_w_in, sc_conv_w, sc_w_out, sg_w_in, sg_ln_g, sg_ln_b, sg_w_s, sg_b_s, sg_w_out, sb_w_qkv, sb_w_out, ffn_w_up, ffn_conv_w, ffn_conv_b, ffn_w_down, loss_target, m_norm_mix_pre, m_norm_mix_post, m_norm_ffn_pre, m_norm_ffn_post, m_sc_w_in, m_sc_conv_w, m_sc_w_out, m_sg_w_in, m_sg_ln_g, m_sg_ln_b, m_sg_w_s, m_sg_b_s, m_sg_w_out, m_sb_w_qkv, m_sb_w_out, m_ffn_w_up, m_ffn_conv_w, m_ffn_conv_b, m_ffn_w_down, v_norm_mix_pre, v_norm_mix_post, v_norm_ffn_pre, v_norm_ffn_post, v_sc_w_in, v_sc_conv_w, v_sc_w_out, v_sg_w_in, v_sg_ln_g, v_sg_ln_b, v_sg_w_s, v_sg_b_s, v_sg_w_out, v_sb_w_qkv, v_sb_w_out, v_ffn_w_up, v_ffn_conv_w, v_ffn_conv_b, v_ffn_w_down):
    weights = dict(norm_mix_pre=norm_mix_pre, norm_mix_post=norm_mix_post, norm_ffn_pre=norm_ffn_pre,
                   norm_ffn_post=norm_ffn_post, sc_w_in=sc_w_in, sc_conv_w=sc_conv_w, sc_w_out=sc_w_out,
                   sg_w_in=sg_w_in, sg_ln_g=sg_ln_g, sg_ln_b=sg_ln_b, sg_w_s=sg_w_s, sg_b_s=sg_b_s,
                   sg_w_out=sg_w_out, sb_w_qkv=sb_w_qkv, sb_w_out=sb_w_out, ffn_w_up=ffn_w_up,
                   ffn_conv_w=ffn_conv_w, ffn_conv_b=ffn_conv_b, ffn_w_down=ffn_w_down)
    mom1 = dict(norm_mix_pre=m_norm_mix_pre, norm_mix_post=m_norm_mix_post, norm_ffn_pre=m_norm_ffn_pre,
                norm_ffn_post=m_norm_ffn_post, sc_w_in=m_sc_w_in, sc_conv_w=m_sc_conv_w, sc_w_out=m_sc_w_out,
                sg_w_in=m_sg_w_in, sg_ln_g=m_sg_ln_g, sg_ln_b=m_sg_ln_b, sg_w_s=m_sg_w_s, sg_b_s=m_sg_b_s,
                sg_w_out=m_sg_w_out, sb_w_qkv=m_sb_w_qkv, sb_w_out=m_sb_w_out, ffn_w_up=m_ffn_w_up,
                ffn_conv_w=m_ffn_conv_w, ffn_conv_b=m_ffn_conv_b, ffn_w_down=m_ffn_w_down)
    mom2 = dict(norm_mix_pre=v_norm_mix_pre, norm_mix_post=v_norm_mix_post, norm_ffn_pre=v_norm_ffn_pre,
                norm_ffn_post=v_norm_ffn_post, sc_w_in=v_sc_w_in, sc_conv_w=v_sc_conv_w, sc_w_out=v_sc_w_out,
                sg_w_in=v_sg_w_in, sg_ln_g=v_sg_ln_g, sg_ln_b=v_sg_ln_b, sg_w_s=v_sg_w_s, sg_b_s=v_sg_b_s,
                sg_w_out=v_sg_w_out, sb_w_qkv=v_sb_w_qkv, sb_w_out=v_sb_w_out, ffn_w_up=v_ffn_w_up,
                ffn_conv_w=v_ffn_conv_w, ffn_conv_b=v_ffn_conv_b, ffn_w_down=v_ffn_w_down)
    order = list(weights)

    mx, my, mc = lax.axis_index("x"), lax.axis_index("y"), lax.axis_index("c")
    dev = 4 * mx + 2 * my + mc
    core_arr = jnp.reshape(mc, (1,)).astype(jnp.int32)
    chip_arr = jnp.reshape(2 * mx + my, (1,)).astype(jnp.int32)

    h = x[0]
    target = loss_target[0]
    depth = norm_mix_pre.shape[0]
    d_model = h.shape[1]
    f_loc = ffn_w_down.shape[1]
    d_ff = f_loc * N_DEV

    def gather_rows8(a2d, name):
        rows = a2d.shape[0]
        pad = (-rows) % 8
        return _all_gather(jnp.pad(a2d, ((0, pad), (0, 0))), 1, name)[:rows]

    n_a = sc_w_in.shape[0]
    sc_conv_full = gather_rows8(sc_conv_w.reshape(n_a * 3, -1), "ag_sc_conv").reshape(n_a, 3, d_model)
    ffn_conv_full = gather_rows8(ffn_conv_w.reshape(depth * 3, -1), "ag_ffn_conv").reshape(depth, 3, 2 * d_ff)

    big = {}

    kinds = [i % 3 for i in range(depth)]
    mixer_names = [("sc_w_in", "sc_w_out"), ("sg_w_in", "sg_w_out"), ("sb_w_qkv", "sb_w_out")]
    mixer_split = [3, 2, 3]
    mixer_proj = ["sc_in_proj", "sg_in_proj", "sb_qkv_proj"]

    def layer_keys(i):
        wi, wo = mixer_names[kinds[i]]
        return (wi, i // 3, 1), (wo, i // 3, 0), ("ffn_w_up", i, 1), ("ffn_w_down", i, 0)

    def gather_job(key):
        name, j, axis = key
        return _GatherJob(weights[name][j].astype(BF16), axis)

    def hosted_mm(a, key, out_dtype, nsplit, name, gather_keys):
        keys = [k for k in gather_keys if k is not None]
        res = _mm_nn(a, big[key[:2]], out_dtype, nsplit, name, jobs=[gather_job(k) for k in keys])
        if not keys:
            return res
        for k, full in zip(keys, res[1:]):
            big[k[:2]] = full
        return res[0]

    sg_ws_t = jnp.swapaxes(sg_w_s, -1, -2)
    sg_bs3 = sg_b_s[..., None]

    k_in0 = layer_keys(0)[0]
    big[k_in0[:2]] = _all_gather(weights[k_in0[0]][k_in0[1]].astype(BF16), k_in0[2], f"ag_{k_in0[0]}")
    saved = []
    for i in range(depth):
        j = i // 3
        k_in, k_out, k_up, k_down = layer_keys(i)
        nxt = layer_keys(i + 1) if i + 1 < depth else (None, None, None, None)
        sv = {"h_in": h}
        hn = _rms_fwd(h, norm_mix_pre[i:i + 1], "rms_fwd")
        sv["hn"] = hn
        p = hosted_mm(hn, k_in, BF16, mixer_split[kinds[i]], mixer_proj[kinds[i]],
                      [k_out, k_up] if i == 0 else [])
        if kinds[i] == 0:
            y = _sc_gate_fwd(p, sc_conv_full[j], "sc_gate_fwd")
        elif kinds[i] == 1:
            y = _sgu_fwd(p, sg_ln_g[j:j + 1], sg_ln_b[j:j + 1], sg_w_s[j], sg_bs3[j], "sgu_fwd")
        else:
            y, sv["runs"] = _sb_fwd(p, "sb_fwd")
        m = hosted_mm(y, k_out, F32, 1, "mix_out_proj", [])
        sv.update(p=p, y=y, m=m)
        h = _resid_rms_fwd(h, m, norm_mix_post[i:i + 1], "resid_rms_fwd")
        sv["h_mid"] = h
        hn2 = _rms_fwd(h, norm_ffn_pre[i:i + 1], "rms_fwd")
        u = hosted_mm(hn2, k_up, BF16, 2, "ffn_up_proj", [k_down, nxt[0], nxt[1]])
        cw = ffn_conv_full[i].reshape(3, 2, d_ff).transpose(1, 0, 2)
        cb = ffn_conv_b[i].reshape(2, 1, d_ff)
        a = _ffn_act_fwd(u, cw, cb, "ffn_act_fwd")
        f = hosted_mm(a, k_down, F32, 1, "ffn_down_proj", [nxt[2]])
        sv.update(hn2=hn2, u=u, a=a, f=f, cw=cw, cb=cb)
        h = _resid_rms_fwd(h, f, norm_ffn_post[i:i + 1], "resid_rms_fwd")
        saved.append(sv)

    dh, loss_part = _loss_head(h, target, "loss_head")

    grads_big = {}
    small = {}

    parts = {}
    queue = []

    def reduce_scatter(name, j, g_full, axis):
        recv1 = _rs_core_exchange(g_full, axis, f"rs1_{name}")
        parts[(name, j)] = _rs_core_add(g_full, recv1, core_arr, axis, f"rs_add_{name}")
        queue.append((name, j))

    def hosting(mm, a, b, out_dtype, name, n_jobs):
        keys = [queue.pop(0) for _ in range(min(n_jobs, len(queue)))]
        res = mm(a, b, out_dtype, name, jobs=[_ChipExchangeJob(parts[k]) for k in keys])
        if not keys:
            return res
        for k, recv in zip(keys, res[1:]):
            grads_big[k] = (parts[k], recv)
        return res[0]

    for i in reversed(range(depth)):
        j = i // 3
        sv = saved[i]
        df, dg = _rms_bwd(sv["f"], norm_ffn_post[i:i + 1], dh, None, BF16, "rms_bwd_post")
        small.setdefault("norm_ffn_post", {})[i] = dg
        da = hosting(_mm_nt, df, big[("ffn_w_down", i)], BF16, "ffn_down_dx", 1)
        reduce_scatter("ffn_w_down", i, hosting(_mm_tn, sv["a"], df, BF16, "ffn_down_dw", 2), 0)
        du, dcw, dcb = _ffn_act_bwd(sv["u"], da, sv["cw"], sv["cb"], "ffn_act_bwd")
        small.setdefault("ffn_conv_w", {})[i] = dcw.transpose(1, 0, 2).reshape(3, 2 * d_ff)
        small.setdefault("ffn_conv_b", {})[i] = dcb.reshape(2 * d_ff)
        dhn2 = _mm_nt(du, big[("ffn_w_up", i)], F32, "ffn_up_dx")
        reduce_scatter("ffn_w_up", i, _mm_tn(sv["hn2"], du, BF16, "ffn_up_dw"), 1)
        dh, dg = _rms_bwd(sv["h_mid"], norm_ffn_pre[i:i + 1], dhn2, dh, F32, "rms_bwd_pre")
        small.setdefault("norm_ffn_pre", {})[i] = dg
        dm, dg = _rms_bwd(sv["m"], norm_mix_post[i:i + 1], dh, None, BF16, "rms_bwd_post")
        small.setdefault("norm_mix_post", {})[i] = dg
        if kinds[i] == 0:
            wo, wi = "sc_w_out", "sc_w_in"
        elif kinds[i] == 1:
            wo, wi = "sg_w_out", "sg_w_in"
        else:
            wo, wi = "sb_w_out", "sb_w_qkv"
        dy = hosting(_mm_nt, dm, big[(wo, j)], BF16, "mix_out_dx", 1)
        reduce_scatter(wo, j, _mm_tn(sv["y"], dm, BF16, "mix_out_dw"), 0)
        if kinds[i] == 0:
            dp, dcw = _sc_gate_bwd(sv["p"], dy, sc_conv_full[j], "sc_gate_bwd")
            small.setdefault("sc_conv_w", {})[j] = dcw
        elif kinds[i] == 1:
            dp, dlg, dlb, dws, dbs = _sgu_bwd(sv["p"], dy, sg_ln_g[j:j + 1], sg_ln_b[j:j + 1], sg_w_s[j], sg_ws_t[j],
                                              sg_bs3[j], "sgu_bwd")
            small.setdefault("sg_ln_g", {})[j] = dlg[0]
            small.setdefault("sg_ln_b", {})[j] = dlb[0]
            small.setdefault("sg_w_s", {})[j] = dws
            small.setdefault("sg_b_s", {})[j] = dbs[..., 0]
        else:
            dq, dkv = _sb_bwd(sv["p"], dy, sv["runs"], "sb_bwd")
            dp = jnp.concatenate([dq[None], dkv], axis=0)
        dhn = hosting(_mm_nt, dp, big[(wi, j)], F32, "mix_in_dx", 2)
        reduce_scatter(wi, j, _mm_tn(sv["hn"], dp, BF16, "mix_in_dw"), 1)
        dh, dg = _rms_bwd(sv["h_in"], norm_mix_pre[i:i + 1], dhn, dh, F32, "rms_bwd_pre")
        small.setdefault("norm_mix_pre", {})[i] = dg

    grad_x = dh[None]
    for k, recv in zip(queue, _rs_chip_exchange([parts[k] for k in queue], "rs2_tail")):
        grads_big[k] = (parts[k], recv)

    small_names = ["norm_mix_pre", "norm_mix_post", "norm_ffn_pre", "norm_ffn_post", "sg_ln_g", "sg_ln_b", "sg_w_s",
                   "sg_b_s", "ffn_conv_b", "sc_conv_w", "ffn_conv_w"]
    full_shapes = {n: weights[n].shape for n in small_names}
    full_shapes["sc_conv_w"] = (n_a, 3, d_model)
    full_shapes["ffn_conv_w"] = (depth, 3, 2 * d_ff)
    partial = [jnp.stack([small[n][k].reshape(full_shapes[n][1:]) for k in sorted(small[n])]) for n in small_names]
    packed = _pack(partial + [loss_part])
    gathered = _all_gather(packed, 0, "ag_small_grads")
    summed = _sum_devices(gathered, "sum_small_grads")
    pieces = _unpack(summed, [full_shapes[n] for n in small_names] + [(1, 1)])
    loss = pieces[-1].reshape(())
    small_grads = dict(zip(small_names, pieces[:-1]))
    for n in ("sc_conv_w", "ffn_conv_w"):
        c_loc = weights[n].shape[-1]
        small_grads[n] = lax.dynamic_slice_in_dim(small_grads[n], dev * c_loc, c_loc, axis=2)
    shapes = [weights[n].shape for n in small_names]
    d_p, m_p, v_p = _adamw_packed(_pack([weights[n] for n in small_names]), _pack([mom1[n] for n in small_names]),
                                  _pack([mom2[n] for n in small_names]), _pack([small_grads[n] for n in small_names]),
                                  "adamw_small")
    out_g, out_d, out_m, out_v = dict(small_grads), {}, {}, {}
    for n, d_, m_, v_ in zip(small_names, _unpack(d_p, shapes), _unpack(m_p, shapes), _unpack(v_p, shapes)):
        out_d[n], out_m[n], out_v[n] = d_, m_, v_

    for n in order:
        if n in small_names:
            continue
        res = [_adamw_shard(weights[n], mom1[n], mom2[n], j, *grads_big[(n, j)], chip_arr, f"adamw_{n}")
               for j in range(weights[n].shape[0])]
        out_g[n], out_d[n], out_m[n], out_v[n] = (jnp.stack([r_[t] for r_ in res]) for t in range(4))

    return (loss, grad_x, *[out_g[n] for n in order], *[out_d[n] for n in order],
            *[out_m[n] for n in order], *[out_v[n] for n in order])
```

```python
import functools

import jax
import jax.numpy as jnp
from jax import lax
from jax.experimental import pallas as pl
from jax.experimental.pallas import tpu as pltpu

F32 = jnp.float32
BF16 = jnp.bfloat16

EPS = 1e-6
CHUNK = 128
HEAD_DIM = 128
SG_GROUPS = 8
HALO = 16
N_DEV = 8
LANES = 128
V7X_VMEM_LIMIT = 56 * 1024 * 1024

ADAM_LR = 0.001
ADAM_B1 = 0.9
ADAM_B2 = 0.999
ADAM_EPS = 1e-08
ADAM_WD = 0.01
ADAM_STEP = 10

MESH = pl.DeviceIdType.MESH
ANY = pl.BlockSpec(memory_space=pl.ANY)


def _tile(n, prefs):
    for p in prefs:
        if p <= n and n % p == 0:
            return p
    return n


def _params(*sem):
    return pltpu.CompilerParams(dimension_semantics=sem, vmem_limit_bytes=V7X_VMEM_LIMIT)


def _shard_of(ref, dev, axis, r, c):
    if axis == 0:
        return ref.at[pl.ds(dev * r, r), :]
    return ref.at[:, pl.ds(dev * c, c)]


class _GatherJob:
    def __init__(self, x, axis):
        self.axis = axis
        self.r, self.c = x.shape
        full = (N_DEV * self.r, self.c) if axis == 0 else (self.r, N_DEV * self.c)
        self.inputs = [x]
        self.out_shapes = [jax.ShapeDtypeStruct(full, x.dtype)]
        self.scratch = [pltpu.SemaphoreType.DMA((7,)), pltpu.SemaphoreType.DMA((7,)), pltpu.SemaphoreType.DMA(())]

    def _plan(self, ins, outs, scr):
        (x_ref,), (out_ref,), (send_sems, recv_sems, local_sem) = ins, outs, scr
        mx, my, mc = lax.axis_index("x"), lax.axis_index("y"), lax.axis_index("c")
        me, sibling = (mx, my, mc), (mx, my, 1 - mc)
        chips = [(1 - mx, my), (mx, 1 - my), (1 - mx, 1 - my)]

        def rows(px, py, pc):
            return _shard_of(out_ref, 4 * px + 2 * py + pc, self.axis, self.r, self.c)

        def copy(k, block, to, src=None):
            return pltpu.make_async_remote_copy(
                src_ref=rows(*block) if src is None else src, dst_ref=rows(*block),
                send_sem=send_sems.at[k], recv_sem=recv_sems.at[k], device_id=to, device_id_type=MESH)

        mine = pltpu.make_async_copy(x_ref, rows(*me), local_sem)
        first = [copy(0, me, sibling, src=x_ref)] + [copy(1 + j, me, (*chip, mc), src=x_ref)
                                                      for j, chip in enumerate(chips)]
        passed = [copy(4 + j, (*chip, mc), sibling) for j, chip in enumerate(chips)]
        landed = [copy(1 + j, (*chip, mc), me) for j, chip in enumerate(chips)]
        from_sibling = [copy(0, sibling, me)] + [copy(4 + j, (*chip, 1 - mc), me) for j, chip in enumerate(chips)]
        return mine, first, passed, landed, from_sibling

    def start(self, ins, outs, scr):
        mine, first, _, _, _ = self._plan(ins, outs, scr)
        mine.start()
        for cp in first:
            cp.start()

    def finish(self, ins, outs, scr):
        mine, first, passed, landed, from_sibling = self._plan(ins, outs, scr)
        for j in range(3):
            landed[j].wait_recv()
            passed[j].start()
        for cp in from_sibling:
            cp.wait_recv()
        for cp in first + passed:
            cp.wait_send()
        mine.wait()


class _ChipExchangeJob:
    def __init__(self, p):
        _, r, c = p.shape
        self.inputs = [p]
        self.out_shapes = [jax.ShapeDtypeStruct((3, r, c), p.dtype)]
        self.scratch = [pltpu.SemaphoreType.DMA((3,)), pltpu.SemaphoreType.DMA((3,))]

    def _plan(self, ins, outs, scr):
        (p_ref,), (recv_ref,), (send_sems, recv_sems) = ins, outs, scr
        mx, my, mc = lax.axis_index("x"), lax.axis_index("y"), lax.axis_index("c")
        chips = [(1 - mx, my), (mx, 1 - my), (1 - mx, 1 - my)]
        return [pltpu.make_async_remote_copy(
            src_ref=p_ref.at[2 * px + py], dst_ref=recv_ref.at[j],
            send_sem=send_sems.at[j], recv_sem=recv_sems.at[j], device_id=(px, py, mc), device_id_type=MESH)
            for j, (px, py) in enumerate(chips)]

    def start(self, ins, outs, scr):
        for cp in self._plan(ins, outs, scr):
            cp.start()

    def finish(self, ins, outs, scr):
        for cp in self._plan(ins, outs, scr):
            cp.wait()


class _Hosted:
    def __init__(self, jobs):
        self.jobs = list(jobs)
        self.inputs = [x for j in self.jobs for x in j.inputs]
        self.out_shapes = [s for j in self.jobs for s in j.out_shapes]
        self.scratch = [s for j in self.jobs for s in j.scratch]
        self.in_specs = [ANY] * len(self.inputs)
        self.out_specs = [ANY] * len(self.out_shapes)

    def split(self, refs, n_in, n_out, n_scratch):
        refs = list(refs)
        ji, jo, js = len(self.inputs), len(self.out_shapes), len(self.scratch)
        ins, refs = refs[:n_in], refs[n_in:]
        jins, refs = refs[:ji], refs[ji:]
        outs, refs = refs[:n_out], refs[n_out:]
        jouts, refs = refs[:jo], refs[jo:]
        scr, jscr = refs[:n_scratch], refs[n_scratch:]
        assert len(jscr) == js
        per_job = []
        for j in self.jobs:
            a, b, c = len(j.inputs), len(j.out_shapes), len(j.scratch)
            per_job.append((jins[:a], jouts[:b], jscr[:c]))
            jins, jouts, jscr = jins[a:], jouts[b:], jscr[c:]
        return ins, outs, scr, per_job

    def start(self, per_job, when):
        if self.jobs:
            @pl.when(when)
            def _():
                for j, refs in zip(self.jobs, per_job):
                    j.start(*refs)

    def finish(self, per_job, when):
        if self.jobs:
            @pl.when(when)
            def _():
                for j, refs in zip(self.jobs, per_job):
                    j.finish(*refs)


def _grid_edges(grid):
    first = last = None
    for ax, n in enumerate(grid):
        f, l = pl.program_id(ax) == 0, pl.program_id(ax) == n - 1
        first = f if first is None else jnp.logical_and(first, f)
        last = l if last is None else jnp.logical_and(last, l)
    return first, last


def _run_jobs(jobs, name):
    hosted = _Hosted(jobs)

    def body(*refs):
        _, _, _, per_job = hosted.split(refs, 0, 0, 0)
        for j, r in zip(hosted.jobs, per_job):
            j.start(*r)
        for j, r in zip(hosted.jobs, per_job):
            j.finish(*r)

    return pl.pallas_call(
        body, name=name, out_shape=tuple(hosted.out_shapes), in_specs=hosted.in_specs,
        out_specs=tuple(hosted.out_specs), scratch_shapes=hosted.scratch,
    )(*hosted.inputs)


def _mm_nn(a, b, out_dtype, nsplit, name, jobs=()):
    m, k = a.shape
    n = b.shape[1]
    w = n // nsplit
    tm = _tile(m, (2048, 1024, 512, 256, 128)) if k <= 2048 else _tile(m, (512, 256, 128))
    tn = _tile(w, (512, 256, 128))
    per = w // tn
    grid = (m // tm, n // tn)
    hosted = _Hosted(jobs)

    def body(*refs):
        (a_ref, b_ref), (o_ref,), _, per_job = hosted.split(refs, 2, 1, 0)
        first, last = _grid_edges(grid)
        hosted.start(per_job, first)
        o_ref[...] = jnp.dot(a_ref[...], b_ref[...], preferred_element_type=F32).astype(o_ref.dtype)
        hosted.finish(per_job, last)

    if nsplit == 1:
        out_shape = jax.ShapeDtypeStruct((m, n), out_dtype)
        out_spec = pl.BlockSpec((tm, tn), lambda i, j: (i, j))
    else:
        out_shape = jax.ShapeDtypeStruct((nsplit, m, w), out_dtype)
        out_spec = pl.BlockSpec((None, tm, tn), lambda i, j: (j // per, i, j % per))
    res = pl.pallas_call(
        body, grid=grid, name=name,
        in_specs=[pl.BlockSpec((tm, k), lambda i, j: (i, 0)), pl.BlockSpec((k, tn), lambda i, j: (0, j))]
        + hosted.in_specs,
        out_specs=tuple([out_spec] + hosted.out_specs), out_shape=tuple([out_shape] + hosted.out_shapes),
        scratch_shapes=hosted.scratch,
        compiler_params=_params("arbitrary", "arbitrary"),
    )(a, b, *hosted.inputs)
    return res if jobs else res[0]


def _mm_nt(a, b, out_dtype, name, jobs=()):
    a3 = a if a.ndim == 3 else a[None]
    ns, m, w = a3.shape
    ko, n = b.shape
    assert n == ns * w
    tm = _tile(m, (1024, 512, 256, 128)) if n <= 2048 else _tile(m, (512, 256, 128))
    tko = _tile(ko, (256, 128)) if n > 8192 else _tile(ko, (512, 256, 128))
    grid = (m // tm, ko // tko)
    hosted = _Hosted(jobs)

    def body(*refs):
        (a_ref, b_ref), (o_ref,), _, per_job = hosted.split(refs, 2, 1, 0)
        first, last = _grid_edges(grid)
        hosted.start(per_job, first)
        acc = None
        for s in range(ns):
            p = lax.dot_general(a_ref[s], b_ref[:, s * w:(s + 1) * w], (((1,), (1,)), ((), ())),
                                preferred_element_type=F32)
            acc = p if acc is None else acc + p
        o_ref[...] = acc.astype(o_ref.dtype)
        hosted.finish(per_job, last)

    res = pl.pallas_call(
        body, grid=grid, name=name,
        in_specs=[pl.BlockSpec((ns, tm, w), lambda i, j: (0, i, 0)), pl.BlockSpec((tko, n), lambda i, j: (j, 0))]
        + hosted.in_specs,
        out_specs=tuple([pl.BlockSpec((tm, tko), lambda i, j: (i, j))] + hosted.out_specs),
        out_shape=tuple([jax.ShapeDtypeStruct((m, ko), out_dtype)] + hosted.out_shapes),
        scratch_shapes=hosted.scratch,
        compiler_params=_params("arbitrary", "arbitrary"),
    )(a3, b, *hosted.inputs)
    return res if jobs else res[0]


def _mm_tn(a, b, out_dtype, name, jobs=()):
    b3 = b if b.ndim == 3 else b[None]
    ns, m, w = b3.shape
    k = a.shape[1]
    n = ns * w
    tk = _tile(k, (512, 256, 128))
    tn = _tile(w, (2816, 2048, 1024, 512, 256, 128))
    ts = _tile(m, (1024, 512, 256, 128))
    per = w // tn
    n_s = m // ts
    grid = (k // tk, n // tn, n_s)
    hosted = _Hosted(jobs)

    def body(*refs):
        (a_ref, b_ref), (o_ref,), (acc_ref,), per_job = hosted.split(refs, 2, 1, 1)
        s = pl.program_id(2)
        first, last = _grid_edges(grid)
        hosted.start(per_job, first)

        @pl.when(s == 0)
        def _():
            acc_ref[...] = jnp.zeros_like(acc_ref)

        acc_ref[...] += lax.dot_general(a_ref[...], b_ref[...], (((0,), (0,)), ((), ())),
                                        preferred_element_type=F32)

        @pl.when(s == n_s - 1)
        def _():
            o_ref[...] = acc_ref[...].astype(o_ref.dtype)

        hosted.finish(per_job, last)

    res = pl.pallas_call(
        body, grid=grid, name=name,
        in_specs=[pl.BlockSpec((ts, tk), lambda i, j, s: (s, i)),
                  pl.BlockSpec((None, ts, tn), lambda i, j, s: (j // per, s, j % per))] + hosted.in_specs,
        out_specs=tuple([pl.BlockSpec((tk, tn), lambda i, j, s: (i, j))] + hosted.out_specs),
        out_shape=tuple([jax.ShapeDtypeStruct((k, n), out_dtype)] + hosted.out_shapes),
        scratch_shapes=[pltpu.VMEM((tk, tn), F32)] + hosted.scratch,
        compiler_params=_params("arbitrary", "arbitrary", "arbitrary"),
    )(a, b3, *hosted.inputs)
    return res if jobs else res[0]


def _rms_rows(s):
    return _tile(s, (256, 128))


def _rms_fwd(x, g, name):
    s, d = x.shape
    ts = _rms_rows(s)

    def body(x_ref, g_ref, o_ref):
        xf = x_ref[...]
        r = lax.rsqrt(jnp.mean(xf * xf, axis=-1, keepdims=True) + EPS)
        o_ref[...] = ((xf * r) * g_ref[...]).astype(o_ref.dtype)

    return pl.pallas_call(
        body, grid=(s // ts,), name=name,
        in_specs=[pl.BlockSpec((ts, d), lambda i: (i, 0)), pl.BlockSpec((1, d), lambda i: (0, 0))],
        out_specs=pl.BlockSpec((ts, d), lambda i: (i, 0)),
        out_shape=jax.ShapeDtypeStruct((s, d), BF16),
        compiler_params=_params("parallel"),
    )(x, g)


def _resid_rms_fwd(h, m, g, g_next, name):
    s, d = h.shape
    ts = _rms_rows(s)
    both = g_next is not None

    def body(*refs):
        h_ref, m_ref, g_ref = refs[:3]
        mf = m_ref[...]
        r = lax.rsqrt(jnp.mean(mf * mf, axis=-1, keepdims=True) + EPS)
        hf = h_ref[...] + (mf * r) * g_ref[...]
        refs[-2 if both else -1][...] = hf
        if both:
            r2 = lax.rsqrt(jnp.mean(hf * hf, axis=-1, keepdims=True) + EPS)
            refs[-1][...] = ((hf * r2) * refs[3][...]).astype(BF16)

    row = pl.BlockSpec((ts, d), lambda i: (i, 0))
    vec = pl.BlockSpec((1, d), lambda i: (0, 0))
    res = pl.pallas_call(
        body, grid=(s // ts,), name=name,
        in_specs=[row, row, vec] + ([vec] if both else []),
        out_specs=(row, row) if both else (row,),
        out_shape=(jax.ShapeDtypeStruct((s, d), F32),) + ((jax.ShapeDtypeStruct((s, d), BF16),) if both else ()),
        compiler_params=_params("parallel"),
    )(*([h, m, g] + ([g_next] if both else [])))
    return res if both else (res[0], None)


def _rms_bwd(x, g, dy, dres, out_dtype, name):
    s, d = x.shape
    ts = _rms_rows(s)
    has_res = dres is not None

    def body(*refs):
        if has_res:
            x_ref, g_ref, dy_ref, dres_ref, dx_ref, dg_ref = refs
        else:
            x_ref, g_ref, dy_ref, dx_ref, dg_ref = refs
        i = pl.program_id(0)
        xf = x_ref[...]
        dyf = dy_ref[...].astype(F32)
        r = lax.rsqrt(jnp.mean(xf * xf, axis=-1, keepdims=True) + EPS)
        xh = xf * r
        dxh = dyf * g_ref[...]
        dx = r * (dxh - xh * jnp.mean(dxh * xh, axis=-1, keepdims=True))
        if has_res:
            dx = dx + dres_ref[...]
        dx_ref[...] = dx.astype(dx_ref.dtype)

        @pl.when(i == 0)
        def _():
            dg_ref[...] = jnp.zeros_like(dg_ref)

        dg_ref[...] += jnp.sum(dyf * xh, axis=0, keepdims=True)

    row = pl.BlockSpec((ts, d), lambda i: (i, 0))
    vec = pl.BlockSpec((1, d), lambda i: (0, 0))
    ins = [x, g, dy] + ([dres] if has_res else [])
    return pl.pallas_call(
        body, grid=(s // ts,), name=name,
        in_specs=[row, vec, row] + ([row] if has_res else []),
        out_specs=(row, vec),
        out_shape=(jax.ShapeDtypeStruct((s, d), out_dtype), jax.ShapeDtypeStruct((1, d), F32)),
        compiler_params=_params("arbitrary"),
    )(*ins)


def _loss_head(h, target, name):
    s, d = h.shape
    ts = _rms_rows(s)
    n_i = s // ts

    def body(h_ref, t_ref, dh_ref, loss_ref, acc_ref):
        i = pl.program_id(0)
        err = h_ref[...] - t_ref[...]
        dh_ref[...] = err * (1.0 / d)

        @pl.when(i == 0)
        def _():
            acc_ref[...] = jnp.zeros_like(acc_ref)

        acc_ref[...] += jnp.sum(err * err, axis=0, keepdims=True)

        @pl.when(i == n_i - 1)
        def _():
            loss_ref[...] = jnp.sum(acc_ref[...], axis=1, keepdims=True) * (0.5 / d)

    row = pl.BlockSpec((ts, d), lambda i: (i, 0))
    return pl.pallas_call(
        body, grid=(n_i,), name=name,
        in_specs=[row, row],
        out_specs=(row, pl.BlockSpec((1, 1), lambda i: (0, 0))),
        out_shape=(jax.ShapeDtypeStruct((s, d), F32), jax.ShapeDtypeStruct((1, 1), F32)),
        scratch_shapes=[pltpu.VMEM((1, d), F32)],
        compiler_params=_params("arbitrary"),
    )(h, target)


def _halo_specs(ns, s, ts, tc):
    hb = ts // HALO
    last = s // HALO - 1
    cur = pl.BlockSpec((ns, ts, tc), lambda j, i: (0, i, j))
    prev = pl.BlockSpec((ns, HALO, tc), lambda j, i: (0, jnp.maximum(i * hb - 1, 0), j))
    nxt = pl.BlockSpec((ns, HALO, tc), lambda j, i: (0, jnp.minimum((i + 1) * hb, last), j))
    return cur, prev, nxt


def _ext(prev, cur, nxt, first, last):
    p = jnp.where(first, 0.0, prev.astype(F32))
    n = jnp.where(last, 0.0, nxt.astype(F32))
    return jnp.concatenate([p, cur.astype(F32), n], axis=0)


def _dn(xe, k):
    return xe if k == 0 else pltpu.roll(xe, k, 0)


def _up(xe, k):
    return xe if k == 0 else pltpu.roll(xe, xe.shape[0] - k, 0)


def _conv_e(xe, w):
    return w[0:1, :] * _dn(xe, 2) + w[1:2, :] * _dn(xe, 1) + w[2:3, :] * xe


def _conv_t_e(de, w):
    return w[2:3, :] * de + w[1:2, :] * _up(de, 1) + w[0:1, :] * _up(de, 2)


def _conv_dw(de, xe, ts):
    return [jnp.sum((de * _dn(xe, 2 - k))[HALO:HALO + ts], axis=0, keepdims=True) for k in range(3)]


def _conv_tiles(s, c):
    return _tile(s, (512, 256, 128)), _tile(c, (512, 256, 128))


def _sigmoid(x):
    return pl.reciprocal(1.0 + jnp.exp(-x), approx=True)


def _sc_gate_fwd(p, w, name):
    _, s, c = p.shape
    ts, tc = _conv_tiles(s, c)
    n_i = s // ts

    def body(p_ref, pp_ref, w_ref, y_ref):
        i = pl.program_id(1)
        ze = _ext(pp_ref[1], p_ref[1], pp_ref[1], i == 0, True) * _ext(pp_ref[2], p_ref[2], pp_ref[2], i == 0, True)
        cz = _conv_e(ze, w_ref[...])[HALO:HALO + ts]
        y_ref[...] = (p_ref[0].astype(F32) * cz).astype(y_ref.dtype)

    cur, prev, _ = _halo_specs(3, s, ts, tc)
    return pl.pallas_call(
        body, grid=(c // tc, n_i), name=name,
        in_specs=[cur, prev, pl.BlockSpec((3, tc), lambda j, i: (0, j))],
        out_specs=pl.BlockSpec((ts, tc), lambda j, i: (i, j)),
        out_shape=jax.ShapeDtypeStruct((s, c), BF16),
        compiler_params=_params("parallel", "arbitrary"),
    )(p, p, w)


def _sc_gate_bwd(p, dy, w, name):
    _, s, c = p.shape
    ts, tc = _conv_tiles(s, c)
    n_i = s // ts

    def body(p_ref, pp_ref, pn_ref, dy_ref, dyn_ref, w_ref, dp_ref, dw_ref):
        i = pl.program_id(1)
        first, last = i == 0, i == n_i - 1
        wv = w_ref[...]
        gbe = _ext(pp_ref[0], p_ref[0], pn_ref[0], first, last)
        gce = _ext(pp_ref[1], p_ref[1], pn_ref[1], first, last)
        hhe = _ext(pp_ref[2], p_ref[2], pn_ref[2], first, last)
        dye = _ext(dyn_ref[...], dy_ref[...], dyn_ref[...], True, last)
        ze = gce * hhe
        cze = _conv_e(ze, wv)
        dcze = dye * gbe
        dze = _conv_t_e(dcze, wv)
        rows = slice(HALO, HALO + ts)
        dp_ref[0] = (dye * cze)[rows].astype(dp_ref.dtype)
        dp_ref[1] = (dze * hhe)[rows].astype(dp_ref.dtype)
        dp_ref[2] = (dze * gce)[rows].astype(dp_ref.dtype)

        @pl.when(first)
        def _():
            dw_ref[...] = jnp.zeros_like(dw_ref)

        dws = _conv_dw(dcze, ze, ts)
        for k in range(3):
            dw_ref[k:k + 1, :] += dws[k]

    cur, prev, nxt = _halo_specs(3, s, ts, tc)
    hb = ts // HALO
    last_h = s // HALO - 1
    return pl.pallas_call(
        body, grid=(c // tc, n_i), name=name,
        in_specs=[cur, prev, nxt,
                  pl.BlockSpec((ts, tc), lambda j, i: (i, j)),
                  pl.BlockSpec((HALO, tc), lambda j, i: (jnp.minimum((i + 1) * hb, last_h), j)),
                  pl.BlockSpec((3, tc), lambda j, i: (0, j))],
        out_specs=(cur, pl.BlockSpec((3, tc), lambda j, i: (0, j))),
        out_shape=(jax.ShapeDtypeStruct((3, s, c), BF16), jax.ShapeDtypeStruct((3, c), F32)),
        compiler_params=_params("parallel", "arbitrary"),
    )(p, p, p, dy, dy, w)


def _ffn_act_fwd(u, w, b, name):
    _, s, f = u.shape
    ts, tc = _conv_tiles(s, f)
    n_i = s // ts

    def body(u_ref, up_ref, w_ref, b_ref, a_ref):
        i = pl.program_id(1)
        rows = slice(HALO, HALO + ts)
        cg = _conv_e(_ext(up_ref[0], u_ref[0], up_ref[0], i == 0, True), w_ref[0])[rows] + b_ref[0]
        cv = _conv_e(_ext(up_ref[1], u_ref[1], up_ref[1], i == 0, True), w_ref[1])[rows] + b_ref[1]
        a_ref[...] = (cg * _sigmoid(cg) * cv).astype(a_ref.dtype)

    cur, prev, _ = _halo_specs(2, s, ts, tc)
    return pl.pallas_call(
        body, grid=(f // tc, n_i), name=name,
        in_specs=[cur, prev, pl.BlockSpec((2, 3, tc), lambda j, i: (0, 0, j)),
                  pl.BlockSpec((2, 1, tc), lambda j, i: (0, 0, j))],
        out_specs=pl.BlockSpec((ts, tc), lambda j, i: (i, j)),
        out_shape=jax.ShapeDtypeStruct((s, f), BF16),
        compiler_params=_params("parallel", "arbitrary"),
    )(u, u, w, b)


def _ffn_act_bwd(u, da, w, b, name):
    _, s, f = u.shape
    ts, tc = _conv_tiles(s, f)
    n_i = s // ts

    def body(u_ref, up_ref, un_ref, da_ref, dan_ref, w_ref, b_ref, du_ref, dw_ref, db_ref):
        i = pl.program_id(1)
        first, last = i == 0, i == n_i - 1
        rows = slice(HALO, HALO + ts)
        uge = _ext(up_ref[0], u_ref[0], un_ref[0], first, last)
        uve = _ext(up_ref[1], u_ref[1], un_ref[1], first, last)
        dae = _ext(dan_ref[...], da_ref[...], dan_ref[...], True, last)
        cg = _conv_e(uge, w_ref[0]) + b_ref[0]
        cv = _conv_e(uve, w_ref[1]) + b_ref[1]
        sg = _sigmoid(cg)
        dcg = dae * cv * (sg * (1.0 + cg * (1.0 - sg)))
        dcv = dae * (cg * sg)
        du_ref[0] = _conv_t_e(dcg, w_ref[0])[rows].astype(du_ref.dtype)
        du_ref[1] = _conv_t_e(dcv, w_ref[1])[rows].astype(du_ref.dtype)

        @pl.when(first)
        def _():
            dw_ref[...] = jnp.zeros_like(dw_ref)
            db_ref[...] = jnp.zeros_like(db_ref)

        for h, (de, xe) in enumerate(((dcg, uge), (dcv, uve))):
            dws = _conv_dw(de, xe, ts)
            for k in range(3):
                dw_ref[h, k:k + 1, :] += dws[k]
            db_ref[h] += jnp.sum(de[rows], axis=0, keepdims=True)

    cur, prev, nxt = _halo_specs(2, s, ts, tc)
    hb = ts // HALO
    last_h = s // HALO - 1
    wspec = pl.BlockSpec((2, 3, tc), lambda j, i: (0, 0, j))
    bspec = pl.BlockSpec((2, 1, tc), lambda j, i: (0, 0, j))
    return pl.pallas_call(
        body, grid=(f // tc, n_i), name=name,
        in_specs=[cur, prev, nxt,
                  pl.BlockSpec((ts, tc), lambda j, i: (i, j)),
                  pl.BlockSpec((HALO, tc), lambda j, i: (jnp.minimum((i + 1) * hb, last_h), j)),
                  wspec, bspec],
        out_specs=(cur, wspec, bspec),
        out_shape=(jax.ShapeDtypeStruct((2, s, f), BF16), jax.ShapeDtypeStruct((2, 3, f), F32),
                   jax.ShapeDtypeStruct((2, 1, f), F32)),
        compiler_params=_params("parallel", "arbitrary"),
    )(u, u, u, da, da, w, b)


_INV_SQRT2 = 0.7071067811865476
_INV_SQRT_2PI = 0.3989422804014327


def _gelu(x):
    return 0.5 * x * (1.0 + lax.erf(x * _INV_SQRT2))


def _gelu_grad(x):
    return 0.5 * (1.0 + lax.erf(x * _INV_SQRT2)) + x * (_INV_SQRT_2PI * jnp.exp(-0.5 * x * x))


def _tril_bf16(w):
    t = lax.broadcasted_iota(jnp.int32, w.shape, 0)
    s = lax.broadcasted_iota(jnp.int32, w.shape, 1)
    return jnp.where(s <= t, w, 0.0).astype(BF16)


def _sgu_fwd(p, ln_g, ln_b, ws, bs, name):
    _, s, c = p.shape
    g_n = ws.shape[0]
    cg = c // g_n

    def body(p_ref, lg_ref, lb_ref, ws_ref, bs_ref, y_ref):
        u = _gelu(p_ref[0].astype(F32))
        v = _gelu(p_ref[1].astype(F32))
        mu = jnp.mean(v, axis=-1, keepdims=True)
        vc = v - mu
        rstd = lax.rsqrt(jnp.mean(vc * vc, axis=-1, keepdims=True) + EPS)
        vn = ((vc * rstd) * lg_ref[...] + lb_ref[...]).astype(BF16)
        for g in range(g_n):
            cols = slice(g * cg, (g + 1) * cg)
            mixed = jnp.dot(_tril_bf16(ws_ref[g]), vn[:, cols], preferred_element_type=F32) + bs_ref[g]
            y_ref[:, cols] = (u[:, cols] * mixed).astype(y_ref.dtype)

    vec = pl.BlockSpec((1, c), lambda i: (0, 0))
    return pl.pallas_call(
        body, grid=(s // CHUNK,), name=name,
        in_specs=[pl.BlockSpec((2, CHUNK, c), lambda i: (0, i, 0)), vec, vec,
                  pl.BlockSpec((g_n, CHUNK, CHUNK), lambda i: (0, 0, 0)),
                  pl.BlockSpec((g_n, CHUNK, 1), lambda i: (0, 0, 0))],
        out_specs=pl.BlockSpec((CHUNK, c), lambda i: (i, 0)),
        out_shape=jax.ShapeDtypeStruct((s, c), BF16),
        compiler_params=_params("parallel"),
    )(p, ln_g, ln_b, ws, bs)


def _sgu_bwd(p, dy, ln_g, ln_b, ws, ws_t, bs, name):
    _, s, c = p.shape
    g_n = ws.shape[0]
    cg = c // g_n

    def body(p_ref, dy_ref, lg_ref, lb_ref, ws_ref, wst_ref, bs_ref, dp_ref, dlg_ref, dlb_ref, dws_ref, dbs_ref,
             dvn_ref):
        i = pl.program_id(0)
        pu = p_ref[0].astype(F32)
        pv = p_ref[1].astype(F32)
        dyf = dy_ref[...].astype(F32)
        u = _gelu(pu)
        v = _gelu(pv)
        mu = jnp.mean(v, axis=-1, keepdims=True)
        vc = v - mu
        rstd = lax.rsqrt(jnp.mean(vc * vc, axis=-1, keepdims=True) + EPS)
        vhat = vc * rstd
        vn = (vhat * lg_ref[...] + lb_ref[...]).astype(BF16)

        @pl.when(i == 0)
        def _():
            dlg_ref[...] = jnp.zeros_like(dlg_ref)
            dlb_ref[...] = jnp.zeros_like(dlb_ref)
            dws_ref[...] = jnp.zeros_like(dws_ref)
            dbs_ref[...] = jnp.zeros_like(dbs_ref)

        t_i = lax.broadcasted_iota(jnp.int32, (CHUNK, CHUNK), 0)
        s_i = lax.broadcasted_iota(jnp.int32, (CHUNK, CHUNK), 1)
        for g in range(g_n):
            cols = slice(g * cg, (g + 1) * cg)
            mixed = jnp.dot(_tril_bf16(ws_ref[g]), vn[:, cols], preferred_element_type=F32) + bs_ref[g]
            dp_ref[0, :, cols] = (dyf[:, cols] * mixed * _gelu_grad(pu[:, cols])).astype(dp_ref.dtype)
            dmixed = dyf[:, cols] * u[:, cols]
            dmb = dmixed.astype(BF16)
            wt = jnp.where(t_i <= s_i, wst_ref[g], 0.0).astype(BF16)
            dvn_ref[:, cols] = jnp.dot(wt, dmb, preferred_element_type=F32)
            dwg = lax.dot_general(dmb, vn[:, cols], (((1,), (1,)), ((), ())), preferred_element_type=F32)
            dws_ref[g] += jnp.where(s_i <= t_i, dwg, 0.0)
            dbs_ref[g] += jnp.sum(dmixed, axis=1, keepdims=True)
        dvn = dvn_ref[...]
        dlg_ref[...] += jnp.sum(dvn * vhat, axis=0, keepdims=True)
        dlb_ref[...] += jnp.sum(dvn, axis=0, keepdims=True)
        dvh = dvn * lg_ref[...]
        dv = rstd * (dvh - jnp.mean(dvh, axis=-1, keepdims=True) - vhat * jnp.mean(dvh * vhat, axis=-1, keepdims=True))
        dp_ref[1] = (dv * _gelu_grad(pv)).astype(dp_ref.dtype)

    vec = pl.BlockSpec((1, c), lambda i: (0, 0))
    wspec = pl.BlockSpec((g_n, CHUNK, CHUNK), lambda i: (0, 0, 0))
    bspec = pl.BlockSpec((g_n, CHUNK, 1), lambda i: (0, 0, 0))
    pspec = pl.BlockSpec((2, CHUNK, c), lambda i: (0, i, 0))
    return pl.pallas_call(
        body, grid=(s // CHUNK,), name=name,
        in_specs=[pspec, pl.BlockSpec((CHUNK, c), lambda i: (i, 0)), vec, vec, wspec, wspec, bspec],
        out_specs=(pspec, vec, vec, wspec, bspec),
        out_shape=(jax.ShapeDtypeStruct((2, s, c), BF16), jax.ShapeDtypeStruct((1, c), F32),
                   jax.ShapeDtypeStruct((1, c), F32), jax.ShapeDtypeStruct((g_n, CHUNK, CHUNK), F32),
                   jax.ShapeDtypeStruct((g_n, CHUNK, 1), F32)),
        scratch_shapes=[pltpu.VMEM((CHUNK, c), F32)],
        compiler_params=_params("arbitrary"),
    )(p, dy, ln_g, ln_b, ws, ws_t, bs)


SB_Q = 512


def _sb_cumsum(x, tri, edge, reverse):
    n = x.shape[0]
    hi = x.astype(BF16)
    lo = (x - hi.astype(F32)).astype(BF16)
    chunks = x.shape[1] // CHUNK
    outs = [None] * chunks
    for c in (reversed(range(chunks)) if reverse else range(chunks)):
        cols = slice(c * CHUNK, (c + 1) * CHUNK)
        both = jnp.dot(jnp.concatenate([hi[:, cols], lo[:, cols]], axis=0), tri, preferred_element_type=F32)
        outs[c] = both[:n] + both[n:] + edge
        edge = edge + jnp.sum(x[:, cols], axis=1, keepdims=True)
    return jnp.concatenate(outs, axis=1), edge


def _sb_block(q, k, offset):
    scale = HEAD_DIM ** -0.5
    z = lax.dot_general(q, k, (((1,), (1,)), ((), ())), preferred_element_type=F32) * scale
    e = jnp.exp(-jnp.abs(z))
    lp = jnp.log(1.0 + e)
    lb = jnp.minimum(z, 0.0) - lp
    l1 = lb - z
    if offset is not None:
        t_i = lax.broadcasted_iota(jnp.int32, z.shape, 0)
        s_i = lax.broadcasted_iota(jnp.int32, z.shape, 1)
        mask = (t_i - s_i) > offset
        l1 = jnp.where(mask, l1, 0.0)
    else:
        mask = None
    return z, e, lb, l1, mask


def _tri(cond):
    j = lax.broadcasted_iota(jnp.int32, (CHUNK, CHUNK), 0)
    s = lax.broadcasted_iota(jnp.int32, (CHUNK, CHUNK), 1)
    return jnp.where(cond(j, s), 1.0, 0.0).astype(BF16)


def _sb_fwd(qkv, name):
    _, s, c = qkv.shape
    heads = c // HEAD_DIM
    bq = _tile(s, (SB_Q, 256, CHUNK))
    nq = s // bq

    def body(q_ref, k_ref, v_ref, o_ref, runs_ref):
        i = pl.program_id(1)
        q = q_ref[...]
        after = _tri(lambda j, s_: j > s_)
        runs_ref[...] = jnp.zeros_like(runs_ref)

        def tile(g, carry, offset):
            o_acc, run = carry
            rows = pl.ds(pl.multiple_of(g * bq, bq), bq)
            runs_ref[pl.ds(pl.multiple_of(g * 8, 8), 8), :] = jnp.transpose(jnp.broadcast_to(run, (bq, LANES)))[:8]
            _, _, lb, l1, mask = _sb_block(q, k_ref[rows, :], offset)
            acc, run = _sb_cumsum(l1, after, run, True)
            a = jnp.exp(lb + acc)
            if mask is not None:
                a = jnp.where(mask, a, 0.0)
            o_acc = o_acc + jnp.dot(a.astype(BF16), v_ref[rows, :], preferred_element_type=F32)
            return o_acc, run

        carry = tile(i, (jnp.zeros((bq, HEAD_DIM), F32), jnp.zeros((bq, 1), F32)), 0)
        o_acc, _ = lax.fori_loop(0, i, lambda n, cr: tile(i - 1 - n, cr, None), carry)
        o_ref[...] = o_acc.astype(o_ref.dtype)

    return pl.pallas_call(
        body, grid=(heads, nq), name=name,
        in_specs=[pl.BlockSpec((None, bq, HEAD_DIM), lambda h, i: (0, i, h)),
                  pl.BlockSpec((None, s, HEAD_DIM), lambda h, i: (1, 0, h)),
                  pl.BlockSpec((None, s, HEAD_DIM), lambda h, i: (2, 0, h))],
        out_specs=(pl.BlockSpec((bq, HEAD_DIM), lambda h, i: (i, h)),
                   pl.BlockSpec((None, None, 8 * nq, bq), lambda h, i: (h, i, 0, 0))),
        out_shape=(jax.ShapeDtypeStruct((s, c), BF16), jax.ShapeDtypeStruct((heads, nq, 8 * nq, bq), F32)),
        compiler_params=_params("parallel", "arbitrary"),
    )(qkv, qkv, qkv)


def _sb_bwd(qkv, do, runs, name):
    _, s, c = qkv.shape
    heads = c // HEAD_DIM
    bq = _tile(s, (SB_Q, 256, CHUNK))
    nq = s // bq
    scale = HEAD_DIM ** -0.5
    tn_dims = (((0,), (0,)), ((), ()))

    def body(q_ref, k_ref, v_ref, do_ref, runs_ref, dq_ref, dkv_ref, dk_acc, dv_acc):
        i = pl.program_id(1)
        q = q_ref[...]
        dob = do_ref[...]
        after = _tri(lambda j, s_: j > s_)
        before = _tri(lambda j, s_: j < s_)

        @pl.when(i == 0)
        def _():
            dk_acc[...] = jnp.zeros_like(dk_acc)
            dv_acc[...] = jnp.zeros_like(dv_acc)

        def main(g, carry, offset):
            dq_acc, csum = carry
            rows = pl.ds(pl.multiple_of(g * bq, bq), bq)
            kblk = k_ref[rows, :]
            vblk = v_ref[rows, :]
            z, e, lb, l1, mask = _sb_block(q, kblk, offset)
            run = jnp.transpose(jnp.broadcast_to(runs_ref[pl.ds(pl.multiple_of(g * 8, 8), 1), :], (LANES, bq)))
            acc, _ = _sb_cumsum(l1, after, run, True)
            a = jnp.exp(lb + acc)
            if mask is not None:
                a = jnp.where(mask, a, 0.0)
            d_a = lax.dot_general(dob, vblk, (((1,), (1,)), ((), ())), preferred_element_type=F32)
            ee = a * d_a
            cc, csum = _sb_cumsum(ee, before, csum, False)
            beta = jnp.exp(lb)
            dz = (ee - (ee + cc) * beta) * scale
            if mask is not None:
                dz = jnp.where(mask, dz, 0.0)
            dzb = dz.astype(BF16)
            dq_acc = dq_acc + jnp.dot(dzb, kblk, preferred_element_type=F32)
            dk_acc[rows, :] += lax.dot_general(dzb, q, tn_dims, preferred_element_type=F32)
            dv_acc[rows, :] += lax.dot_general(a.astype(BF16), dob, tn_dims, preferred_element_type=F32)
            return dq_acc, csum

        carry = (jnp.zeros((bq, HEAD_DIM), F32), jnp.zeros((bq, 1), F32))
        carry = lax.fori_loop(0, i, lambda g, cr: main(g, cr, None), carry)
        dq_acc, _ = main(i, carry, 0)
        dq_ref[...] = dq_acc.astype(dq_ref.dtype)

        @pl.when(i == nq - 1)
        def _():
            dkv_ref[0] = dk_acc[...].astype(dkv_ref.dtype)
            dkv_ref[1] = dv_acc[...].astype(dkv_ref.dtype)

    blk = pl.BlockSpec((bq, HEAD_DIM), lambda h, i: (i, h))
    return pl.pallas_call(
        body, grid=(heads, nq), name=name,
        in_specs=[pl.BlockSpec((None, bq, HEAD_DIM), lambda h, i: (0, i, h)),
                  pl.BlockSpec((None, s, HEAD_DIM), lambda h, i: (1, 0, h)),
                  pl.BlockSpec((None, s, HEAD_DIM), lambda h, i: (2, 0, h)),
                  blk,
                  pl.BlockSpec((None, None, 8 * nq, bq), lambda h, i: (h, i, 0, 0))],
        out_specs=(blk, pl.BlockSpec((2, s, HEAD_DIM), lambda h, i: (0, 0, h))),
        out_shape=(jax.ShapeDtypeStruct((s, c), BF16), jax.ShapeDtypeStruct((2, s, c), BF16)),
        scratch_shapes=[pltpu.VMEM((s, HEAD_DIM), F32), pltpu.VMEM((s, HEAD_DIM), F32)],
        compiler_params=_params("parallel", "arbitrary"),
    )(qkv, qkv, qkv, do, runs)


def _all_gather(x, axis, name):
    return _run_jobs([_GatherJob(x, axis)], name)[0]


def _rs_core_exchange(g, axis, name):
    r, c = (g.shape[0] // N_DEV, g.shape[1]) if axis == 0 else (g.shape[0], g.shape[1] // N_DEV)

    def body(g_ref, recv_ref, send_sems, recv_sems):
        mx, my, mc = lax.axis_index("x"), lax.axis_index("y"), lax.axis_index("c")
        copies = [pltpu.make_async_remote_copy(
            src_ref=_shard_of(g_ref, 2 * k + (1 - mc), axis, r, c), dst_ref=recv_ref.at[k],
            send_sem=send_sems.at[k], recv_sem=recv_sems.at[k], device_id=(mx, my, 1 - mc), device_id_type=MESH)
            for k in range(4)]
        for cp in copies:
            cp.start()
        for cp in copies:
            cp.wait()

    return pl.pallas_call(
        body, name=name,
        out_shape=jax.ShapeDtypeStruct((4, r, c), g.dtype),
        in_specs=[ANY], out_specs=ANY,
        scratch_shapes=[pltpu.SemaphoreType.DMA((4,)), pltpu.SemaphoreType.DMA((4,))],
    )(g)


def _ew_tiles(r, c):
    tc = c if c <= 2048 else _tile(c, (2048, 1024, 512, 256, 128))
    return _tile(r, [p for p in (512, 256, 128, 64, 32, 16, 8) if p * tc <= 256 * 1024]), tc


def _rs_core_add(g, recv, core, axis, name):
    _, r, c = recv.shape
    tr, tc = _ew_tiles(r, c)
    nrb, ncb = r // tr, c // tc

    def body(core_ref, g_ref, x_ref, o_ref):
        del core_ref
        o_ref[...] = (g_ref[...].astype(F32) + x_ref[...].astype(F32)).astype(o_ref.dtype)

    if axis == 0:
        g_spec = pl.BlockSpec((tr, tc), lambda k, i, j, core: ((2 * k + core[0]) * nrb + i, j))
    else:
        g_spec = pl.BlockSpec((tr, tc), lambda k, i, j, core: (i, (2 * k + core[0]) * ncb + j))
    blk = pl.BlockSpec((None, tr, tc), lambda k, i, j, core: (k, i, j))
    return pl.pallas_call(
        body, name=name,
        grid_spec=pltpu.PrefetchScalarGridSpec(
            num_scalar_prefetch=1, grid=(4, nrb, ncb), in_specs=[g_spec, blk], out_specs=blk),
        out_shape=jax.ShapeDtypeStruct(recv.shape, BF16),
        compiler_params=_params("parallel", "parallel", "parallel"),
    )(core, g, recv)


def _rs_chip_exchange(parts, name):
    return _run_jobs([_ChipExchangeJob(p) for p in parts], name)


def _adamw_math(w, g, m, v):
    m = ADAM_B1 * m + (1.0 - ADAM_B1) * g
    v = ADAM_B2 * v + (1.0 - ADAM_B2) * (g * g)
    m_hat = m / (1.0 - ADAM_B1 ** ADAM_STEP)
    v_hat = v / (1.0 - ADAM_B2 ** ADAM_STEP)
    delta = -ADAM_LR * (m_hat / (jnp.sqrt(v_hat) + ADAM_EPS) + ADAM_WD * w)
    return delta, m, v


def _adamw_shard(w, m, v, layer, p, recv, chip, acc, name):
    n_layers, r, c = w.shape
    tr, tc = _ew_tiles(r, c)
    if n_layers == 1:
        acc = ()
    elif acc is None:
        acc = tuple(lax.empty((n_layers, r, c), F32) for _ in range(4))

    def body(chip_ref, w_ref, m_ref, v_ref, p_ref, x_ref, *rest):
        del chip_ref
        g_out, d_out, m_out, v_out = rest[len(acc):]
        g = p_ref[...].astype(F32)
        for j in range(3):
            g = g + x_ref[j].astype(F32)
        d, mn, vn = _adamw_math(w_ref[...], g, m_ref[...], v_ref[...])
        g_out[...] = g
        d_out[...] = d
        m_out[...] = mn
        v_out[...] = vn

    lay = pl.BlockSpec((None, tr, tc), lambda i, j, chip: (layer, i, j))
    out = jax.ShapeDtypeStruct((n_layers, r, c), F32)
    return pl.pallas_call(
        body, name=name,
        grid_spec=pltpu.PrefetchScalarGridSpec(
            num_scalar_prefetch=1, grid=(r // tr, c // tc),
            in_specs=[lay, lay, lay, pl.BlockSpec((None, tr, tc), lambda i, j, chip: (chip[0], i, j)),
                      pl.BlockSpec((3, tr, tc), lambda i, j, chip: (0, i, j))] + [ANY] * len(acc),
            out_specs=(lay, lay, lay, lay)),
        out_shape=(out, out, out, out),
        input_output_aliases={6 + t: t for t in range(len(acc))},
        compiler_params=_params("parallel", "parallel"),
    )(chip, w, m, v, p, recv, *acc)


def _adamw_packed(w, m, v, g, name):
    r, c = w.shape
    tr = _tile(r, (512, 256, 128, 64, 32, 16, 8))

    def body(w_ref, m_ref, v_ref, g_ref, d_out, m_out, v_out):
        d, mn, vn = _adamw_math(w_ref[...], g_ref[...], m_ref[...], v_ref[...])
        d_out[...] = d
        m_out[...] = mn
        v_out[...] = vn

    blk = pl.BlockSpec((tr, c), lambda i: (i, 0))
    out = jax.ShapeDtypeStruct((r, c), F32)
    return pl.pallas_call(
        body, grid=(r // tr,), name=name, in_specs=[blk, blk, blk, blk], out_specs=(blk, blk, blk),
        out_shape=(out, out, out), compiler_params=_params("parallel"),
    )(w, m, v, g)


def _sum_devices(x, name):
    r = x.shape[0] // N_DEV
    tr = _tile(r, (512, 256, 128, 64, 32, 16, 8))

    def body(x_ref, o_ref):
        acc = x_ref[0]
        for d in range(1, N_DEV):
            acc = acc + x_ref[d]
        o_ref[...] = acc

    return pl.pallas_call(
        body, grid=(r // tr,), name=name,
        in_specs=[pl.BlockSpec((N_DEV, tr, LANES), lambda i: (0, i, 0))],
        out_specs=pl.BlockSpec((tr, LANES), lambda i: (i, 0)),
        out_shape=jax.ShapeDtypeStruct((r, LANES), F32),
        compiler_params=_params("parallel"),
    )(x.reshape(N_DEV, r, LANES))


def _pack(arrays):
    flat = []
    for a in arrays:
        f = a.reshape(-1).astype(F32)
        flat.append(jnp.pad(f, (0, (-f.shape[0]) % LANES)))
    f = jnp.concatenate(flat)
    f = jnp.pad(f, (0, (-f.shape[0]) % (8 * LANES)))
    return f.reshape(-1, LANES)


def _unpack(packed, shapes):
    flat = packed.reshape(-1)
    out, off = [], 0
    for shp in shapes:
        n = 1
        for d in shp:
            n *= d
        out.append(flat[off:off + n].reshape(shp))
        off += n + (-n) % LANES
    return out


def kernel(x, norm_mix_pre, norm_mix_post, norm_ffn_pre, norm_ffn_post, sc_w_in, sc_conv_w, sc_w_out, sg_w_in, sg_ln_g, sg_ln_b, sg_w_s, sg_b_s, sg_w_out, sb_w_qkv, sb_w_out, ffn_w_up, ffn_conv_w, ffn_conv_b, ffn_w_down, loss_target, m_norm_mix_pre, m_norm_mix_post, m_norm_ffn_pre, m_norm_ffn_post, m_sc_w_in, m_sc_conv_w, m_sc_w_out, m_sg_w_in, m_sg_ln_g, m_sg_ln_b, m_sg_w_s, m_sg_b_s, m_sg_w_out, m_sb_w_qkv, m_sb_w_out, m_ffn_w_up, m_ffn_conv_w, m_ffn_conv_b, m_ffn_w_down, v_norm_mix_pre, v_norm_mix_post, v_norm_ffn_pre, v_norm_ffn_post, v_sc_w_in, v_sc_conv_w, v_sc_w_out, v_sg_w_in, v_sg_ln_g, v_sg_ln_b, v_sg_w_s, v_sg_b_s, v_sg_w_out, v_sb_w_qkv, v_sb_w_out, v_ffn_w_up, v_ffn_conv_w, v_ffn_conv_b, v_ffn_w_down):
    weights = dict(norm_mix_pre=norm_mix_pre, norm_mix_post=norm_mix_post, norm_ffn_pre=norm_ffn_pre,
                   norm_ffn_post=norm_ffn_post, sc_w_in=sc_w_in, sc_conv_w=sc_conv_w, sc_w_out=sc_w_out,
                   sg_w_in=sg_w_in, sg_ln_g=sg_ln_g, sg_ln_b=sg_ln_b, sg_w_s=sg_w_s, sg_b_s=sg_b_s,
                   sg_w_out=sg_w_out, sb_w_qkv=sb_w_qkv, sb_w_out=sb_w_out, ffn_w_up=ffn_w_up,
                   ffn_conv_w=ffn_conv_w, ffn_conv_b=ffn_conv_b, ffn_w_down=ffn_w_down)
    mom1 = dict(norm_mix_pre=m_norm_mix_pre, norm_mix_post=m_norm_mix_post, norm_ffn_pre=m_norm_ffn_pre,
                norm_ffn_post=m_norm_ffn_post, sc_w_in=m_sc_w_in, sc_conv_w=m_sc_conv_w, sc_w_out=m_sc_w_out,
                sg_w_in=m_sg_w_in, sg_ln_g=m_sg_ln_g, sg_ln_b=m_sg_ln_b, sg_w_s=m_sg_w_s, sg_b_s=m_sg_b_s,
                sg_w_out=m_sg_w_out, sb_w_qkv=m_sb_w_qkv, sb_w_out=m_sb_w_out, ffn_w_up=m_ffn_w_up,
                ffn_conv_w=m_ffn_conv_w, ffn_conv_b=m_ffn_conv_b, ffn_w_down=m_ffn_w_down)
    mom2 = dict(norm_mix_pre=v_norm_mix_pre, norm_mix_post=v_norm_mix_post, norm_ffn_pre=v_norm_ffn_pre,
                norm_ffn_post=v_norm_ffn_post, sc_w_in=v_sc_w_in, sc_conv_w=v_sc_conv_w, sc_w_out=v_sc_w_out,
                sg_w_in=v_sg_w_in, sg_ln_g=v_sg_ln_g, sg_ln_b=v_sg_ln_b, sg_w_s=v_sg_w_s, sg_b_s=v_sg_b_s,
                sg_w_out=v_sg_w_out, sb_w_qkv=v_sb_w_qkv, sb_w_out=v_sb_w_out, ffn_w_up=v_ffn_w_up,
                ffn_conv_w=v_ffn_conv_w, ffn_conv_b=v_ffn_conv_b, ffn_w_down=v_ffn_w_down)
    order = list(weights)

    mx, my, mc = lax.axis_index("x"), lax.axis_index("y"), lax.axis_index("c")
    dev = 4 * mx + 2 * my + mc
    core_arr = jnp.reshape(mc, (1,)).astype(jnp.int32)
    chip_arr = jnp.reshape(2 * mx + my, (1,)).astype(jnp.int32)

    h = x[0]
    target = loss_target[0]
    depth = norm_mix_pre.shape[0]
    d_model = h.shape[1]
    f_loc = ffn_w_down.shape[1]
    d_ff = f_loc * N_DEV

    def gather_rows8(a2d, name):
        rows = a2d.shape[0]
        pad = (-rows) % 8
        return _all_gather(jnp.pad(a2d, ((0, pad), (0, 0))), 1, name)[:rows]

    n_a = sc_w_in.shape[0]
    sc_conv_full = gather_rows8(sc_conv_w.reshape(n_a * 3, -1), "ag_sc_conv").reshape(n_a, 3, d_model)
    ffn_conv_full = gather_rows8(ffn_conv_w.reshape(depth * 3, -1), "ag_ffn_conv").reshape(depth, 3, 2 * d_ff)

    big = {}

    kinds = [i % 3 for i in range(depth)]
    mixer_names = [("sc_w_in", "sc_w_out"), ("sg_w_in", "sg_w_out"), ("sb_w_qkv", "sb_w_out")]
    mixer_split = [3, 2, 3]
    mixer_proj = ["sc_in_proj", "sg_in_proj", "sb_qkv_proj"]

    def layer_keys(i):
        wi, wo = mixer_names[kinds[i]]
        return (wi, i // 3, 1), (wo, i // 3, 0), ("ffn_w_up", i, 1), ("ffn_w_down", i, 0)

    def gather_job(key):
        name, j, axis = key
        return _GatherJob(weights[name][j].astype(BF16), axis)

    def hosted_mm(a, key, out_dtype, nsplit, name, gather_keys):
        keys = [k for k in gather_keys if k is not None]
        res = _mm_nn(a, big[key[:2]], out_dtype, nsplit, name, jobs=[gather_job(k) for k in keys])
        if not keys:
            return res
        for k, full in zip(keys, res[1:]):
            big[k[:2]] = full
        return res[0]

    sg_ws_t = jnp.swapaxes(sg_w_s, -1, -2)
    sg_bs3 = sg_b_s[..., None]

    k_in0 = layer_keys(0)[0]
    big[k_in0[:2]] = _all_gather(weights[k_in0[0]][k_in0[1]].astype(BF16), k_in0[2], f"ag_{k_in0[0]}")
    saved = []
    hn = _rms_fwd(h, norm_mix_pre[0:1], "rms_fwd")
    for i in range(depth):
        j = i // 3
        k_in, k_out, k_up, k_down = layer_keys(i)
        nxt = layer_keys(i + 1) if i + 1 < depth else (None, None, None, None)
        sv = {"h_in": h, "hn": hn}
        p = hosted_mm(hn, k_in, BF16, mixer_split[kinds[i]], mixer_proj[kinds[i]],
                      [k_out, k_up] if i == 0 else [])
        if kinds[i] == 0:
            y = _sc_gate_fwd(p, sc_conv_full[j], "sc_gate_fwd")
        elif kinds[i] == 1:
            y = _sgu_fwd(p, sg_ln_g[j:j + 1], sg_ln_b[j:j + 1], sg_w_s[j], sg_bs3[j], "sgu_fwd")
        else:
            y, sv["runs"] = _sb_fwd(p, "sb_fwd")
        m = hosted_mm(y, k_out, F32, 1, "mix_out_proj", [])
        sv.update(p=p, y=y, m=m)
        h, hn2 = _resid_rms_fwd(h, m, norm_mix_post[i:i + 1], norm_ffn_pre[i:i + 1], "resid_rms_fwd")
        sv["h_mid"] = h
        u = hosted_mm(hn2, k_up, BF16, 2, "ffn_up_proj", [k_down, nxt[0], nxt[1]])
        cw = ffn_conv_full[i].reshape(3, 2, d_ff).transpose(1, 0, 2)
        cb = ffn_conv_b[i].reshape(2, 1, d_ff)
        a = _ffn_act_fwd(u, cw, cb, "ffn_act_fwd")
        f = hosted_mm(a, k_down, F32, 1, "ffn_down_proj", [nxt[2]])
        sv.update(hn2=hn2, u=u, a=a, f=f, cw=cw, cb=cb)
        h, hn = _resid_rms_fwd(h, f, norm_ffn_post[i:i + 1], norm_mix_pre[i + 1:i + 2] if i + 1 < depth else None,
                               "resid_rms_fwd")
        saved.append(sv)

    dh, loss_part = _loss_head(h, target, "loss_head")

    grads_big = {}
    small = {}

    parts = {}
    queue = []

    def reduce_scatter(name, j, g_full, axis):
        recv1 = _rs_core_exchange(g_full, axis, f"rs1_{name}")
        parts[(name, j)] = _rs_core_add(g_full, recv1, core_arr, axis, f"rs_add_{name}")
        queue.append((name, j))

    def hosting(mm, a, b, out_dtype, name, n_jobs):
        keys = [queue.pop(0) for _ in range(min(n_jobs, len(queue)))]
        res = mm(a, b, out_dtype, name, jobs=[_ChipExchangeJob(parts[k]) for k in keys])
        if not keys:
            return res
        for k, recv in zip(keys, res[1:]):
            grads_big[k] = (parts[k], recv)
        return res[0]

    for i in reversed(range(depth)):
        j = i // 3
        sv = saved[i]
        df, dg = _rms_bwd(sv["f"], norm_ffn_post[i:i + 1], dh, None, BF16, "rms_bwd_post")
        small.setdefault("norm_ffn_post", {})[i] = dg
        da = hosting(_mm_nt, df, big[("ffn_w_down", i)], BF16, "ffn_down_dx", 1)
        reduce_scatter("ffn_w_down", i, hosting(_mm_tn, sv["a"], df, BF16, "ffn_down_dw", 2), 0)
        du, dcw, dcb = _ffn_act_bwd(sv["u"], da, sv["cw"], sv["cb"], "ffn_act_bwd")
        small.setdefault("ffn_conv_w", {})[i] = dcw.transpose(1, 0, 2).reshape(3, 2 * d_ff)
        small.setdefault("ffn_conv_b", {})[i] = dcb.reshape(2 * d_ff)
        dhn2 = _mm_nt(du, big[("ffn_w_up", i)], F32, "ffn_up_dx")
        reduce_scatter("ffn_w_up", i, _mm_tn(sv["hn2"], du, BF16, "ffn_up_dw"), 1)
        dh, dg = _rms_bwd(sv["h_mid"], norm_ffn_pre[i:i + 1], dhn2, dh, F32, "rms_bwd_pre")
        small.setdefault("norm_ffn_pre", {})[i] = dg
        dm, dg = _rms_bwd(sv["m"], norm_mix_post[i:i + 1], dh, None, BF16, "rms_bwd_post")
        small.setdefault("norm_mix_post", {})[i] = dg
        if kinds[i] == 0:
            wo, wi = "sc_w_out", "sc_w_in"
        elif kinds[i] == 1:
            wo, wi = "sg_w_out", "sg_w_in"
        else:
            wo, wi = "sb_w_out", "sb_w_qkv"
        dy = hosting(_mm_nt, dm, big[(wo, j)], BF16, "mix_out_dx", 1)
        reduce_scatter(wo, j, _mm_tn(sv["y"], dm, BF16, "mix_out_dw"), 0)
        if kinds[i] == 0:
            dp, dcw = _sc_gate_bwd(sv["p"], dy, sc_conv_full[j], "sc_gate_bwd")
            small.setdefault("sc_conv_w", {})[j] = dcw
        elif kinds[i] == 1:
            dp, dlg, dlb, dws, dbs = _sgu_bwd(sv["p"], dy, sg_ln_g[j:j + 1], sg_ln_b[j:j + 1], sg_w_s[j], sg_ws_t[j],
                                              sg_bs3[j], "sgu_bwd")
            small.setdefault("sg_ln_g", {})[j] = dlg[0]
            small.setdefault("sg_ln_b", {})[j] = dlb[0]
            small.setdefault("sg_w_s", {})[j] = dws
            small.setdefault("sg_b_s", {})[j] = dbs[..., 0]
        else:
            dq, dkv = _sb_bwd(sv["p"], dy, sv["runs"], "sb_bwd")
            dp = jnp.concatenate([dq[None], dkv], axis=0)
        dhn = hosting(_mm_nt, dp, big[(wi, j)], F32, "mix_in_dx", 2)
        reduce_scatter(wi, j, _mm_tn(sv["hn"], dp, BF16, "mix_in_dw"), 1)
        dh, dg = _rms_bwd(sv["h_in"], norm_mix_pre[i:i + 1], dhn, dh, F32, "rms_bwd_pre")
        small.setdefault("norm_mix_pre", {})[i] = dg

    grad_x = dh[None]
    for k, recv in zip(queue, _rs_chip_exchange([parts[k] for k in queue], "rs2_tail")):
        grads_big[k] = (parts[k], recv)

    small_names = ["norm_mix_pre", "norm_mix_post", "norm_ffn_pre", "norm_ffn_post", "sg_ln_g", "sg_ln_b", "sg_w_s",
                   "sg_b_s", "ffn_conv_b", "sc_conv_w", "ffn_conv_w"]
    full_shapes = {n: weights[n].shape for n in small_names}
    full_shapes["sc_conv_w"] = (n_a, 3, d_model)
    full_shapes["ffn_conv_w"] = (depth, 3, 2 * d_ff)
    partial = [jnp.stack([small[n][k].reshape(full_shapes[n][1:]) for k in sorted(small[n])]) for n in small_names]
    packed = _pack(partial + [loss_part])
    gathered = _all_gather(packed, 0, "ag_small_grads")
    summed = _sum_devices(gathered, "sum_small_grads")
    pieces = _unpack(summed, [full_shapes[n] for n in small_names] + [(1, 1)])
    loss = pieces[-1].reshape(())
    small_grads = dict(zip(small_names, pieces[:-1]))
    for n in ("sc_conv_w", "ffn_conv_w"):
        c_loc = weights[n].shape[-1]
        small_grads[n] = lax.dynamic_slice_in_dim(small_grads[n], dev * c_loc, c_loc, axis=2)
    shapes = [weights[n].shape for n in small_names]
    d_p, m_p, v_p = _adamw_packed(_pack([weights[n] for n in small_names]), _pack([mom1[n] for n in small_names]),
                                  _pack([mom2[n] for n in small_names]), _pack([small_grads[n] for n in small_names]),
                                  "adamw_small")
    out_g, out_d, out_m, out_v = dict(small_grads), {}, {}, {}
    for n, d_, m_, v_ in zip(small_names, _unpack(d_p, shapes), _unpack(m_p, shapes), _unpack(v_p, shapes)):
        out_d[n], out_m[n], out_v[n] = d_, m_, v_

    for n in order:
        if n in small_names:
            continue
        acc = None
        for j in range(weights[n].shape[0]):
            acc = _adamw_shard(weights[n], mom1[n], mom2[n], j, *grads_big[(n, j)], chip_arr, acc, f"adamw_{n}")
        out_g[n], out_d[n], out_m[n], out_v[n] = acc

    return (loss, grad_x, *[out_g[n] for n in order], *[out_d[n] for n in order],
            *[out_m[n] for n in order], *[out_v[n] for n in order])
```

```python
import functools

import jax
import jax.numpy as jnp
from jax import lax
from jax.experimental import pallas as pl
from jax.experimental.pallas import tpu as pltpu

F32 = jnp.float32
BF16 = jnp.bfloat16

EPS = 1e-6
CHUNK = 128
HEAD_DIM = 128
SG_GROUPS = 8
HALO = 16
N_DEV = 8
LANES = 128
V7X_VMEM_LIMIT = 56 * 1024 * 1024

ADAM_LR = 0.001
ADAM_B1 = 0.9
ADAM_B2 = 0.999
ADAM_EPS = 1e-08
ADAM_WD = 0.01
ADAM_STEP = 10

MESH = pl.DeviceIdType.MESH
ANY = pl.BlockSpec(memory_space=pl.ANY)


def _tile(n, prefs):
    for p in prefs:
        if p <= n and n % p == 0:
            return p
    return n


def _params(*sem):
    return pltpu.CompilerParams(dimension_semantics=sem, vmem_limit_bytes=V7X_VMEM_LIMIT)


def _shard_of(ref, dev, axis, r, c):
    if axis == 0:
        return ref.at[pl.ds(dev * r, r), :]
    return ref.at[:, pl.ds(dev * c, c)]


class _GatherJob:
    def __init__(self, x, axis):
        self.axis = axis
        self.r, self.c = x.shape
        full = (N_DEV * self.r, self.c) if axis == 0 else (self.r, N_DEV * self.c)
        self.inputs = [x]
        self.out_shapes = [jax.ShapeDtypeStruct(full, x.dtype)]
        self.scratch = [pltpu.SemaphoreType.DMA((7,)), pltpu.SemaphoreType.DMA((7,)), pltpu.SemaphoreType.DMA(())]

    def _plan(self, ins, outs, scr):
        (x_ref,), (out_ref,), (send_sems, recv_sems, local_sem) = ins, outs, scr
        mx, my, mc = lax.axis_index("x"), lax.axis_index("y"), lax.axis_index("c")
        me, sibling = (mx, my, mc), (mx, my, 1 - mc)
        chips = [(1 - mx, my), (mx, 1 - my), (1 - mx, 1 - my)]

        def rows(px, py, pc):
            return _shard_of(out_ref, 4 * px + 2 * py + pc, self.axis, self.r, self.c)

        def copy(k, block, to, src=None):
            return pltpu.make_async_remote_copy(
                src_ref=rows(*block) if src is None else src, dst_ref=rows(*block),
                send_sem=send_sems.at[k], recv_sem=recv_sems.at[k], device_id=to, device_id_type=MESH)

        mine = pltpu.make_async_copy(x_ref, rows(*me), local_sem)
        first = [copy(0, me, sibling, src=x_ref)] + [copy(1 + j, me, (*chip, mc), src=x_ref)
                                                      for j, chip in enumerate(chips)]
        passed = [copy(4 + j, (*chip, mc), sibling) for j, chip in enumerate(chips)]
        landed = [copy(1 + j, (*chip, mc), me) for j, chip in enumerate(chips)]
        from_sibling = [copy(0, sibling, me)] + [copy(4 + j, (*chip, 1 - mc), me) for j, chip in enumerate(chips)]
        return mine, first, passed, landed, from_sibling

    def start(self, ins, outs, scr):
        mine, first, _, _, _ = self._plan(ins, outs, scr)
        mine.start()
        for cp in first:
            cp.start()

    def finish(self, ins, outs, scr):
        mine, first, passed, landed, from_sibling = self._plan(ins, outs, scr)
        for j in range(3):
            landed[j].wait_recv()
            passed[j].start()
        for cp in from_sibling:
            cp.wait_recv()
        for cp in first + passed:
            cp.wait_send()
        mine.wait()


class _ChipExchangeJob:
    def __init__(self, p):
        _, r, c = p.shape
        self.inputs = [p]
        self.out_shapes = [jax.ShapeDtypeStruct((3, r, c), p.dtype)]
        self.scratch = [pltpu.SemaphoreType.DMA((3,)), pltpu.SemaphoreType.DMA((3,))]

    def _plan(self, ins, outs, scr):
        (p_ref,), (recv_ref,), (send_sems, recv_sems) = ins, outs, scr
        mx, my, mc = lax.axis_index("x"), lax.axis_index("y"), lax.axis_index("c")
        chips = [(1 - mx, my), (mx, 1 - my), (1 - mx, 1 - my)]
        return [pltpu.make_async_remote_copy(
            src_ref=p_ref.at[2 * px + py], dst_ref=recv_ref.at[j],
            send_sem=send_sems.at[j], recv_sem=recv_sems.at[j], device_id=(px, py, mc), device_id_type=MESH)
            for j, (px, py) in enumerate(chips)]

    def start(self, ins, outs, scr):
        for cp in self._plan(ins, outs, scr):
            cp.start()

    def finish(self, ins, outs, scr):
        for cp in self._plan(ins, outs, scr):
            cp.wait()


class _Hosted:
    def __init__(self, jobs):
        self.jobs = list(jobs)
        self.inputs = [x for j in self.jobs for x in j.inputs]
        self.out_shapes = [s for j in self.jobs for s in j.out_shapes]
        self.scratch = [s for j in self.jobs for s in j.scratch]
        self.in_specs = [ANY] * len(self.inputs)
        self.out_specs = [ANY] * len(self.out_shapes)

    def split(self, refs, n_in, n_out, n_scratch):
        refs = list(refs)
        ji, jo, js = len(self.inputs), len(self.out_shapes), len(self.scratch)
        ins, refs = refs[:n_in], refs[n_in:]
        jins, refs = refs[:ji], refs[ji:]
        outs, refs = refs[:n_out], refs[n_out:]
        jouts, refs = refs[:jo], refs[jo:]
        scr, jscr = refs[:n_scratch], refs[n_scratch:]
        assert len(jscr) == js
        per_job = []
        for j in self.jobs:
            a, b, c = len(j.inputs), len(j.out_shapes), len(j.scratch)
            per_job.append((jins[:a], jouts[:b], jscr[:c]))
            jins, jouts, jscr = jins[a:], jouts[b:], jscr[c:]
        return ins, outs, scr, per_job

    def start(self, per_job, when):
        if self.jobs:
            @pl.when(when)
            def _():
                for j, refs in zip(self.jobs, per_job):
                    j.start(*refs)

    def finish(self, per_job, when):
        if self.jobs:
            @pl.when(when)
            def _():
                for j, refs in zip(self.jobs, per_job):
                    j.finish(*refs)


def _grid_edges(grid):
    first = last = None
    for ax, n in enumerate(grid):
        f, l = pl.program_id(ax) == 0, pl.program_id(ax) == n - 1
        first = f if first is None else jnp.logical_and(first, f)
        last = l if last is None else jnp.logical_and(last, l)
    return first, last


def _run_jobs(jobs, name):
    hosted = _Hosted(jobs)

    def body(*refs):
        _, _, _, per_job = hosted.split(refs, 0, 0, 0)
        for j, r in zip(hosted.jobs, per_job):
            j.start(*r)
        for j, r in zip(hosted.jobs, per_job):
            j.finish(*r)

    return pl.pallas_call(
        body, name=name, out_shape=tuple(hosted.out_shapes), in_specs=hosted.in_specs,
        out_specs=tuple(hosted.out_specs), scratch_shapes=hosted.scratch,
    )(*hosted.inputs)


def _mm_nn(a, b, out_dtype, nsplit, name, jobs=()):
    m, k = a.shape
    n = b.shape[1]
    w = n // nsplit
    tm = _tile(m, (2048, 1024, 512, 256, 128)) if k <= 2048 else _tile(m, (1024, 512, 256, 128))
    tn = _tile(w, (512, 256, 128))
    per = w // tn
    grid = (m // tm, n // tn)
    hosted = _Hosted(jobs)

    def body(*refs):
        (a_ref, b_ref), (o_ref,), _, per_job = hosted.split(refs, 2, 1, 0)
        first, last = _grid_edges(grid)
        hosted.start(per_job, first)
        o_ref[...] = jnp.dot(a_ref[...], b_ref[...], preferred_element_type=F32).astype(o_ref.dtype)
        hosted.finish(per_job, last)

    if nsplit == 1:
        out_shape = jax.ShapeDtypeStruct((m, n), out_dtype)
        out_spec = pl.BlockSpec((tm, tn), lambda i, j: (i, j))
    else:
        out_shape = jax.ShapeDtypeStruct((nsplit, m, w), out_dtype)
        out_spec = pl.BlockSpec((None, tm, tn), lambda i, j: (j // per, i, j % per))
    res = pl.pallas_call(
        body, grid=grid, name=name,
        in_specs=[pl.BlockSpec((tm, k), lambda i, j: (i, 0)), pl.BlockSpec((k, tn), lambda i, j: (0, j))]
        + hosted.in_specs,
        out_specs=tuple([out_spec] + hosted.out_specs), out_shape=tuple([out_shape] + hosted.out_shapes),
        scratch_shapes=hosted.scratch,
        compiler_params=_params("arbitrary", "arbitrary"),
    )(a, b, *hosted.inputs)
    return res if jobs else res[0]


def _mm_nt(a, b, out_dtype, name, jobs=()):
    a3 = a if a.ndim == 3 else a[None]
    ns, m, w = a3.shape
    ko, n = b.shape
    assert n == ns * w
    tm = _tile(m, (1024, 512, 256, 128)) if n <= 2048 else _tile(m, (512, 256, 128))
    tko = _tile(ko, (256, 128)) if n > 8192 else _tile(ko, (512, 256, 128))
    grid = (m // tm, ko // tko)
    hosted = _Hosted(jobs)

    def body(*refs):
        (a_ref, b_ref), (o_ref,), _, per_job = hosted.split(refs, 2, 1, 0)
        first, last = _grid_edges(grid)
        hosted.start(per_job, first)
        acc = None
        for s in range(ns):
            p = lax.dot_general(a_ref[s], b_ref[:, s * w:(s + 1) * w], (((1,), (1,)), ((), ())),
                                preferred_element_type=F32)
            acc = p if acc is None else acc + p
        o_ref[...] = acc.astype(o_ref.dtype)
        hosted.finish(per_job, last)

    res = pl.pallas_call(
        body, grid=grid, name=name,
        in_specs=[pl.BlockSpec((ns, tm, w), lambda i, j: (0, i, 0)), pl.BlockSpec((tko, n), lambda i, j: (j, 0))]
        + hosted.in_specs,
        out_specs=tuple([pl.BlockSpec((tm, tko), lambda i, j: (i, j))] + hosted.out_specs),
        out_shape=tuple([jax.ShapeDtypeStruct((m, ko), out_dtype)] + hosted.out_shapes),
        scratch_shapes=hosted.scratch,
        compiler_params=_params("arbitrary", "arbitrary"),
    )(a3, b, *hosted.inputs)
    return res if jobs else res[0]


def _mm_tn(a, b, out_dtype, name, jobs=()):
    b3 = b if b.ndim == 3 else b[None]
    ns, m, w = b3.shape
    k = a.shape[1]
    n = ns * w
    tk = _tile(k, (512, 256, 128))
    tn = _tile(w, (2816, 2048, 1024, 512, 256, 128))
    ts = _tile(m, (2048, 1024, 512, 256, 128))
    per = w // tn
    n_s = m // ts
    grid = (k // tk, n // tn, n_s)
    hosted = _Hosted(jobs)

    def body(*refs):
        (a_ref, b_ref), (o_ref,), (acc_ref,), per_job = hosted.split(refs, 2, 1, 1)
        s = pl.program_id(2)
        first, last = _grid_edges(grid)
        hosted.start(per_job, first)

        @pl.when(s == 0)
        def _():
            acc_ref[...] = jnp.zeros_like(acc_ref)

        acc_ref[...] += lax.dot_general(a_ref[...], b_ref[...], (((0,), (0,)), ((), ())),
                                        preferred_element_type=F32)

        @pl.when(s == n_s - 1)
        def _():
            o_ref[...] = acc_ref[...].astype(o_ref.dtype)

        hosted.finish(per_job, last)

    res = pl.pallas_call(
        body, grid=grid, name=name,
        in_specs=[pl.BlockSpec((ts, tk), lambda i, j, s: (s, i)),
                  pl.BlockSpec((None, ts, tn), lambda i, j, s: (j // per, s, j % per))] + hosted.in_specs,
        out_specs=tuple([pl.BlockSpec((tk, tn), lambda i, j, s: (i, j))] + hosted.out_specs),
        out_shape=tuple([jax.ShapeDtypeStruct((k, n), out_dtype)] + hosted.out_shapes),
        scratch_shapes=[pltpu.VMEM((tk, tn), F32)] + hosted.scratch,
        compiler_params=_params("arbitrary", "arbitrary", "arbitrary"),
    )(a, b3, *hosted.inputs)
    return res if jobs else res[0]


def _rms_rows(s):
    return _tile(s, (256, 128))


def _rms_fwd(x, g, name):
    s, d = x.shape
    ts = _rms_rows(s)

    def body(x_ref, g_ref, o_ref):
        xf = x_ref[...]
        r = lax.rsqrt(jnp.mean(xf * xf, axis=-1, keepdims=True) + EPS)
        o_ref[...] = ((xf * r) * g_ref[...]).astype(o_ref.dtype)

    return pl.pallas_call(
        body, grid=(s // ts,), name=name,
        in_specs=[pl.BlockSpec((ts, d), lambda i: (i, 0)), pl.BlockSpec((1, d), lambda i: (0, 0))],
        out_specs=pl.BlockSpec((ts, d), lambda i: (i, 0)),
        out_shape=jax.ShapeDtypeStruct((s, d), BF16),
        compiler_params=_params("parallel"),
    )(x, g)


def _resid_rms_fwd(h, m, g, g_next, name):
    s, d = h.shape
    ts = _rms_rows(s)
    both = g_next is not None

    def body(*refs):
        h_ref, m_ref, g_ref = refs[:3]
        mf = m_ref[...]
        r = lax.rsqrt(jnp.mean(mf * mf, axis=-1, keepdims=True) + EPS)
        hf = h_ref[...] + (mf * r) * g_ref[...]
        refs[-2 if both else -1][...] = hf
        if both:
            r2 = lax.rsqrt(jnp.mean(hf * hf, axis=-1, keepdims=True) + EPS)
            refs[-1][...] = ((hf * r2) * refs[3][...]).astype(BF16)

    row = pl.BlockSpec((ts, d), lambda i: (i, 0))
    vec = pl.BlockSpec((1, d), lambda i: (0, 0))
    res = pl.pallas_call(
        body, grid=(s // ts,), name=name,
        in_specs=[row, row, vec] + ([vec] if both else []),
        out_specs=(row, row) if both else (row,),
        out_shape=(jax.ShapeDtypeStruct((s, d), F32),) + ((jax.ShapeDtypeStruct((s, d), BF16),) if both else ()),
        compiler_params=_params("parallel"),
    )(*([h, m, g] + ([g_next] if both else [])))
    return res if both else (res[0], None)


def _rms_bwd(x, g, dy, dres, out_dtype, name):
    s, d = x.shape
    ts = _rms_rows(s)
    has_res = dres is not None

    def body(*refs):
        if has_res:
            x_ref, g_ref, dy_ref, dres_ref, dx_ref, dg_ref = refs
        else:
            x_ref, g_ref, dy_ref, dx_ref, dg_ref = refs
        i = pl.program_id(0)
        xf = x_ref[...]
        dyf = dy_ref[...].astype(F32)
        r = lax.rsqrt(jnp.mean(xf * xf, axis=-1, keepdims=True) + EPS)
        xh = xf * r
        dxh = dyf * g_ref[...]
        dx = r * (dxh - xh * jnp.mean(dxh * xh, axis=-1, keepdims=True))
        if has_res:
            dx = dx + dres_ref[...]
        dx_ref[...] = dx.astype(dx_ref.dtype)

        @pl.when(i == 0)
        def _():
            dg_ref[...] = jnp.zeros_like(dg_ref)

        dg_ref[...] += jnp.sum(dyf * xh, axis=0, keepdims=True)

    row = pl.BlockSpec((ts, d), lambda i: (i, 0))
    vec = pl.BlockSpec((1, d), lambda i: (0, 0))
    ins = [x, g, dy] + ([dres] if has_res else [])
    return pl.pallas_call(
        body, grid=(s // ts,), name=name,
        in_specs=[row, vec, row] + ([row] if has_res else []),
        out_specs=(row, vec),
        out_shape=(jax.ShapeDtypeStruct((s, d), out_dtype), jax.ShapeDtypeStruct((1, d), F32)),
        compiler_params=_params("arbitrary"),
    )(*ins)


def _loss_head(h, target, name):
    s, d = h.shape
    ts = _rms_rows(s)
    n_i = s // ts

    def body(h_ref, t_ref, dh_ref, loss_ref, acc_ref):
        i = pl.program_id(0)
        err = h_ref[...] - t_ref[...]
        dh_ref[...] = err * (1.0 / d)

        @pl.when(i == 0)
        def _():
            acc_ref[...] = jnp.zeros_like(acc_ref)

        acc_ref[...] += jnp.sum(err * err, axis=0, keepdims=True)

        @pl.when(i == n_i - 1)
        def _():
            loss_ref[...] = jnp.sum(acc_ref[...], axis=1, keepdims=True) * (0.5 / d)

    row = pl.BlockSpec((ts, d), lambda i: (i, 0))
    return pl.pallas_call(
        body, grid=(n_i,), name=name,
        in_specs=[row, row],
        out_specs=(row, pl.BlockSpec((1, 1), lambda i: (0, 0))),
        out_shape=(jax.ShapeDtypeStruct((s, d), F32), jax.ShapeDtypeStruct((1, 1), F32)),
        scratch_shapes=[pltpu.VMEM((1, d), F32)],
        compiler_params=_params("arbitrary"),
    )(h, target)


def _halo_specs(ns, s, ts, tc):
    hb = ts // HALO
    last = s // HALO - 1
    cur = pl.BlockSpec((ns, ts, tc), lambda j, i: (0, i, j))
    prev = pl.BlockSpec((ns, HALO, tc), lambda j, i: (0, jnp.maximum(i * hb - 1, 0), j))
    nxt = pl.BlockSpec((ns, HALO, tc), lambda j, i: (0, jnp.minimum((i + 1) * hb, last), j))
    return cur, prev, nxt


def _ext(prev, cur, nxt, first, last):
    p = jnp.where(first, 0.0, prev.astype(F32))
    n = jnp.where(last, 0.0, nxt.astype(F32))
    return jnp.concatenate([p, cur.astype(F32), n], axis=0)


def _dn(xe, k):
    return xe if k == 0 else pltpu.roll(xe, k, 0)


def _up(xe, k):
    return xe if k == 0 else pltpu.roll(xe, xe.shape[0] - k, 0)


def _conv_e(xe, w):
    return w[0:1, :] * _dn(xe, 2) + w[1:2, :] * _dn(xe, 1) + w[2:3, :] * xe


def _conv_t_e(de, w):
    return w[2:3, :] * de + w[1:2, :] * _up(de, 1) + w[0:1, :] * _up(de, 2)


def _conv_dw(de, xe, ts):
    return [jnp.sum((de * _dn(xe, 2 - k))[HALO:HALO + ts], axis=0, keepdims=True) for k in range(3)]


def _conv_tiles(s, c):
    return _tile(s, (512, 256, 128)), _tile(c, (512, 256, 128))


def _sigmoid(x):
    return pl.reciprocal(1.0 + jnp.exp(-x), approx=True)


def _sc_gate_fwd(p, w, name):
    _, s, c = p.shape
    ts, tc = _conv_tiles(s, c)
    n_i = s // ts

    def body(p_ref, pp_ref, w_ref, y_ref):
        i = pl.program_id(1)
        ze = _ext(pp_ref[1], p_ref[1], pp_ref[1], i == 0, True) * _ext(pp_ref[2], p_ref[2], pp_ref[2], i == 0, True)
        cz = _conv_e(ze, w_ref[...])[HALO:HALO + ts]
        y_ref[...] = (p_ref[0].astype(F32) * cz).astype(y_ref.dtype)

    cur, prev, _ = _halo_specs(3, s, ts, tc)
    return pl.pallas_call(
        body, grid=(c // tc, n_i), name=name,
        in_specs=[cur, prev, pl.BlockSpec((3, tc), lambda j, i: (0, j))],
        out_specs=pl.BlockSpec((ts, tc), lambda j, i: (i, j)),
        out_shape=jax.ShapeDtypeStruct((s, c), BF16),
        compiler_params=_params("parallel", "arbitrary"),
    )(p, p, w)


def _sc_gate_bwd(p, dy, w, name):
    _, s, c = p.shape
    ts, tc = _conv_tiles(s, c)
    n_i = s // ts

    def body(p_ref, pp_ref, pn_ref, dy_ref, dyn_ref, w_ref, dp_ref, dw_ref):
        i = pl.program_id(1)
        first, last = i == 0, i == n_i - 1
        wv = w_ref[...]
        gbe = _ext(pp_ref[0], p_ref[0], pn_ref[0], first, last)
        gce = _ext(pp_ref[1], p_ref[1], pn_ref[1], first, last)
        hhe = _ext(pp_ref[2], p_ref[2], pn_ref[2], first, last)
        dye = _ext(dyn_ref[...], dy_ref[...], dyn_ref[...], True, last)
        ze = gce * hhe
        cze = _conv_e(ze, wv)
        dcze = dye * gbe
        dze = _conv_t_e(dcze, wv)
        rows = slice(HALO, HALO + ts)
        dp_ref[0] = (dye * cze)[rows].astype(dp_ref.dtype)
        dp_ref[1] = (dze * hhe)[rows].astype(dp_ref.dtype)
        dp_ref[2] = (dze * gce)[rows].astype(dp_ref.dtype)

        @pl.when(first)
        def _():
            dw_ref[...] = jnp.zeros_like(dw_ref)

        dws = _conv_dw(dcze, ze, ts)
        for k in range(3):
            dw_ref[k:k + 1, :] += dws[k]

    cur, prev, nxt = _halo_specs(3, s, ts, tc)
    hb = ts // HALO
    last_h = s // HALO - 1
    return pl.pallas_call(
        body, grid=(c // tc, n_i), name=name,
        in_specs=[cur, prev, nxt,
                  pl.BlockSpec((ts, tc), lambda j, i: (i, j)),
                  pl.BlockSpec((HALO, tc), lambda j, i: (jnp.minimum((i + 1) * hb, last_h), j)),
                  pl.BlockSpec((3, tc), lambda j, i: (0, j))],
        out_specs=(cur, pl.BlockSpec((3, tc), lambda j, i: (0, j))),
        out_shape=(jax.ShapeDtypeStruct((3, s, c), BF16), jax.ShapeDtypeStruct((3, c), F32)),
        compiler_params=_params("parallel", "arbitrary"),
    )(p, p, p, dy, dy, w)


def _ffn_act_fwd(u, w, b, name):
    _, s, f = u.shape
    ts, tc = _conv_tiles(s, f)
    n_i = s // ts

    def body(u_ref, up_ref, w_ref, b_ref, a_ref):
        i = pl.program_id(1)
        rows = slice(HALO, HALO + ts)
        cg = _conv_e(_ext(up_ref[0], u_ref[0], up_ref[0], i == 0, True), w_ref[0])[rows] + b_ref[0]
        cv = _conv_e(_ext(up_ref[1], u_ref[1], up_ref[1], i == 0, True), w_ref[1])[rows] + b_ref[1]
        a_ref[...] = (cg * _sigmoid(cg) * cv).astype(a_ref.dtype)

    cur, prev, _ = _halo_specs(2, s, ts, tc)
    return pl.pallas_call(
        body, grid=(f // tc, n_i), name=name,
        in_specs=[cur, prev, pl.BlockSpec((2, 3, tc), lambda j, i: (0, 0, j)),
                  pl.BlockSpec((2, 1, tc), lambda j, i: (0, 0, j))],
        out_specs=pl.BlockSpec((ts, tc), lambda j, i: (i, j)),
        out_shape=jax.ShapeDtypeStruct((s, f), BF16),
        compiler_params=_params("parallel", "arbitrary"),
    )(u, u, w, b)


def _ffn_act_bwd(u, da, w, b, name):
    _, s, f = u.shape
    ts, tc = _conv_tiles(s, f)
    n_i = s // ts

    def body(u_ref, up_ref, un_ref, da_ref, dan_ref, w_ref, b_ref, du_ref, dw_ref, db_ref):
        i = pl.program_id(1)
        first, last = i == 0, i == n_i - 1
        rows = slice(HALO, HALO + ts)
        uge = _ext(up_ref[0], u_ref[0], un_ref[0], first, last)
        uve = _ext(up_ref[1], u_ref[1], un_ref[1], first, last)
        dae = _ext(dan_ref[...], da_ref[...], dan_ref[...], True, last)
        cg = _conv_e(uge, w_ref[0]) + b_ref[0]
        cv = _conv_e(uve, w_ref[1]) + b_ref[1]
        sg = _sigmoid(cg)
        dcg = dae * cv * (sg * (1.0 + cg * (1.0 - sg)))
        dcv = dae * (cg * sg)
        du_ref[0] = _conv_t_e(dcg, w_ref[0])[rows].astype(du_ref.dtype)
        du_ref[1] = _conv_t_e(dcv, w_ref[1])[rows].astype(du_ref.dtype)

        @pl.when(first)
        def _():
            dw_ref[...] = jnp.zeros_like(dw_ref)
            db_ref[...] = jnp.zeros_like(db_ref)

        for h, (de, xe) in enumerate(((dcg, uge), (dcv, uve))):
            dws = _conv_dw(de, xe, ts)
            for k in range(3):
                dw_ref[h, k:k + 1, :] += dws[k]
            db_ref[h] += jnp.sum(de[rows], axis=0, keepdims=True)

    cur, prev, nxt = _halo_specs(2, s, ts, tc)
    hb = ts // HALO
    last_h = s // HALO - 1
    wspec = pl.BlockSpec((2, 3, tc), lambda j, i: (0, 0, j))
    bspec = pl.BlockSpec((2, 1, tc), lambda j, i: (0, 0, j))
    return pl.pallas_call(
        body, grid=(f // tc, n_i), name=name,
        in_specs=[cur, prev, nxt,
                  pl.BlockSpec((ts, tc), lambda j, i: (i, j)),
                  pl.BlockSpec((HALO, tc), lambda j, i: (jnp.minimum((i + 1) * hb, last_h), j)),
                  wspec, bspec],
        out_specs=(cur, wspec, bspec),
        out_shape=(jax.ShapeDtypeStruct((2, s, f), BF16), jax.ShapeDtypeStruct((2, 3, f), F32),
                   jax.ShapeDtypeStruct((2, 1, f), F32)),
        compiler_params=_params("parallel", "arbitrary"),
    )(u, u, u, da, da, w, b)


_INV_SQRT2 = 0.7071067811865476
_INV_SQRT_2PI = 0.3989422804014327


def _gelu(x):
    return 0.5 * x * (1.0 + lax.erf(x * _INV_SQRT2))


def _gelu_grad(x):
    return 0.5 * (1.0 + lax.erf(x * _INV_SQRT2)) + x * (_INV_SQRT_2PI * jnp.exp(-0.5 * x * x))


def _tril_bf16(w):
    t = lax.broadcasted_iota(jnp.int32, w.shape, 0)
    s = lax.broadcasted_iota(jnp.int32, w.shape, 1)
    return jnp.where(s <= t, w, 0.0).astype(BF16)


def _sgu_fwd(p, ln_g, ln_b, ws, bs, name):
    _, s, c = p.shape
    g_n = ws.shape[0]
    cg = c // g_n

    def body(p_ref, lg_ref, lb_ref, ws_ref, bs_ref, y_ref):
        u = _gelu(p_ref[0].astype(F32))
        v = _gelu(p_ref[1].astype(F32))
        mu = jnp.mean(v, axis=-1, keepdims=True)
        vc = v - mu
        rstd = lax.rsqrt(jnp.mean(vc * vc, axis=-1, keepdims=True) + EPS)
        vn = ((vc * rstd) * lg_ref[...] + lb_ref[...]).astype(BF16)
        for g in range(g_n):
            cols = slice(g * cg, (g + 1) * cg)
            mixed = jnp.dot(_tril_bf16(ws_ref[g]), vn[:, cols], preferred_element_type=F32) + bs_ref[g]
            y_ref[:, cols] = (u[:, cols] * mixed).astype(y_ref.dtype)

    vec = pl.BlockSpec((1, c), lambda i: (0, 0))
    return pl.pallas_call(
        body, grid=(s // CHUNK,), name=name,
        in_specs=[pl.BlockSpec((2, CHUNK, c), lambda i: (0, i, 0)), vec, vec,
                  pl.BlockSpec((g_n, CHUNK, CHUNK), lambda i: (0, 0, 0)),
                  pl.BlockSpec((g_n, CHUNK, 1), lambda i: (0, 0, 0))],
        out_specs=pl.BlockSpec((CHUNK, c), lambda i: (i, 0)),
        out_shape=jax.ShapeDtypeStruct((s, c), BF16),
        compiler_params=_params("parallel"),
    )(p, ln_g, ln_b, ws, bs)


def _sgu_bwd(p, dy, ln_g, ln_b, ws, ws_t, bs, name):
    _, s, c = p.shape
    g_n = ws.shape[0]
    cg = c // g_n

    def body(p_ref, dy_ref, lg_ref, lb_ref, ws_ref, wst_ref, bs_ref, dp_ref, dlg_ref, dlb_ref, dws_ref, dbs_ref,
             dvn_ref):
        i = pl.program_id(0)
        pu = p_ref[0].astype(F32)
        pv = p_ref[1].astype(F32)
        dyf = dy_ref[...].astype(F32)
        u = _gelu(pu)
        v = _gelu(pv)
        mu = jnp.mean(v, axis=-1, keepdims=True)
        vc = v - mu
        rstd = lax.rsqrt(jnp.mean(vc * vc, axis=-1, keepdims=True) + EPS)
        vhat = vc * rstd
        vn = (vhat * lg_ref[...] + lb_ref[...]).astype(BF16)

        @pl.when(i == 0)
        def _():
            dlg_ref[...] = jnp.zeros_like(dlg_ref)
            dlb_ref[...] = jnp.zeros_like(dlb_ref)
            dws_ref[...] = jnp.zeros_like(dws_ref)
            dbs_ref[...] = jnp.zeros_like(dbs_ref)

        t_i = lax.broadcasted_iota(jnp.int32, (CHUNK, CHUNK), 0)
        s_i = lax.broadcasted_iota(jnp.int32, (CHUNK, CHUNK), 1)
        for g in range(g_n):
            cols = slice(g * cg, (g + 1) * cg)
            mixed = jnp.dot(_tril_bf16(ws_ref[g]), vn[:, cols], preferred_element_type=F32) + bs_ref[g]
            dp_ref[0, :, cols] = (dyf[:, cols] * mixed * _gelu_grad(pu[:, cols])).astype(dp_ref.dtype)
            dmixed = dyf[:, cols] * u[:, cols]
            dmb = dmixed.astype(BF16)
            wt = jnp.where(t_i <= s_i, wst_ref[g], 0.0).astype(BF16)
            dvn_ref[:, cols] = jnp.dot(wt, dmb, preferred_element_type=F32)
            dwg = lax.dot_general(dmb, vn[:, cols], (((1,), (1,)), ((), ())), preferred_element_type=F32)
            dws_ref[g] += jnp.where(s_i <= t_i, dwg, 0.0)
            dbs_ref[g] += jnp.sum(dmixed, axis=1, keepdims=True)
        dvn = dvn_ref[...]
        dlg_ref[...] += jnp.sum(dvn * vhat, axis=0, keepdims=True)
        dlb_ref[...] += jnp.sum(dvn, axis=0, keepdims=True)
        dvh = dvn * lg_ref[...]
        dv = rstd * (dvh - jnp.mean(dvh, axis=-1, keepdims=True) - vhat * jnp.mean(dvh * vhat, axis=-1, keepdims=True))
        dp_ref[1] = (dv * _gelu_grad(pv)).astype(dp_ref.dtype)

    vec = pl.BlockSpec((1, c), lambda i: (0, 0))
    wspec = pl.BlockSpec((g_n, CHUNK, CHUNK), lambda i: (0, 0, 0))
    bspec = pl.BlockSpec((g_n, CHUNK, 1), lambda i: (0, 0, 0))
    pspec = pl.BlockSpec((2, CHUNK, c), lambda i: (0, i, 0))
    return pl.pallas_call(
        body, grid=(s // CHUNK,), name=name,
        in_specs=[pspec, pl.BlockSpec((CHUNK, c), lambda i: (i, 0)), vec, vec, wspec, wspec, bspec],
        out_specs=(pspec, vec, vec, wspec, bspec),
        out_shape=(jax.ShapeDtypeStruct((2, s, c), BF16), jax.ShapeDtypeStruct((1, c), F32),
                   jax.ShapeDtypeStruct((1, c), F32), jax.ShapeDtypeStruct((g_n, CHUNK, CHUNK), F32),
                   jax.ShapeDtypeStruct((g_n, CHUNK, 1), F32)),
        scratch_shapes=[pltpu.VMEM((CHUNK, c), F32)],
        compiler_params=_params("arbitrary"),
    )(p, dy, ln_g, ln_b, ws, ws_t, bs)


SB_Q = 512


def _sb_cumsum(x, tri, edge, reverse):
    n = x.shape[0]
    hi = x.astype(BF16)
    lo = (x - hi.astype(F32)).astype(BF16)
    chunks = x.shape[1] // CHUNK
    outs = [None] * chunks
    for c in (reversed(range(chunks)) if reverse else range(chunks)):
        cols = slice(c * CHUNK, (c + 1) * CHUNK)
        both = jnp.dot(jnp.concatenate([hi[:, cols], lo[:, cols]], axis=0), tri, preferred_element_type=F32)
        outs[c] = both[:n] + both[n:] + edge
        edge = edge + jnp.sum(x[:, cols], axis=1, keepdims=True)
    return jnp.concatenate(outs, axis=1), edge


def _sb_block(q, k, offset):
    scale = HEAD_DIM ** -0.5
    z = lax.dot_general(q, k, (((1,), (1,)), ((), ())), preferred_element_type=F32) * scale
    e = jnp.exp(-jnp.abs(z))
    lp = jnp.log(1.0 + e)
    lb = jnp.minimum(z, 0.0) - lp
    l1 = lb - z
    if offset is not None:
        t_i = lax.broadcasted_iota(jnp.int32, z.shape, 0)
        s_i = lax.broadcasted_iota(jnp.int32, z.shape, 1)
        mask = (t_i - s_i) > offset
        l1 = jnp.where(mask, l1, 0.0)
    else:
        mask = None
    return z, e, lb, l1, mask


def _tri(cond):
    j = lax.broadcasted_iota(jnp.int32, (CHUNK, CHUNK), 0)
    s = lax.broadcasted_iota(jnp.int32, (CHUNK, CHUNK), 1)
    return jnp.where(cond(j, s), 1.0, 0.0).astype(BF16)


def _sb_fwd(qkv, name):
    _, s, c = qkv.shape
    heads = c // HEAD_DIM
    bq = _tile(s, (SB_Q, 256, CHUNK))
    nq = s // bq

    def body(q_ref, k_ref, v_ref, o_ref, runs_ref):
        i = pl.program_id(1)
        q = q_ref[...]
        after = _tri(lambda j, s_: j > s_)
        runs_ref[...] = jnp.zeros_like(runs_ref)

        def tile(g, carry, offset):
            o_acc, run = carry
            rows = pl.ds(pl.multiple_of(g * bq, bq), bq)
            runs_ref[pl.ds(pl.multiple_of(g * 8, 8), 8), :] = jnp.transpose(jnp.broadcast_to(run, (bq, LANES)))[:8]
            _, _, lb, l1, mask = _sb_block(q, k_ref[rows, :], offset)
            acc, run = _sb_cumsum(l1, after, run, True)
            a = jnp.exp(lb + acc)
            if mask is not None:
                a = jnp.where(mask, a, 0.0)
            o_acc = o_acc + jnp.dot(a.astype(BF16), v_ref[rows, :], preferred_element_type=F32)
            return o_acc, run

        carry = tile(i, (jnp.zeros((bq, HEAD_DIM), F32), jnp.zeros((bq, 1), F32)), 0)
        o_acc, _ = lax.fori_loop(0, i, lambda n, cr: tile(i - 1 - n, cr, None), carry)
        o_ref[...] = o_acc.astype(o_ref.dtype)

    return pl.pallas_call(
        body, grid=(heads, nq), name=name,
        in_specs=[pl.BlockSpec((None, bq, HEAD_DIM), lambda h, i: (0, i, h)),
                  pl.BlockSpec((None, s, HEAD_DIM), lambda h, i: (1, 0, h)),
                  pl.BlockSpec((None, s, HEAD_DIM), lambda h, i: (2, 0, h))],
        out_specs=(pl.BlockSpec((bq, HEAD_DIM), lambda h, i: (i, h)),
                   pl.BlockSpec((None, None, 8 * nq, bq), lambda h, i: (h, i, 0, 0))),
        out_shape=(jax.ShapeDtypeStruct((s, c), BF16), jax.ShapeDtypeStruct((heads, nq, 8 * nq, bq), F32)),
        compiler_params=_params("parallel", "arbitrary"),
    )(qkv, qkv, qkv)


def _sb_bwd(qkv, do, runs, name):
    _, s, c = qkv.shape
    heads = c // HEAD_DIM
    bq = _tile(s, (SB_Q, 256, CHUNK))
    nq = s // bq
    scale = HEAD_DIM ** -0.5
    tn_dims = (((0,), (0,)), ((), ()))

    def body(q_ref, k_ref, v_ref, do_ref, runs_ref, dq_ref, dkv_ref, dk_acc, dv_acc):
        i = pl.program_id(1)
        q = q_ref[...]
        dob = do_ref[...]
        after = _tri(lambda j, s_: j > s_)
        before = _tri(lambda j, s_: j < s_)

        @pl.when(i == 0)
        def _():
            dk_acc[...] = jnp.zeros_like(dk_acc)
            dv_acc[...] = jnp.zeros_like(dv_acc)

        def main(g, carry, offset):
            dq_acc, csum = carry
            rows = pl.ds(pl.multiple_of(g * bq, bq), bq)
            kblk = k_ref[rows, :]
            vblk = v_ref[rows, :]
            z, e, lb, l1, mask = _sb_block(q, kblk, offset)
            run = jnp.transpose(jnp.broadcast_to(runs_ref[pl.ds(pl.multiple_of(g * 8, 8), 1), :], (LANES, bq)))
            acc, _ = _sb_cumsum(l1, after, run, True)
            a = jnp.exp(lb + acc)
            if mask is not None:
                a = jnp.where(mask, a, 0.0)
            d_a = lax.dot_general(dob, vblk, (((1,), (1,)), ((), ())), preferred_element_type=F32)
            ee = a * d_a
            cc, csum = _sb_cumsum(ee, before, csum, False)
            beta = jnp.exp(lb)
            dz = (ee - (ee + cc) * beta) * scale
            if mask is not None:
                dz = jnp.where(mask, dz, 0.0)
            dzb = dz.astype(BF16)
            dq_acc = dq_acc + jnp.dot(dzb, kblk, preferred_element_type=F32)
            dk_acc[rows, :] += lax.dot_general(dzb, q, tn_dims, preferred_element_type=F32)
            dv_acc[rows, :] += lax.dot_general(a.astype(BF16), dob, tn_dims, preferred_element_type=F32)
            return dq_acc, csum

        carry = (jnp.zeros((bq, HEAD_DIM), F32), jnp.zeros((bq, 1), F32))
        carry = lax.fori_loop(0, i, lambda g, cr: main(g, cr, None), carry)
        dq_acc, _ = main(i, carry, 0)
        dq_ref[...] = dq_acc.astype(dq_ref.dtype)

        @pl.when(i == nq - 1)
        def _():
            dkv_ref[0] = dk_acc[...].astype(dkv_ref.dtype)
            dkv_ref[1] = dv_acc[...].astype(dkv_ref.dtype)

    blk = pl.BlockSpec((bq, HEAD_DIM), lambda h, i: (i, h))
    return pl.pallas_call(
        body, grid=(heads, nq), name=name,
        in_specs=[pl.BlockSpec((None, bq, HEAD_DIM), lambda h, i: (0, i, h)),
                  pl.BlockSpec((None, s, HEAD_DIM), lambda h, i: (1, 0, h)),
                  pl.BlockSpec((None, s, HEAD_DIM), lambda h, i: (2, 0, h)),
                  blk,
                  pl.BlockSpec((None, None, 8 * nq, bq), lambda h, i: (h, i, 0, 0))],
        out_specs=(blk, pl.BlockSpec((2, s, HEAD_DIM), lambda h, i: (0, 0, h))),
        out_shape=(jax.ShapeDtypeStruct((s, c), BF16), jax.ShapeDtypeStruct((2, s, c), BF16)),
        scratch_shapes=[pltpu.VMEM((s, HEAD_DIM), F32), pltpu.VMEM((s, HEAD_DIM), F32)],
        compiler_params=_params("parallel", "arbitrary"),
    )(qkv, qkv, qkv, do, runs)


def _all_gather(x, axis, name):
    return _run_jobs([_GatherJob(x, axis)], name)[0]


def _rs_core_exchange(g, axis, name):
    r, c = (g.shape[0] // N_DEV, g.shape[1]) if axis == 0 else (g.shape[0], g.shape[1] // N_DEV)

    def body(g_ref, recv_ref, send_sems, recv_sems):
        mx, my, mc = lax.axis_index("x"), lax.axis_index("y"), lax.axis_index("c")
        copies = [pltpu.make_async_remote_copy(
            src_ref=_shard_of(g_ref, 2 * k + (1 - mc), axis, r, c), dst_ref=recv_ref.at[k],
            send_sem=send_sems.at[k], recv_sem=recv_sems.at[k], device_id=(mx, my, 1 - mc), device_id_type=MESH)
            for k in range(4)]
        for cp in copies:
            cp.start()
        for cp in copies:
            cp.wait()

    return pl.pallas_call(
        body, name=name,
        out_shape=jax.ShapeDtypeStruct((4, r, c), g.dtype),
        in_specs=[ANY], out_specs=ANY,
        scratch_shapes=[pltpu.SemaphoreType.DMA((4,)), pltpu.SemaphoreType.DMA((4,))],
    )(g)


def _ew_tiles(r, c):
    tc = c if c <= 2048 else _tile(c, (2048, 1024, 512, 256, 128))
    return _tile(r, [p for p in (512, 256, 128, 64, 32, 16, 8) if p * tc <= 256 * 1024]), tc


def _rs_core_add(g, recv, core, axis, name):
    _, r, c = recv.shape
    tr, tc = _ew_tiles(r, c)
    nrb, ncb = r // tr, c // tc

    def body(core_ref, g_ref, x_ref, o_ref):
        del core_ref
        o_ref[...] = (g_ref[...].astype(F32) + x_ref[...].astype(F32)).astype(o_ref.dtype)

    if axis == 0:
        g_spec = pl.BlockSpec((tr, tc), lambda k, i, j, core: ((2 * k + core[0]) * nrb + i, j))
    else:
        g_spec = pl.BlockSpec((tr, tc), lambda k, i, j, core: (i, (2 * k + core[0]) * ncb + j))
    blk = pl.BlockSpec((None, tr, tc), lambda k, i, j, core: (k, i, j))
    return pl.pallas_call(
        body, name=name,
        grid_spec=pltpu.PrefetchScalarGridSpec(
            num_scalar_prefetch=1, grid=(4, nrb, ncb), in_specs=[g_spec, blk], out_specs=blk),
        out_shape=jax.ShapeDtypeStruct(recv.shape, BF16),
        compiler_params=_params("parallel", "parallel", "parallel"),
    )(core, g, recv)


def _rs_chip_exchange(parts, name):
    return _run_jobs([_ChipExchangeJob(p) for p in parts], name)


def _adamw_math(w, g, m, v):
    m = ADAM_B1 * m + (1.0 - ADAM_B1) * g
    v = ADAM_B2 * v + (1.0 - ADAM_B2) * (g * g)
    m_hat = m / (1.0 - ADAM_B1 ** ADAM_STEP)
    v_hat = v / (1.0 - ADAM_B2 ** ADAM_STEP)
    delta = -ADAM_LR * (m_hat / (jnp.sqrt(v_hat) + ADAM_EPS) + ADAM_WD * w)
    return delta, m, v


def _adamw_shard(w, m, v, layer, p, recv, chip, acc, name):
    n_layers, r, c = w.shape
    tr, tc = _ew_tiles(r, c)
    if n_layers == 1:
        acc = ()
    elif acc is None:
        acc = tuple(lax.empty((n_layers, r, c), F32) for _ in range(4))

    def body(chip_ref, w_ref, m_ref, v_ref, p_ref, x_ref, *rest):
        del chip_ref
        g_out, d_out, m_out, v_out = rest[len(acc):]
        g = p_ref[...].astype(F32)
        for j in range(3):
            g = g + x_ref[j].astype(F32)
        d, mn, vn = _adamw_math(w_ref[...], g, m_ref[...], v_ref[...])
        g_out[...] = g
        d_out[...] = d
        m_out[...] = mn
        v_out[...] = vn

    lay = pl.BlockSpec((None, tr, tc), lambda i, j, chip: (layer, i, j))
    out = jax.ShapeDtypeStruct((n_layers, r, c), F32)
    return pl.pallas_call(
        body, name=name,
        grid_spec=pltpu.PrefetchScalarGridSpec(
            num_scalar_prefetch=1, grid=(r // tr, c // tc),
            in_specs=[lay, lay, lay, pl.BlockSpec((None, tr, tc), lambda i, j, chip: (chip[0], i, j)),
                      pl.BlockSpec((3, tr, tc), lambda i, j, chip: (0, i, j))] + [ANY] * len(acc),
            out_specs=(lay, lay, lay, lay)),
        out_shape=(out, out, out, out),
        input_output_aliases={6 + t: t for t in range(len(acc))},
        compiler_params=_params("parallel", "parallel"),
    )(chip, w, m, v, p, recv, *acc)


def _adamw_packed(w, m, v, g, name):
    r, c = w.shape
    tr = _tile(r, (512, 256, 128, 64, 32, 16, 8))

    def body(w_ref, m_ref, v_ref, g_ref, d_out, m_out, v_out):
        d, mn, vn = _adamw_math(w_ref[...], g_ref[...], m_ref[...], v_ref[...])
        d_out[...] = d
        m_out[...] = mn
        v_out[...] = vn

    blk = pl.BlockSpec((tr, c), lambda i: (i, 0))
    out = jax.ShapeDtypeStruct((r, c), F32)
    return pl.pallas_call(
        body, grid=(r // tr,), name=name, in_specs=[blk, blk, blk, blk], out_specs=(blk, blk, blk),
        out_shape=(out, out, out), compiler_params=_params("parallel"),
    )(w, m, v, g)


def _sum_devices(x, name):
    r = x.shape[0] // N_DEV
    tr = _tile(r, (512, 256, 128, 64, 32, 16, 8))

    def body(x_ref, o_ref):
        acc = x_ref[0]
        for d in range(1, N_DEV):
            acc = acc + x_ref[d]
        o_ref[...] = acc

    return pl.pallas_call(
        body, grid=(r // tr,), name=name,
        in_specs=[pl.BlockSpec((N_DEV, tr, LANES), lambda i: (0, i, 0))],
        out_specs=pl.BlockSpec((tr, LANES), lambda i: (i, 0)),
        out_shape=jax.ShapeDtypeStruct((r, LANES), F32),
        compiler_params=_params("parallel"),
    )(x.reshape(N_DEV, r, LANES))


def _pack(arrays):
    flat = []
    for a in arrays:
        f = a.reshape(-1).astype(F32)
        flat.append(jnp.pad(f, (0, (-f.shape[0]) % LANES)))
    f = jnp.concatenate(flat)
    f = jnp.pad(f, (0, (-f.shape[0]) % (8 * LANES)))
    return f.reshape(-1, LANES)


def _unpack(packed, shapes):
    flat = packed.reshape(-1)
    out, off = [], 0
    for shp in shapes:
        n = 1
        for d in shp:
            n *= d
        out.append(flat[off:off + n].reshape(shp))
        off += n + (-n) % LANES
    return out


def kernel(x, norm_mix_pre, norm_mix_post, norm_ffn_pre, norm_ffn_post, sc_w_in, sc_conv_w, sc_w_out, sg_w_in, sg_ln_g, sg_ln_b, sg_w_s, sg_b_s, sg_w_out, sb_w_qkv, sb_w_out, ffn_w_up, ffn_conv_w, ffn_conv_b, ffn_w_down, loss_target, m_norm_mix_pre, m_norm_mix_post, m_norm_ffn_pre, m_norm_ffn_post, m_sc_w_in, m_sc_conv_w, m_sc_w_out, m_sg_w_in, m_sg_ln_g, m_sg_ln_b, m_sg_w_s, m_sg_b_s, m_sg_w_out, m_sb_w_qkv, m_sb_w_out, m_ffn_w_up, m_ffn_conv_w, m_ffn_conv_b, m_ffn_w_down, v_norm_mix_pre, v_norm_mix_post, v_norm_ffn_pre, v_norm_ffn_post, v_sc_w_in, v_sc_conv_w, v_sc_w_out, v_sg_w_in, v_sg_ln_g, v_sg_ln_b, v_sg_w_s, v_sg_b_s, v_sg_w_out, v_sb_w_qkv, v_sb_w_out, v_ffn_w_up, v_ffn_conv_w, v_ffn_conv_b, v_ffn_w_down):
    weights = dict(norm_mix_pre=norm_mix_pre, norm_mix_post=norm_mix_post, norm_ffn_pre=norm_ffn_pre,
                   norm_ffn_post=norm_ffn_post, sc_w_in=sc_w_in, sc_conv_w=sc_conv_w, sc_w_out=sc_w_out,
                   sg_w_in=sg_w_in, sg_ln_g=sg_ln_g, sg_ln_b=sg_ln_b, sg_w_s=sg_w_s, sg_b_s=sg_b_s,
                   sg_w_out=sg_w_out, sb_w_qkv=sb_w_qkv, sb_w_out=sb_w_out, ffn_w_up=ffn_w_up,
                   ffn_conv_w=ffn_conv_w, ffn_conv_b=ffn_conv_b, ffn_w_down=ffn_w_down)
    mom1 = dict(norm_mix_pre=m_norm_mix_pre, norm_mix_post=m_norm_mix_post, norm_ffn_pre=m_norm_ffn_pre,
                norm_ffn_post=m_norm_ffn_post, sc_w_in=m_sc_w_in, sc_conv_w=m_sc_conv_w, sc_w_out=m_sc_w_out,
                sg_w_in=m_sg_w_in, sg_ln_g=m_sg_ln_g, sg_ln_b=m_sg_ln_b, sg_w_s=m_sg_w_s, sg_b_s=m_sg_b_s,
                sg_w_out=m_sg_w_out, sb_w_qkv=m_sb_w_qkv, sb_w_out=m_sb_w_out, ffn_w_up=m_ffn_w_up,
                ffn_conv_w=m_ffn_conv_w, ffn_conv_b=m_ffn_conv_b, ffn_w_down=m_ffn_w_down)
    mom2 = dict(norm_mix_pre=v_norm_mix_pre, norm_mix_post=v_norm_mix_post, norm_ffn_pre=v_norm_ffn_pre,
                norm_ffn_post=v_norm_ffn_post, sc_w_in=v_sc_w_in, sc_conv_w=v_sc_conv_w, sc_w_out=v_sc_w_out,
                sg_w_in=v_sg_w_in, sg_ln_g=v_sg_ln_g, sg_ln_b=v_sg_ln_b, sg_w_s=v_sg_w_s, sg_b_s=v_sg_b_s,
                sg_w_out=v_sg_w_out, sb_w_qkv=v_sb_w_qkv, sb_w_out=v_sb_w_out, ffn_w_up=v_ffn_w_up,
                ffn_conv_w=v_ffn_conv_w, ffn_conv_b=v_ffn_conv_b, ffn_w_down=v_ffn_w_down)
    order = list(weights)

    mx, my, mc = lax.axis_index("x"), lax.axis_index("y"), lax.axis_index("c")
    dev = 4 * mx + 2 * my + mc
    core_arr = jnp.reshape(mc, (1,)).astype(jnp.int32)
    chip_arr = jnp.reshape(2 * mx + my, (1,)).astype(jnp.int32)

    h = x[0]
    target = loss_target[0]
    depth = norm_mix_pre.shape[0]
    d_model = h.shape[1]
    f_loc = ffn_w_down.shape[1]
    d_ff = f_loc * N_DEV

    def gather_rows8(a2d, name):
        rows = a2d.shape[0]
        pad = (-rows) % 8
        return _all_gather(jnp.pad(a2d, ((0, pad), (0, 0))), 1, name)[:rows]

    n_a = sc_w_in.shape[0]
    sc_conv_full = gather_rows8(sc_conv_w.reshape(n_a * 3, -1), "ag_sc_conv").reshape(n_a, 3, d_model)
    ffn_conv_full = gather_rows8(ffn_conv_w.reshape(depth * 3, -1), "ag_ffn_conv").reshape(depth, 3, 2 * d_ff)

    big = {}

    kinds = [i % 3 for i in range(depth)]
    mixer_names = [("sc_w_in", "sc_w_out"), ("sg_w_in", "sg_w_out"), ("sb_w_qkv", "sb_w_out")]
    mixer_split = [3, 2, 3]
    mixer_proj = ["sc_in_proj", "sg_in_proj", "sb_qkv_proj"]

    def layer_keys(i):
        wi, wo = mixer_names[kinds[i]]
        return (wi, i // 3, 1), (wo, i // 3, 0), ("ffn_w_up", i, 1), ("ffn_w_down", i, 0)

    def gather_job(key):
        name, j, axis = key
        return _GatherJob(weights[name][j].astype(BF16), axis)

    def hosted_mm(a, key, out_dtype, nsplit, name, gather_keys):
        keys = [k for k in gather_keys if k is not None]
        res = _mm_nn(a, big[key[:2]], out_dtype, nsplit, name, jobs=[gather_job(k) for k in keys])
        if not keys:
            return res
        for k, full in zip(keys, res[1:]):
            big[k[:2]] = full
        return res[0]

    sg_ws_t = jnp.swapaxes(sg_w_s, -1, -2)
    sg_bs3 = sg_b_s[..., None]

    k_in0 = layer_keys(0)[0]
    big[k_in0[:2]] = _all_gather(weights[k_in0[0]][k_in0[1]].astype(BF16), k_in0[2], f"ag_{k_in0[0]}")
    saved = []
    hn = _rms_fwd(h, norm_mix_pre[0:1], "rms_fwd")
    for i in range(depth):
        j = i // 3
        k_in, k_out, k_up, k_down = layer_keys(i)
        nxt = layer_keys(i + 1) if i + 1 < depth else (None, None, None, None)
        sv = {"h_in": h, "hn": hn}
        p = hosted_mm(hn, k_in, BF16, mixer_split[kinds[i]], mixer_proj[kinds[i]],
                      [k_out, k_up] if i == 0 else [])
        if kinds[i] == 0:
            y = _sc_gate_fwd(p, sc_conv_full[j], "sc_gate_fwd")
        elif kinds[i] == 1:
            y = _sgu_fwd(p, sg_ln_g[j:j + 1], sg_ln_b[j:j + 1], sg_w_s[j], sg_bs3[j], "sgu_fwd")
        else:
            y, sv["runs"] = _sb_fwd(p, "sb_fwd")
        m = hosted_mm(y, k_out, F32, 1, "mix_out_proj", [])
        sv.update(p=p, y=y, m=m)
        h, hn2 = _resid_rms_fwd(h, m, norm_mix_post[i:i + 1], norm_ffn_pre[i:i + 1], "resid_rms_fwd")
        sv["h_mid"] = h
        u = hosted_mm(hn2, k_up, BF16, 2, "ffn_up_proj", [k_down, nxt[0], nxt[1]])
        cw = ffn_conv_full[i].reshape(3, 2, d_ff).transpose(1, 0, 2)
        cb = ffn_conv_b[i].reshape(2, 1, d_ff)
        a = _ffn_act_fwd(u, cw, cb, "ffn_act_fwd")
        f = hosted_mm(a, k_down, F32, 1, "ffn_down_proj", [nxt[2]])
        sv.update(hn2=hn2, u=u, a=a, f=f, cw=cw, cb=cb)
        h, hn = _resid_rms_fwd(h, f, norm_ffn_post[i:i + 1], norm_mix_pre[i + 1:i + 2] if i + 1 < depth else None,
                               "resid_rms_fwd")
        saved.append(sv)

    dh, loss_part = _loss_head(h, target, "loss_head")

    grads_big = {}
    small = {}

    parts = {}
    queue = []

    def reduce_scatter(name, j, g_full, axis):
        recv1 = _rs_core_exchange(g_full, axis, f"rs1_{name}")
        parts[(name, j)] = _rs_core_add(g_full, recv1, core_arr, axis, f"rs_add_{name}")
        queue.append((name, j))

    def hosting(mm, a, b, out_dtype, name, n_jobs):
        keys = [queue.pop(0) for _ in range(min(n_jobs, len(queue)))]
        res = mm(a, b, out_dtype, name, jobs=[_ChipExchangeJob(parts[k]) for k in keys])
        if not keys:
            return res
        for k, recv in zip(keys, res[1:]):
            grads_big[k] = (parts[k], recv)
        return res[0]

    for i in reversed(range(depth)):
        j = i // 3
        sv = saved[i]
        df, dg = _rms_bwd(sv["f"], norm_ffn_post[i:i + 1], dh, None, BF16, "rms_bwd_post")
        small.setdefault("norm_ffn_post", {})[i] = dg
        da = hosting(_mm_nt, df, big[("ffn_w_down", i)], BF16, "ffn_down_dx", 1)
        reduce_scatter("ffn_w_down", i, hosting(_mm_tn, sv["a"], df, BF16, "ffn_down_dw", 2), 0)
        du, dcw, dcb = _ffn_act_bwd(sv["u"], da, sv["cw"], sv["cb"], "ffn_act_bwd")
        small.setdefault("ffn_conv_w", {})[i] = dcw.transpose(1, 0, 2).reshape(3, 2 * d_ff)
        small.setdefault("ffn_conv_b", {})[i] = dcb.reshape(2 * d_ff)
        dhn2 = _mm_nt(du, big[("ffn_w_up", i)], F32, "ffn_up_dx")
        reduce_scatter("ffn_w_up", i, _mm_tn(sv["hn2"], du, BF16, "ffn_up_dw"), 1)
        dh, dg = _rms_bwd(sv["h_mid"], norm_ffn_pre[i:i + 1], dhn2, dh, F32, "rms_bwd_pre")
        small.setdefault("norm_ffn_pre", {})[i] = dg
        dm, dg = _rms_bwd(sv["m"], norm_mix_post[i:i + 1], dh, None, BF16, "rms_bwd_post")
        small.setdefault("norm_mix_post", {})[i] = dg
        if kinds[i] == 0:
            wo, wi = "sc_w_out", "sc_w_in"
        elif kinds[i] == 1:
            wo, wi = "sg_w_out", "sg_w_in"
        else:
            wo, wi = "sb_w_out", "sb_w_qkv"
        dy = hosting(_mm_nt, dm, big[(wo, j)], BF16, "mix_out_dx", 1)
        reduce_scatter(wo, j, _mm_tn(sv["y"], dm, BF16, "mix_out_dw"), 0)
        if kinds[i] == 0:
            dp, dcw = _sc_gate_bwd(sv["p"], dy, sc_conv_full[j], "sc_gate_bwd")
            small.setdefault("sc_conv_w", {})[j] = dcw
        elif kinds[i] == 1:
            dp, dlg, dlb, dws, dbs = _sgu_bwd(sv["p"], dy, sg_ln_g[j:j + 1], sg_ln_b[j:j + 1], sg_w_s[j], sg_ws_t[j],
                                              sg_bs3[j], "sgu_bwd")
            small.setdefault("sg_ln_g", {})[j] = dlg[0]
            small.setdefault("sg_ln_b", {})[j] = dlb[0]
            small.setdefault("sg_w_s", {})[j] = dws
            small.setdefault("sg_b_s", {})[j] = dbs[..., 0]
        else:
            dq, dkv = _sb_bwd(sv["p"], dy, sv["runs"], "sb_bwd")
            dp = jnp.concatenate([dq[None], dkv], axis=0)
        dhn = hosting(_mm_nt, dp, big[(wi, j)], F32, "mix_in_dx", 2)
        reduce_scatter(wi, j, _mm_tn(sv["hn"], dp, BF16, "mix_in_dw"), 1)
        dh, dg = _rms_bwd(sv["h_in"], norm_mix_pre[i:i + 1], dhn, dh, F32, "rms_bwd_pre")
        small.setdefault("norm_mix_pre", {})[i] = dg

    grad_x = dh[None]
    for k, recv in zip(queue, _rs_chip_exchange([parts[k] for k in queue], "rs2_tail")):
        grads_big[k] = (parts[k], recv)

    small_names = ["norm_mix_pre", "norm_mix_post", "norm_ffn_pre", "norm_ffn_post", "sg_ln_g", "sg_ln_b", "sg_w_s",
                   "sg_b_s", "ffn_conv_b", "sc_conv_w", "ffn_conv_w"]
    full_shapes = {n: weights[n].shape for n in small_names}
    full_shapes["sc_conv_w"] = (n_a, 3, d_model)
    full_shapes["ffn_conv_w"] = (depth, 3, 2 * d_ff)
    partial = [jnp.stack([small[n][k].reshape(full_shapes[n][1:]) for k in sorted(small[n])]) for n in small_names]
    packed = _pack(partial + [loss_part])
    gathered = _all_gather(packed, 0, "ag_small_grads")
    summed = _sum_devices(gathered, "sum_small_grads")
    pieces = _unpack(summed, [full_shapes[n] for n in small_names] + [(1, 1)])
    loss = pieces[-1].reshape(())
    small_grads = dict(zip(small_names, pieces[:-1]))
    for n in ("sc_conv_w", "ffn_conv_w"):
        c_loc = weights[n].shape[-1]
        small_grads[n] = lax.dynamic_slice_in_dim(small_grads[n], dev * c_loc, c_loc, axis=2)
    shapes = [weights[n].shape for n in small_names]
    d_p, m_p, v_p = _adamw_packed(_pack([weights[n] for n in small_names]), _pack([mom1[n] for n in small_names]),
                                  _pack([mom2[n] for n in small_names]), _pack([small_grads[n] for n in small_names]),
                                  "adamw_small")
    out_g, out_d, out_m, out_v = dict(small_grads), {}, {}, {}
    for n, d_, m_, v_ in zip(small_names, _unpack(d_p, shapes), _unpack(m_p, shapes), _unpack(v_p, shapes)):
        out_d[n], out_m[n], out_v[n] = d_, m_, v_

    for n in order:
        if n in small_names:
            continue
        acc = None
        for j in range(weights[n].shape[0]):
            acc = _adamw_shard(weights[n], mom1[n], mom2[n], j, *grads_big[(n, j)], chip_arr, acc, f"adamw_{n}")
        out_g[n], out_d[n], out_m[n], out_v[n] = acc

    return (loss, grad_x, *[out_g[n] for n in order], *[out_d[n] for n in order],
            *[out_m[n] for n in order], *[out_v[n] for n in order])
```

```python
import functools

import jax
import jax.numpy as jnp
from jax import lax
from jax.experimental import pallas as pl
from jax.experimental.pallas import tpu as pltpu

F32 = jnp.float32
BF16 = jnp.bfloat16

EPS = 1e-6
CHUNK = 128
HEAD_DIM = 128
SG_GROUPS = 8
HALO = 16
N_DEV = 8
LANES = 128
V7X_VMEM_LIMIT = 56 * 1024 * 1024

ADAM_LR = 0.001
ADAM_B1 = 0.9
ADAM_B2 = 0.999
ADAM_EPS = 1e-08
ADAM_WD = 0.01
ADAM_STEP = 10

MESH = pl.DeviceIdType.MESH
ANY = pl.BlockSpec(memory_space=pl.ANY)


def _tile(n, prefs):
    for p in prefs:
        if p <= n and n % p == 0:
            return p
    return n


def _params(*sem):
    return pltpu.CompilerParams(dimension_semantics=sem, vmem_limit_bytes=V7X_VMEM_LIMIT)


def _shard_of(ref, dev, axis, r, c):
    if axis == 0:
        return ref.at[pl.ds(dev * r, r), :]
    return ref.at[:, pl.ds(dev * c, c)]


class _GatherJob:
    def __init__(self, x, axis):
        self.axis = axis
        self.r, self.c = x.shape
        full = (N_DEV * self.r, self.c) if axis == 0 else (self.r, N_DEV * self.c)
        self.inputs = [x]
        self.out_shapes = [jax.ShapeDtypeStruct(full, x.dtype)]
        self.scratch = [pltpu.SemaphoreType.DMA((7,)), pltpu.SemaphoreType.DMA((7,)), pltpu.SemaphoreType.DMA(())]

    def _plan(self, ins, outs, scr):
        (x_ref,), (out_ref,), (send_sems, recv_sems, local_sem) = ins, outs, scr
        mx, my, mc = lax.axis_index("x"), lax.axis_index("y"), lax.axis_index("c")
        me, sibling = (mx, my, mc), (mx, my, 1 - mc)
        chips = [(1 - mx, my), (mx, 1 - my), (1 - mx, 1 - my)]

        def rows(px, py, pc):
            return _shard_of(out_ref, 4 * px + 2 * py + pc, self.axis, self.r, self.c)

        def copy(k, block, to, src=None):
            return pltpu.make_async_remote_copy(
                src_ref=rows(*block) if src is None else src, dst_ref=rows(*block),
                send_sem=send_sems.at[k], recv_sem=recv_sems.at[k], device_id=to, device_id_type=MESH)

        mine = pltpu.make_async_copy(x_ref, rows(*me), local_sem)
        first = [copy(0, me, sibling, src=x_ref)] + [copy(1 + j, me, (*chip, mc), src=x_ref)
                                                      for j, chip in enumerate(chips)]
        passed = [copy(4 + j, (*chip, mc), sibling) for j, chip in enumerate(chips)]
        landed = [copy(1 + j, (*chip, mc), me) for j, chip in enumerate(chips)]
        from_sibling = [copy(0, sibling, me)] + [copy(4 + j, (*chip, 1 - mc), me) for j, chip in enumerate(chips)]
        return mine, first, passed, landed, from_sibling

    def start(self, ins, outs, scr):
        mine, first, _, _, _ = self._plan(ins, outs, scr)
        mine.start()
        for cp in first:
            cp.start()

    def finish(self, ins, outs, scr):
        mine, first, passed, landed, from_sibling = self._plan(ins, outs, scr)
        for j in range(3):
            landed[j].wait_recv()
            passed[j].start()
        for cp in from_sibling:
            cp.wait_recv()
        for cp in first + passed:
            cp.wait_send()
        mine.wait()


class _ChipExchangeJob:
    def __init__(self, p):
        _, r, c = p.shape
        self.inputs = [p]
        self.out_shapes = [jax.ShapeDtypeStruct((3, r, c), p.dtype)]
        self.scratch = [pltpu.SemaphoreType.DMA((3,)), pltpu.SemaphoreType.DMA((3,))]

    def _plan(self, ins, outs, scr):
        (p_ref,), (recv_ref,), (send_sems, recv_sems) = ins, outs, scr
        mx, my, mc = lax.axis_index("x"), lax.axis_index("y"), lax.axis_index("c")
        chips = [(1 - mx, my), (mx, 1 - my), (1 - mx, 1 - my)]
        return [pltpu.make_async_remote_copy(
            src_ref=p_ref.at[2 * px + py], dst_ref=recv_ref.at[j],
            send_sem=send_sems.at[j], recv_sem=recv_sems.at[j], device_id=(px, py, mc), device_id_type=MESH)
            for j, (px, py) in enumerate(chips)]

    def start(self, ins, outs, scr):
        for cp in self._plan(ins, outs, scr):
            cp.start()

    def finish(self, ins, outs, scr):
        for cp in self._plan(ins, outs, scr):
            cp.wait()


class _CoreExchangeJob:
    def __init__(self, g, axis):
        self.axis = axis
        self.r, self.c = (g.shape[0] // N_DEV, g.shape[1]) if axis == 0 else (g.shape[0], g.shape[1] // N_DEV)
        self.inputs = [g]
        self.out_shapes = [jax.ShapeDtypeStruct((4, self.r, self.c), g.dtype)]
        self.scratch = [pltpu.SemaphoreType.DMA((4,)), pltpu.SemaphoreType.DMA((4,))]

    def _plan(self, ins, outs, scr):
        (g_ref,), (recv_ref,), (send_sems, recv_sems) = ins, outs, scr
        mx, my, mc = lax.axis_index("x"), lax.axis_index("y"), lax.axis_index("c")
        return [pltpu.make_async_remote_copy(
            src_ref=_shard_of(g_ref, 2 * k + (1 - mc), self.axis, self.r, self.c), dst_ref=recv_ref.at[k],
            send_sem=send_sems.at[k], recv_sem=recv_sems.at[k], device_id=(mx, my, 1 - mc), device_id_type=MESH)
            for k in range(4)]

    def start(self, ins, outs, scr):
        for cp in self._plan(ins, outs, scr):
            cp.start()

    def finish(self, ins, outs, scr):
        for cp in self._plan(ins, outs, scr):
            cp.wait()


class _Hosted:
    def __init__(self, jobs):
        self.jobs = list(jobs)
        self.inputs = [x for j in self.jobs for x in j.inputs]
        self.out_shapes = [s for j in self.jobs for s in j.out_shapes]
        self.scratch = [s for j in self.jobs for s in j.scratch]
        self.in_specs = [ANY] * len(self.inputs)
        self.out_specs = [ANY] * len(self.out_shapes)

    def split(self, refs, n_in, n_out, n_scratch):
        refs = list(refs)
        ji, jo, js = len(self.inputs), len(self.out_shapes), len(self.scratch)
        ins, refs = refs[:n_in], refs[n_in:]
        jins, refs = refs[:ji], refs[ji:]
        outs, refs = refs[:n_out], refs[n_out:]
        jouts, refs = refs[:jo], refs[jo:]
        scr, jscr = refs[:n_scratch], refs[n_scratch:]
        assert len(jscr) == js
        per_job = []
        for j in self.jobs:
            a, b, c = len(j.inputs), len(j.out_shapes), len(j.scratch)
            per_job.append((jins[:a], jouts[:b], jscr[:c]))
            jins, jouts, jscr = jins[a:], jouts[b:], jscr[c:]
        return ins, outs, scr, per_job

    def start(self, per_job, when):
        if self.jobs:
            @pl.when(when)
            def _():
                for j, refs in zip(self.jobs, per_job):
                    j.start(*refs)

    def finish(self, per_job, when):
        if self.jobs:
            @pl.when(when)
            def _():
                for j, refs in zip(self.jobs, per_job):
                    j.finish(*refs)


def _grid_edges(grid):
    first = last = None
    for ax, n in enumerate(grid):
        f, l = pl.program_id(ax) == 0, pl.program_id(ax) == n - 1
        first = f if first is None else jnp.logical_and(first, f)
        last = l if last is None else jnp.logical_and(last, l)
    return first, last


def _run_jobs(jobs, name):
    hosted = _Hosted(jobs)

    def body(*refs):
        _, _, _, per_job = hosted.split(refs, 0, 0, 0)
        for j, r in zip(hosted.jobs, per_job):
            j.start(*r)
        for j, r in zip(hosted.jobs, per_job):
            j.finish(*r)

    return pl.pallas_call(
        body, name=name, out_shape=tuple(hosted.out_shapes), in_specs=hosted.in_specs,
        out_specs=tuple(hosted.out_specs), scratch_shapes=hosted.scratch,
    )(*hosted.inputs)


def _mm_nn(a, b, out_dtype, nsplit, name, jobs=()):
    m, k = a.shape
    n = b.shape[1]
    w = n // nsplit
    tm = _tile(m, (2048, 1024, 512, 256, 128)) if k <= 2048 else _tile(m, (1024, 512, 256, 128))
    tn = _tile(w, (512, 256, 128))
    per = w // tn
    grid = (m // tm, n // tn)
    hosted = _Hosted(jobs)

    def body(*refs):
        (a_ref, b_ref), (o_ref,), _, per_job = hosted.split(refs, 2, 1, 0)
        first, last = _grid_edges(grid)
        hosted.start(per_job, first)
        o_ref[...] = jnp.dot(a_ref[...], b_ref[...], preferred_element_type=F32).astype(o_ref.dtype)
        hosted.finish(per_job, last)

    if nsplit == 1:
        out_shape = jax.ShapeDtypeStruct((m, n), out_dtype)
        out_spec = pl.BlockSpec((tm, tn), lambda i, j: (i, j))
    else:
        out_shape = jax.ShapeDtypeStruct((nsplit, m, w), out_dtype)
        out_spec = pl.BlockSpec((None, tm, tn), lambda i, j: (j // per, i, j % per))
    res = pl.pallas_call(
        body, grid=grid, name=name,
        in_specs=[pl.BlockSpec((tm, k), lambda i, j: (i, 0)), pl.BlockSpec((k, tn), lambda i, j: (0, j))]
        + hosted.in_specs,
        out_specs=tuple([out_spec] + hosted.out_specs), out_shape=tuple([out_shape] + hosted.out_shapes),
        scratch_shapes=hosted.scratch,
        compiler_params=_params("arbitrary", "arbitrary"),
    )(a, b, *hosted.inputs)
    return res if jobs else res[0]


def _mm_nt(a, b, out_dtype, name, jobs=()):
    a3 = a if a.ndim == 3 else a[None]
    ns, m, w = a3.shape
    ko, n = b.shape
    assert n == ns * w
    tm = _tile(m, (1024, 512, 256, 128)) if n <= 2048 else _tile(m, (512, 256, 128))
    tko = _tile(ko, (256, 128)) if n > 8192 else _tile(ko, (512, 256, 128))
    grid = (m // tm, ko // tko)
    hosted = _Hosted(jobs)

    def body(*refs):
        (a_ref, b_ref), (o_ref,), _, per_job = hosted.split(refs, 2, 1, 0)
        first, last = _grid_edges(grid)
        hosted.start(per_job, first)
        acc = None
        for s in range(ns):
            p = lax.dot_general(a_ref[s], b_ref[:, s * w:(s + 1) * w], (((1,), (1,)), ((), ())),
                                preferred_element_type=F32)
            acc = p if acc is None else acc + p
        o_ref[...] = acc.astype(o_ref.dtype)
        hosted.finish(per_job, last)

    res = pl.pallas_call(
        body, grid=grid, name=name,
        in_specs=[pl.BlockSpec((ns, tm, w), lambda i, j: (0, i, 0)), pl.BlockSpec((tko, n), lambda i, j: (j, 0))]
        + hosted.in_specs,
        out_specs=tuple([pl.BlockSpec((tm, tko), lambda i, j: (i, j))] + hosted.out_specs),
        out_shape=tuple([jax.ShapeDtypeStruct((m, ko), out_dtype)] + hosted.out_shapes),
        scratch_shapes=hosted.scratch,
        compiler_params=_params("arbitrary", "arbitrary"),
    )(a3, b, *hosted.inputs)
    return res if jobs else res[0]


def _mm_tn(a, b, out_dtype, name, jobs=()):
    b3 = b if b.ndim == 3 else b[None]
    ns, m, w = b3.shape
    k = a.shape[1]
    n = ns * w
    tk = _tile(k, (512, 256, 128))
    tn = _tile(w, (2816, 2048, 1024, 512, 256, 128))
    ts = _tile(m, (2048, 1024, 512, 256, 128))
    per = w // tn
    n_s = m // ts
    grid = (k // tk, n // tn, n_s)
    hosted = _Hosted(jobs)

    def body(*refs):
        (a_ref, b_ref), (o_ref,), (acc_ref,), per_job = hosted.split(refs, 2, 1, 1)
        s = pl.program_id(2)
        first, last = _grid_edges(grid)
        hosted.start(per_job, first)

        @pl.when(s == 0)
        def _():
            acc_ref[...] = jnp.zeros_like(acc_ref)

        acc_ref[...] += lax.dot_general(a_ref[...], b_ref[...], (((0,), (0,)), ((), ())),
                                        preferred_element_type=F32)

        @pl.when(s == n_s - 1)
        def _():
            o_ref[...] = acc_ref[...].astype(o_ref.dtype)

        hosted.finish(per_job, last)

    res = pl.pallas_call(
        body, grid=grid, name=name,
        in_specs=[pl.BlockSpec((ts, tk), lambda i, j, s: (s, i)),
                  pl.BlockSpec((None, ts, tn), lambda i, j, s: (j // per, s, j % per))] + hosted.in_specs,
        out_specs=tuple([pl.BlockSpec((tk, tn), lambda i, j, s: (i, j))] + hosted.out_specs),
        out_shape=tuple([jax.ShapeDtypeStruct((k, n), out_dtype)] + hosted.out_shapes),
        scratch_shapes=[pltpu.VMEM((tk, tn), F32)] + hosted.scratch,
        compiler_params=_params("arbitrary", "arbitrary", "arbitrary"),
    )(a, b3, *hosted.inputs)
    return res if jobs else res[0]


def _rms_rows(s):
    return _tile(s, (256, 128))


def _rms_fwd(x, g, name):
    s, d = x.shape
    ts = _rms_rows(s)

    def body(x_ref, g_ref, o_ref):
        xf = x_ref[...]
        r = lax.rsqrt(jnp.mean(xf * xf, axis=-1, keepdims=True) + EPS)
        o_ref[...] = ((xf * r) * g_ref[...]).astype(o_ref.dtype)

    return pl.pallas_call(
        body, grid=(s // ts,), name=name,
        in_specs=[pl.BlockSpec((ts, d), lambda i: (i, 0)), pl.BlockSpec((1, d), lambda i: (0, 0))],
        out_specs=pl.BlockSpec((ts, d), lambda i: (i, 0)),
        out_shape=jax.ShapeDtypeStruct((s, d), BF16),
        compiler_params=_params("parallel"),
    )(x, g)


def _resid_rms_fwd(h, m, g, g_next, name):
    s, d = h.shape
    ts = _rms_rows(s)
    both = g_next is not None

    def body(*refs):
        h_ref, m_ref, g_ref = refs[:3]
        mf = m_ref[...]
        r = lax.rsqrt(jnp.mean(mf * mf, axis=-1, keepdims=True) + EPS)
        hf = h_ref[...] + (mf * r) * g_ref[...]
        refs[-2 if both else -1][...] = hf
        if both:
            r2 = lax.rsqrt(jnp.mean(hf * hf, axis=-1, keepdims=True) + EPS)
            refs[-1][...] = ((hf * r2) * refs[3][...]).astype(BF16)

    row = pl.BlockSpec((ts, d), lambda i: (i, 0))
    vec = pl.BlockSpec((1, d), lambda i: (0, 0))
    res = pl.pallas_call(
        body, grid=(s // ts,), name=name,
        in_specs=[row, row, vec] + ([vec] if both else []),
        out_specs=(row, row) if both else (row,),
        out_shape=(jax.ShapeDtypeStruct((s, d), F32),) + ((jax.ShapeDtypeStruct((s, d), BF16),) if both else ()),
        compiler_params=_params("parallel"),
    )(*([h, m, g] + ([g_next] if both else [])))
    return res if both else (res[0], None)


def _rms_bwd(x, g, dy, dres, out_dtype, name):
    s, d = x.shape
    ts = _rms_rows(s)
    has_res = dres is not None

    def body(*refs):
        if has_res:
            x_ref, g_ref, dy_ref, dres_ref, dx_ref, dg_ref = refs
        else:
            x_ref, g_ref, dy_ref, dx_ref, dg_ref = refs
        i = pl.program_id(0)
        xf = x_ref[...]
        dyf = dy_ref[...].astype(F32)
        r = lax.rsqrt(jnp.mean(xf * xf, axis=-1, keepdims=True) + EPS)
        xh = xf * r
        dxh = dyf * g_ref[...]
        dx = r * (dxh - xh * jnp.mean(dxh * xh, axis=-1, keepdims=True))
        if has_res:
            dx = dx + dres_ref[...]
        dx_ref[...] = dx.astype(dx_ref.dtype)

        @pl.when(i == 0)
        def _():
            dg_ref[...] = jnp.zeros_like(dg_ref)

        dg_ref[...] += jnp.sum(dyf * xh, axis=0, keepdims=True)

    row = pl.BlockSpec((ts, d), lambda i: (i, 0))
    vec = pl.BlockSpec((1, d), lambda i: (0, 0))
    ins = [x, g, dy] + ([dres] if has_res else [])
    return pl.pallas_call(
        body, grid=(s // ts,), name=name,
        in_specs=[row, vec, row] + ([row] if has_res else []),
        out_specs=(row, vec),
        out_shape=(jax.ShapeDtypeStruct((s, d), out_dtype), jax.ShapeDtypeStruct((1, d), F32)),
        compiler_params=_params("arbitrary"),
    )(*ins)


def _loss_head(h, target, name):
    s, d = h.shape
    ts = _rms_rows(s)
    n_i = s // ts

    def body(h_ref, t_ref, dh_ref, loss_ref, acc_ref):
        i = pl.program_id(0)
        err = h_ref[...] - t_ref[...]
        dh_ref[...] = err * (1.0 / d)

        @pl.when(i == 0)
        def _():
            acc_ref[...] = jnp.zeros_like(acc_ref)

        acc_ref[...] += jnp.sum(err * err, axis=0, keepdims=True)

        @pl.when(i == n_i - 1)
        def _():
            loss_ref[...] = jnp.sum(acc_ref[...], axis=1, keepdims=True) * (0.5 / d)

    row = pl.BlockSpec((ts, d), lambda i: (i, 0))
    return pl.pallas_call(
        body, grid=(n_i,), name=name,
        in_specs=[row, row],
        out_specs=(row, pl.BlockSpec((1, 1), lambda i: (0, 0))),
        out_shape=(jax.ShapeDtypeStruct((s, d), F32), jax.ShapeDtypeStruct((1, 1), F32)),
        scratch_shapes=[pltpu.VMEM((1, d), F32)],
        compiler_params=_params("arbitrary"),
    )(h, target)


def _halo_specs(ns, s, ts, tc):
    hb = ts // HALO
    last = s // HALO - 1
    cur = pl.BlockSpec((ns, ts, tc), lambda j, i: (0, i, j))
    prev = pl.BlockSpec((ns, HALO, tc), lambda j, i: (0, jnp.maximum(i * hb - 1, 0), j))
    nxt = pl.BlockSpec((ns, HALO, tc), lambda j, i: (0, jnp.minimum((i + 1) * hb, last), j))
    return cur, prev, nxt


def _ext(prev, cur, nxt, first, last):
    p = jnp.where(first, 0.0, prev.astype(F32))
    n = jnp.where(last, 0.0, nxt.astype(F32))
    return jnp.concatenate([p, cur.astype(F32), n], axis=0)


def _dn(xe, k):
    return xe if k == 0 else pltpu.roll(xe, k, 0)


def _up(xe, k):
    return xe if k == 0 else pltpu.roll(xe, xe.shape[0] - k, 0)


def _conv_e(xe, w):
    return w[0:1, :] * _dn(xe, 2) + w[1:2, :] * _dn(xe, 1) + w[2:3, :] * xe


def _conv_t_e(de, w):
    return w[2:3, :] * de + w[1:2, :] * _up(de, 1) + w[0:1, :] * _up(de, 2)


def _conv_dw(de, xe, ts):
    return [jnp.sum((de * _dn(xe, 2 - k))[HALO:HALO + ts], axis=0, keepdims=True) for k in range(3)]


def _conv_tiles(s, c):
    return _tile(s, (512, 256, 128)), _tile(c, (512, 256, 128))


def _sigmoid(x):
    return pl.reciprocal(1.0 + jnp.exp(-x), approx=True)


def _sc_gate_fwd(p, w, name):
    _, s, c = p.shape
    ts, tc = _conv_tiles(s, c)
    n_i = s // ts

    def body(p_ref, pp_ref, w_ref, y_ref):
        i = pl.program_id(1)
        ze = _ext(pp_ref[1], p_ref[1], pp_ref[1], i == 0, True) * _ext(pp_ref[2], p_ref[2], pp_ref[2], i == 0, True)
        cz = _conv_e(ze, w_ref[...])[HALO:HALO + ts]
        y_ref[...] = (p_ref[0].astype(F32) * cz).astype(y_ref.dtype)

    cur, prev, _ = _halo_specs(3, s, ts, tc)
    return pl.pallas_call(
        body, grid=(c // tc, n_i), name=name,
        in_specs=[cur, prev, pl.BlockSpec((3, tc), lambda j, i: (0, j))],
        out_specs=pl.BlockSpec((ts, tc), lambda j, i: (i, j)),
        out_shape=jax.ShapeDtypeStruct((s, c), BF16),
        compiler_params=_params("parallel", "arbitrary"),
    )(p, p, w)


def _sc_gate_bwd(p, dy, w, name):
    _, s, c = p.shape
    ts, tc = _conv_tiles(s, c)
    n_i = s // ts

    def body(p_ref, pp_ref, pn_ref, dy_ref, dyn_ref, w_ref, dp_ref, dw_ref):
        i = pl.program_id(1)
        first, last = i == 0, i == n_i - 1
        wv = w_ref[...]
        gbe = _ext(pp_ref[0], p_ref[0], pn_ref[0], first, last)
        gce = _ext(pp_ref[1], p_ref[1], pn_ref[1], first, last)
        hhe = _ext(pp_ref[2], p_ref[2], pn_ref[2], first, last)
        dye = _ext(dyn_ref[...], dy_ref[...], dyn_ref[...], True, last)
        ze = gce * hhe
        cze = _conv_e(ze, wv)
        dcze = dye * gbe
        dze = _conv_t_e(dcze, wv)
        rows = slice(HALO, HALO + ts)
        dp_ref[0] = (dye * cze)[rows].astype(dp_ref.dtype)
        dp_ref[1] = (dze * hhe)[rows].astype(dp_ref.dtype)
        dp_ref[2] = (dze * gce)[rows].astype(dp_ref.dtype)

        @pl.when(first)
        def _():
            dw_ref[...] = jnp.zeros_like(dw_ref)

        dws = _conv_dw(dcze, ze, ts)
        for k in range(3):
            dw_ref[k:k + 1, :] += dws[k]

    cur, prev, nxt = _halo_specs(3, s, ts, tc)
    hb = ts // HALO
    last_h = s // HALO - 1
    return pl.pallas_call(
        body, grid=(c // tc, n_i), name=name,
        in_specs=[cur, prev, nxt,
                  pl.BlockSpec((ts, tc), lambda j, i: (i, j)),
                  pl.BlockSpec((HALO, tc), lambda j, i: (jnp.minimum((i + 1) * hb, last_h), j)),
                  pl.BlockSpec((3, tc), lambda j, i: (0, j))],
        out_specs=(cur, pl.BlockSpec((3, tc), lambda j, i: (0, j))),
        out_shape=(jax.ShapeDtypeStruct((3, s, c), BF16), jax.ShapeDtypeStruct((3, c), F32)),
        compiler_params=_params("parallel", "arbitrary"),
    )(p, p, p, dy, dy, w)


def _ffn_act_fwd(u, w, b, name):
    _, s, f = u.shape
    ts, tc = _conv_tiles(s, f)
    n_i = s // ts

    def body(u_ref, up_ref, w_ref, b_ref, a_ref):
        i = pl.program_id(1)
        rows = slice(HALO, HALO + ts)
        cg = _conv_e(_ext(up_ref[0], u_ref[0], up_ref[0], i == 0, True), w_ref[0])[rows] + b_ref[0]
        cv = _conv_e(_ext(up_ref[1], u_ref[1], up_ref[1], i == 0, True), w_ref[1])[rows] + b_ref[1]
        a_ref[...] = (cg * _sigmoid(cg) * cv).astype(a_ref.dtype)

    cur, prev, _ = _halo_specs(2, s, ts, tc)
    return pl.pallas_call(
        body, grid=(f // tc, n_i), name=name,
        in_specs=[cur, prev, pl.BlockSpec((2, 3, tc), lambda j, i: (0, 0, j)),
                  pl.BlockSpec((2, 1, tc), lambda j, i: (0, 0, j))],
        out_specs=pl.BlockSpec((ts, tc), lambda j, i: (i, j)),
        out_shape=jax.ShapeDtypeStruct((s, f), BF16),
        compiler_params=_params("parallel", "arbitrary"),
    )(u, u, w, b)


def _ffn_act_bwd(u, da, w, b, name):
    _, s, f = u.shape
    ts, tc = _conv_tiles(s, f)
    n_i = s // ts

    def body(u_ref, up_ref, un_ref, da_ref, dan_ref, w_ref, b_ref, du_ref, dw_ref, db_ref):
        i = pl.program_id(1)
        first, last = i == 0, i == n_i - 1
        rows = slice(HALO, HALO + ts)
        uge = _ext(up_ref[0], u_ref[0], un_ref[0], first, last)
        uve = _ext(up_ref[1], u_ref[1], un_ref[1], first, last)
        dae = _ext(dan_ref[...], da_ref[...], dan_ref[...], True, last)
        cg = _conv_e(uge, w_ref[0]) + b_ref[0]
        cv = _conv_e(uve, w_ref[1]) + b_ref[1]
        sg = _sigmoid(cg)
        dcg = dae * cv * (sg * (1.0 + cg * (1.0 - sg)))
        dcv = dae * (cg * sg)
        du_ref[0] = _conv_t_e(dcg, w_ref[0])[rows].astype(du_ref.dtype)
        du_ref[1] = _conv_t_e(dcv, w_ref[1])[rows].astype(du_ref.dtype)

        @pl.when(first)
        def _():
            dw_ref[...] = jnp.zeros_like(dw_ref)
            db_ref[...] = jnp.zeros_like(db_ref)

        for h, (de, xe) in enumerate(((dcg, uge), (dcv, uve))):
            dws = _conv_dw(de, xe, ts)
            for k in range(3):
                dw_ref[h, k:k + 1, :] += dws[k]
            db_ref[h] += jnp.sum(de[rows], axis=0, keepdims=True)

    cur, prev, nxt = _halo_specs(2, s, ts, tc)
    hb = ts // HALO
    last_h = s // HALO - 1
    wspec = pl.BlockSpec((2, 3, tc), lambda j, i: (0, 0, j))
    bspec = pl.BlockSpec((2, 1, tc), lambda j, i: (0, 0, j))
    return pl.pallas_call(
        body, grid=(f // tc, n_i), name=name,
        in_specs=[cur, prev, nxt,
                  pl.BlockSpec((ts, tc), lambda j, i: (i, j)),
                  pl.BlockSpec((HALO, tc), lambda j, i: (jnp.minimum((i + 1) * hb, last_h), j)),
                  wspec, bspec],
        out_specs=(cur, wspec, bspec),
        out_shape=(jax.ShapeDtypeStruct((2, s, f), BF16), jax.ShapeDtypeStruct((2, 3, f), F32),
                   jax.ShapeDtypeStruct((2, 1, f), F32)),
        compiler_params=_params("parallel", "arbitrary"),
    )(u, u, u, da, da, w, b)


_INV_SQRT2 = 0.7071067811865476
_INV_SQRT_2PI = 0.3989422804014327


def _gelu(x):
    return 0.5 * x * (1.0 + lax.erf(x * _INV_SQRT2))


def _gelu_grad(x):
    return 0.5 * (1.0 + lax.erf(x * _INV_SQRT2)) + x * (_INV_SQRT_2PI * jnp.exp(-0.5 * x * x))


def _tril_bf16(w):
    t = lax.broadcasted_iota(jnp.int32, w.shape, 0)
    s = lax.broadcasted_iota(jnp.int32, w.shape, 1)
    return jnp.where(s <= t, w, 0.0).astype(BF16)


def _sgu_fwd(p, ln_g, ln_b, ws, bs, name):
    _, s, c = p.shape
    g_n = ws.shape[0]
    cg = c // g_n

    def body(p_ref, lg_ref, lb_ref, ws_ref, bs_ref, y_ref):
        u = _gelu(p_ref[0].astype(F32))
        v = _gelu(p_ref[1].astype(F32))
        mu = jnp.mean(v, axis=-1, keepdims=True)
        vc = v - mu
        rstd = lax.rsqrt(jnp.mean(vc * vc, axis=-1, keepdims=True) + EPS)
        vn = ((vc * rstd) * lg_ref[...] + lb_ref[...]).astype(BF16)
        for g in range(g_n):
            cols = slice(g * cg, (g + 1) * cg)
            mixed = jnp.dot(_tril_bf16(ws_ref[g]), vn[:, cols], preferred_element_type=F32) + bs_ref[g]
            y_ref[:, cols] = (u[:, cols] * mixed).astype(y_ref.dtype)

    vec = pl.BlockSpec((1, c), lambda i: (0, 0))
    return pl.pallas_call(
        body, grid=(s // CHUNK,), name=name,
        in_specs=[pl.BlockSpec((2, CHUNK, c), lambda i: (0, i, 0)), vec, vec,
                  pl.BlockSpec((g_n, CHUNK, CHUNK), lambda i: (0, 0, 0)),
                  pl.BlockSpec((g_n, CHUNK, 1), lambda i: (0, 0, 0))],
        out_specs=pl.BlockSpec((CHUNK, c), lambda i: (i, 0)),
        out_shape=jax.ShapeDtypeStruct((s, c), BF16),
        compiler_params=_params("parallel"),
    )(p, ln_g, ln_b, ws, bs)


def _sgu_bwd(p, dy, ln_g, ln_b, ws, ws_t, bs, name):
    _, s, c = p.shape
    g_n = ws.shape[0]
    cg = c // g_n

    def body(p_ref, dy_ref, lg_ref, lb_ref, ws_ref, wst_ref, bs_ref, dp_ref, dlg_ref, dlb_ref, dws_ref, dbs_ref,
             dvn_ref):
        i = pl.program_id(0)
        pu = p_ref[0].astype(F32)
        pv = p_ref[1].astype(F32)
        dyf = dy_ref[...].astype(F32)
        u = _gelu(pu)
        v = _gelu(pv)
        mu = jnp.mean(v, axis=-1, keepdims=True)
        vc = v - mu
        rstd = lax.rsqrt(jnp.mean(vc * vc, axis=-1, keepdims=True) + EPS)
        vhat = vc * rstd
        vn = (vhat * lg_ref[...] + lb_ref[...]).astype(BF16)

        @pl.when(i == 0)
        def _():
            dlg_ref[...] = jnp.zeros_like(dlg_ref)
            dlb_ref[...] = jnp.zeros_like(dlb_ref)
            dws_ref[...] = jnp.zeros_like(dws_ref)
            dbs_ref[...] = jnp.zeros_like(dbs_ref)

        t_i = lax.broadcasted_iota(jnp.int32, (CHUNK, CHUNK), 0)
        s_i = lax.broadcasted_iota(jnp.int32, (CHUNK, CHUNK), 1)
        for g in range(g_n):
            cols = slice(g * cg, (g + 1) * cg)
            mixed = jnp.dot(_tril_bf16(ws_ref[g]), vn[:, cols], preferred_element_type=F32) + bs_ref[g]
            dp_ref[0, :, cols] = (dyf[:, cols] * mixed * _gelu_grad(pu[:, cols])).astype(dp_ref.dtype)
            dmixed = dyf[:, cols] * u[:, cols]
            dmb = dmixed.astype(BF16)
            wt = jnp.where(t_i <= s_i, wst_ref[g], 0.0).astype(BF16)
            dvn_ref[:, cols] = jnp.dot(wt, dmb, preferred_element_type=F32)
            dwg = lax.dot_general(dmb, vn[:, cols], (((1,), (1,)), ((), ())), preferred_element_type=F32)
            dws_ref[g] += jnp.where(s_i <= t_i, dwg, 0.0)
            dbs_ref[g] += jnp.sum(dmixed, axis=1, keepdims=True)
        dvn = dvn_ref[...]
        dlg_ref[...] += jnp.sum(dvn * vhat, axis=0, keepdims=True)
        dlb_ref[...] += jnp.sum(dvn, axis=0, keepdims=True)
        dvh = dvn * lg_ref[...]
        dv = rstd * (dvh - jnp.mean(dvh, axis=-1, keepdims=True) - vhat * jnp.mean(dvh * vhat, axis=-1, keepdims=True))
        dp_ref[1] = (dv * _gelu_grad(pv)).astype(dp_ref.dtype)

    vec = pl.BlockSpec((1, c), lambda i: (0, 0))
    wspec = pl.BlockSpec((g_n, CHUNK, CHUNK), lambda i: (0, 0, 0))
    bspec = pl.BlockSpec((g_n, CHUNK, 1), lambda i: (0, 0, 0))
    pspec = pl.BlockSpec((2, CHUNK, c), lambda i: (0, i, 0))
    return pl.pallas_call(
        body, grid=(s // CHUNK,), name=name,
        in_specs=[pspec, pl.BlockSpec((CHUNK, c), lambda i: (i, 0)), vec, vec, wspec, wspec, bspec],
        out_specs=(pspec, vec, vec, wspec, bspec),
        out_shape=(jax.ShapeDtypeStruct((2, s, c), BF16), jax.ShapeDtypeStruct((1, c), F32),
                   jax.ShapeDtypeStruct((1, c), F32), jax.ShapeDtypeStruct((g_n, CHUNK, CHUNK), F32),
                   jax.ShapeDtypeStruct((g_n, CHUNK, 1), F32)),
        scratch_shapes=[pltpu.VMEM((CHUNK, c), F32)],
        compiler_params=_params("arbitrary"),
    )(p, dy, ln_g, ln_b, ws, ws_t, bs)


SB_Q = 512


def _sb_cumsum(x, tri, edge, reverse):
    n = x.shape[0]
    hi = x.astype(BF16)
    lo = (x - hi.astype(F32)).astype(BF16)
    chunks = x.shape[1] // CHUNK
    outs = [None] * chunks
    for c in (reversed(range(chunks)) if reverse else range(chunks)):
        cols = slice(c * CHUNK, (c + 1) * CHUNK)
        both = jnp.dot(jnp.concatenate([hi[:, cols], lo[:, cols]], axis=0), tri, preferred_element_type=F32)
        outs[c] = both[:n] + both[n:] + edge
        edge = edge + jnp.sum(x[:, cols], axis=1, keepdims=True)
    return jnp.concatenate(outs, axis=1), edge


def _sb_block(q, k, offset):
    scale = HEAD_DIM ** -0.5
    z = lax.dot_general(q, k, (((1,), (1,)), ((), ())), preferred_element_type=F32) * scale
    e = jnp.exp(-jnp.abs(z))
    lp = jnp.log(1.0 + e)
    lb = jnp.minimum(z, 0.0) - lp
    l1 = lb - z
    if offset is not None:
        t_i = lax.broadcasted_iota(jnp.int32, z.shape, 0)
        s_i = lax.broadcasted_iota(jnp.int32, z.shape, 1)
        mask = (t_i - s_i) > offset
        l1 = jnp.where(mask, l1, 0.0)
    else:
        mask = None
    return z, e, lb, l1, mask


def _tri(cond):
    j = lax.broadcasted_iota(jnp.int32, (CHUNK, CHUNK), 0)
    s = lax.broadcasted_iota(jnp.int32, (CHUNK, CHUNK), 1)
    return jnp.where(cond(j, s), 1.0, 0.0).astype(BF16)


def _sb_fwd(qkv, name):
    _, s, c = qkv.shape
    heads = c // HEAD_DIM
    bq = _tile(s, (SB_Q, 256, CHUNK))
    nq = s // bq

    def body(q_ref, k_ref, v_ref, o_ref, runs_ref):
        i = pl.program_id(1)
        q = q_ref[...]
        after = _tri(lambda j, s_: j > s_)
        runs_ref[...] = jnp.zeros_like(runs_ref)

        def tile(g, carry, offset):
            o_acc, run = carry
            rows = pl.ds(pl.multiple_of(g * bq, bq), bq)
            runs_ref[pl.ds(pl.multiple_of(g * 8, 8), 8), :] = jnp.transpose(jnp.broadcast_to(run, (bq, LANES)))[:8]
            _, _, lb, l1, mask = _sb_block(q, k_ref[rows, :], offset)
            acc, run = _sb_cumsum(l1, after, run, True)
            a = jnp.exp(lb + acc)
            if mask is not None:
                a = jnp.where(mask, a, 0.0)
            o_acc = o_acc + jnp.dot(a.astype(BF16), v_ref[rows, :], preferred_element_type=F32)
            return o_acc, run

        carry = tile(i, (jnp.zeros((bq, HEAD_DIM), F32), jnp.zeros((bq, 1), F32)), 0)
        o_acc, _ = lax.fori_loop(0, i, lambda n, cr: tile(i - 1 - n, cr, None), carry)
        o_ref[...] = o_acc.astype(o_ref.dtype)

    return pl.pallas_call(
        body, grid=(heads, nq), name=name,
        in_specs=[pl.BlockSpec((None, bq, HEAD_DIM), lambda h, i: (0, i, h)),
                  pl.BlockSpec((None, s, HEAD_DIM), lambda h, i: (1, 0, h)),
                  pl.BlockSpec((None, s, HEAD_DIM), lambda h, i: (2, 0, h))],
        out_specs=(pl.BlockSpec((bq, HEAD_DIM), lambda h, i: (i, h)),
                   pl.BlockSpec((None, None, 8 * nq, bq), lambda h, i: (h, i, 0, 0))),
        out_shape=(jax.ShapeDtypeStruct((s, c), BF16), jax.ShapeDtypeStruct((heads, nq, 8 * nq, bq), F32)),
        compiler_params=_params("parallel", "arbitrary"),
    )(qkv, qkv, qkv)


def _sb_bwd(qkv, do, runs, name):
    _, s, c = qkv.shape
    heads = c // HEAD_DIM
    bq = _tile(s, (SB_Q, 256, CHUNK))
    nq = s // bq
    scale = HEAD_DIM ** -0.5
    tn_dims = (((0,), (0,)), ((), ()))

    def body(q_ref, k_ref, v_ref, do_ref, runs_ref, dq_ref, dkv_ref, dk_acc, dv_acc):
        i = pl.program_id(1)
        q = q_ref[...]
        dob = do_ref[...]
        after = _tri(lambda j, s_: j > s_)
        before = _tri(lambda j, s_: j < s_)

        @pl.when(i == 0)
        def _():
            dk_acc[...] = jnp.zeros_like(dk_acc)
            dv_acc[...] = jnp.zeros_like(dv_acc)

        def main(g, carry, offset):
            dq_acc, csum = carry
            rows = pl.ds(pl.multiple_of(g * bq, bq), bq)
            kblk = k_ref[rows, :]
            vblk = v_ref[rows, :]
            z, e, lb, l1, mask = _sb_block(q, kblk, offset)
            run = jnp.transpose(jnp.broadcast_to(runs_ref[pl.ds(pl.multiple_of(g * 8, 8), 1), :], (LANES, bq)))
            acc, _ = _sb_cumsum(l1, after, run, True)
            a = jnp.exp(lb + acc)
            if mask is not None:
                a = jnp.where(mask, a, 0.0)
            d_a = lax.dot_general(dob, vblk, (((1,), (1,)), ((), ())), preferred_element_type=F32)
            ee = a * d_a
            cc, csum = _sb_cumsum(ee, before, csum, False)
            beta = jnp.exp(lb)
            dz = (ee - (ee + cc) * beta) * scale
            if mask is not None:
                dz = jnp.where(mask, dz, 0.0)
            dzb = dz.astype(BF16)
            dq_acc = dq_acc + jnp.dot(dzb, kblk, preferred_element_type=F32)
            dk_acc[rows, :] += lax.dot_general(dzb, q, tn_dims, preferred_element_type=F32)
            dv_acc[rows, :] += lax.dot_general(a.astype(BF16), dob, tn_dims, preferred_element_type=F32)
            return dq_acc, csum

        carry = (jnp.zeros((bq, HEAD_DIM), F32), jnp.zeros((bq, 1), F32))
        carry = lax.fori_loop(0, i, lambda g, cr: main(g, cr, None), carry)
        dq_acc, _ = main(i, carry, 0)
        dq_ref[...] = dq_acc.astype(dq_ref.dtype)

        @pl.when(i == nq - 1)
        def _():
            dkv_ref[0] = dk_acc[...].astype(dkv_ref.dtype)
            dkv_ref[1] = dv_acc[...].astype(dkv_ref.dtype)

    blk = pl.BlockSpec((bq, HEAD_DIM), lambda h, i: (i, h))
    return pl.pallas_call(
        body, grid=(heads, nq), name=name,
        in_specs=[pl.BlockSpec((None, bq, HEAD_DIM), lambda h, i: (0, i, h)),
                  pl.BlockSpec((None, s, HEAD_DIM), lambda h, i: (1, 0, h)),
                  pl.BlockSpec((None, s, HEAD_DIM), lambda h, i: (2, 0, h)),
                  blk,
                  pl.BlockSpec((None, None, 8 * nq, bq), lambda h, i: (h, i, 0, 0))],
        out_specs=(blk, pl.BlockSpec((2, s, HEAD_DIM), lambda h, i: (0, 0, h))),
        out_shape=(jax.ShapeDtypeStruct((s, c), BF16), jax.ShapeDtypeStruct((2, s, c), BF16)),
        scratch_shapes=[pltpu.VMEM((s, HEAD_DIM), F32), pltpu.VMEM((s, HEAD_DIM), F32)],
        compiler_params=_params("parallel", "arbitrary"),
    )(qkv, qkv, qkv, do, runs)


def _all_gather(x, axis, name):
    return _run_jobs([_GatherJob(x, axis)], name)[0]


def _rs_core_exchange(g, axis, name):
    return _run_jobs([_CoreExchangeJob(g, axis)], name)[0]


def _ew_tiles(r, c):
    tc = c if c <= 2048 else _tile(c, (2048, 1024, 512, 256, 128))
    return _tile(r, [p for p in (512, 256, 128, 64, 32, 16, 8) if p * tc <= 256 * 1024]), tc


def _rs_core_add(g, recv, core, axis, name):
    _, r, c = recv.shape
    tr, tc = _ew_tiles(r, c)
    nrb, ncb = r // tr, c // tc

    def body(core_ref, g_ref, x_ref, o_ref):
        del core_ref
        o_ref[...] = (g_ref[...].astype(F32) + x_ref[...].astype(F32)).astype(o_ref.dtype)

    if axis == 0:
        g_spec = pl.BlockSpec((tr, tc), lambda k, i, j, core: ((2 * k + core[0]) * nrb + i, j))
    else:
        g_spec = pl.BlockSpec((tr, tc), lambda k, i, j, core: (i, (2 * k + core[0]) * ncb + j))
    blk = pl.BlockSpec((None, tr, tc), lambda k, i, j, core: (k, i, j))
    return pl.pallas_call(
        body, name=name,
        grid_spec=pltpu.PrefetchScalarGridSpec(
            num_scalar_prefetch=1, grid=(4, nrb, ncb), in_specs=[g_spec, blk], out_specs=blk),
        out_shape=jax.ShapeDtypeStruct(recv.shape, BF16),
        compiler_params=_params("parallel", "parallel", "parallel"),
    )(core, g, recv)


def _rs_chip_exchange(parts, name):
    return _run_jobs([_ChipExchangeJob(p) for p in parts], name)


def _adamw_math(w, g, m, v):
    m = ADAM_B1 * m + (1.0 - ADAM_B1) * g
    v = ADAM_B2 * v + (1.0 - ADAM_B2) * (g * g)
    m_hat = m / (1.0 - ADAM_B1 ** ADAM_STEP)
    v_hat = v / (1.0 - ADAM_B2 ** ADAM_STEP)
    delta = -ADAM_LR * (m_hat / (jnp.sqrt(v_hat) + ADAM_EPS) + ADAM_WD * w)
    return delta, m, v


def _adamw_shard(w, m, v, layer, p, recv, chip, acc, name):
    n_layers, r, c = w.shape
    tr, tc = _ew_tiles(r, c)
    if n_layers == 1:
        acc = ()
    elif acc is None:
        acc = tuple(lax.empty((n_layers, r, c), F32) for _ in range(4))

    def body(chip_ref, w_ref, m_ref, v_ref, p_ref, x_ref, *rest):
        del chip_ref
        g_out, d_out, m_out, v_out = rest[len(acc):]
        g = p_ref[...].astype(F32)
        for j in range(3):
            g = g + x_ref[j].astype(F32)
        d, mn, vn = _adamw_math(w_ref[...], g, m_ref[...], v_ref[...])
        g_out[...] = g
        d_out[...] = d
        m_out[...] = mn
        v_out[...] = vn

    lay = pl.BlockSpec((None, tr, tc), lambda i, j, chip: (layer, i, j))
    out = jax.ShapeDtypeStruct((n_layers, r, c), F32)
    return pl.pallas_call(
        body, name=name,
        grid_spec=pltpu.PrefetchScalarGridSpec(
            num_scalar_prefetch=1, grid=(r // tr, c // tc),
            in_specs=[lay, lay, lay, pl.BlockSpec((None, tr, tc), lambda i, j, chip: (chip[0], i, j)),
                      pl.BlockSpec((3, tr, tc), lambda i, j, chip: (0, i, j))] + [ANY] * len(acc),
            out_specs=(lay, lay, lay, lay)),
        out_shape=(out, out, out, out),
        input_output_aliases={6 + t: t for t in range(len(acc))},
        compiler_params=_params("parallel", "parallel"),
    )(chip, w, m, v, p, recv, *acc)


def _adamw_packed(w, m, v, g, name):
    r, c = w.shape
    tr = _tile(r, (512, 256, 128, 64, 32, 16, 8))

    def body(w_ref, m_ref, v_ref, g_ref, d_out, m_out, v_out):
        d, mn, vn = _adamw_math(w_ref[...], g_ref[...], m_ref[...], v_ref[...])
        d_out[...] = d
        m_out[...] = mn
        v_out[...] = vn

    blk = pl.BlockSpec((tr, c), lambda i: (i, 0))
    out = jax.ShapeDtypeStruct((r, c), F32)
    return pl.pallas_call(
        body, grid=(r // tr,), name=name, in_specs=[blk, blk, blk, blk], out_specs=(blk, blk, blk),
        out_shape=(out, out, out), compiler_params=_params("parallel"),
    )(w, m, v, g)


def _sum_devices(x, name):
    r = x.shape[0] // N_DEV
    tr = _tile(r, (512, 256, 128, 64, 32, 16, 8))

    def body(x_ref, o_ref):
        acc = x_ref[0]
        for d in range(1, N_DEV):
            acc = acc + x_ref[d]
        o_ref[...] = acc

    return pl.pallas_call(
        body, grid=(r // tr,), name=name,
        in_specs=[pl.BlockSpec((N_DEV, tr, LANES), lambda i: (0, i, 0))],
        out_specs=pl.BlockSpec((tr, LANES), lambda i: (i, 0)),
        out_shape=jax.ShapeDtypeStruct((r, LANES), F32),
        compiler_params=_params("parallel"),
    )(x.reshape(N_DEV, r, LANES))


def _pack(arrays):
    flat = []
    for a in arrays:
        f = a.reshape(-1).astype(F32)
        flat.append(jnp.pad(f, (0, (-f.shape[0]) % LANES)))
    f = jnp.concatenate(flat)
    f = jnp.pad(f, (0, (-f.shape[0]) % (8 * LANES)))
    return f.reshape(-1, LANES)


def _unpack(packed, shapes):
    flat = packed.reshape(-1)
    out, off = [], 0
    for shp in shapes:
        n = 1
        for d in shp:
            n *= d
        out.append(flat[off:off + n].reshape(shp))
        off += n + (-n) % LANES
    return out


def kernel(x, norm_mix_pre, norm_mix_post, norm_ffn_pre, norm_ffn_post, sc_w_in, sc_conv_w, sc_w_out, sg_w_in, sg_ln_g, sg_ln_b, sg_w_s, sg_b_s, sg_w_out, sb_w_qkv, sb_w_out, ffn_w_up, ffn_conv_w, ffn_conv_b, ffn_w_down, loss_target, m_norm_mix_pre, m_norm_mix_post, m_norm_ffn_pre, m_norm_ffn_post, m_sc_w_in, m_sc_conv_w, m_sc_w_out, m_sg_w_in, m_sg_ln_g, m_sg_ln_b, m_sg_w_s, m_sg_b_s, m_sg_w_out, m_sb_w_qkv, m_sb_w_out, m_ffn_w_up, m_ffn_conv_w, m_ffn_conv_b, m_ffn_w_down, v_norm_mix_pre, v_norm_mix_post, v_norm_ffn_pre, v_norm_ffn_post, v_sc_w_in, v_sc_conv_w, v_sc_w_out, v_sg_w_in, v_sg_ln_g, v_sg_ln_b, v_sg_w_s, v_sg_b_s, v_sg_w_out, v_sb_w_qkv, v_sb_w_out, v_ffn_w_up, v_ffn_conv_w, v_ffn_conv_b, v_ffn_w_down):
    weights = dict(norm_mix_pre=norm_mix_pre, norm_mix_post=norm_mix_post, norm_ffn_pre=norm_ffn_pre,
                   norm_ffn_post=norm_ffn_post, sc_w_in=sc_w_in, sc_conv_w=sc_conv_w, sc_w_out=sc_w_out,
                   sg_w_in=sg_w_in, sg_ln_g=sg_ln_g, sg_ln_b=sg_ln_b, sg_w_s=sg_w_s, sg_b_s=sg_b_s,
                   sg_w_out=sg_w_out, sb_w_qkv=sb_w_qkv, sb_w_out=sb_w_out, ffn_w_up=ffn_w_up,
                   ffn_conv_w=ffn_conv_w, ffn_conv_b=ffn_conv_b, ffn_w_down=ffn_w_down)
    mom1 = dict(norm_mix_pre=m_norm_mix_pre, norm_mix_post=m_norm_mix_post, norm_ffn_pre=m_norm_ffn_pre,
                norm_ffn_post=m_norm_ffn_post, sc_w_in=m_sc_w_in, sc_conv_w=m_sc_conv_w, sc_w_out=m_sc_w_out,
                sg_w_in=m_sg_w_in, sg_ln_g=m_sg_ln_g, sg_ln_b=m_sg_ln_b, sg_w_s=m_sg_w_s, sg_b_s=m_sg_b_s,
                sg_w_out=m_sg_w_out, sb_w_qkv=m_sb_w_qkv, sb_w_out=m_sb_w_out, ffn_w_up=m_ffn_w_up,
                ffn_conv_w=m_ffn_conv_w, ffn_conv_b=m_ffn_conv_b, ffn_w_down=m_ffn_w_down)
    mom2 = dict(norm_mix_pre=v_norm_mix_pre, norm_mix_post=v_norm_mix_post, norm_ffn_pre=v_norm_ffn_pre,
                norm_ffn_post=v_norm_ffn_post, sc_w_in=v_sc_w_in, sc_conv_w=v_sc_conv_w, sc_w_out=v_sc_w_out,
                sg_w_in=v_sg_w_in, sg_ln_g=v_sg_ln_g, sg_ln_b=v_sg_ln_b, sg_w_s=v_sg_w_s, sg_b_s=v_sg_b_s,
                sg_w_out=v_sg_w_out, sb_w_qkv=v_sb_w_qkv, sb_w_out=v_sb_w_out, ffn_w_up=v_ffn_w_up,
                ffn_conv_w=v_ffn_conv_w, ffn_conv_b=v_ffn_conv_b, ffn_w_down=v_ffn_w_down)
    order = list(weights)

    mx, my, mc = lax.axis_index("x"), lax.axis_index("y"), lax.axis_index("c")
    dev = 4 * mx + 2 * my + mc
    core_arr = jnp.reshape(mc, (1,)).astype(jnp.int32)
    chip_arr = jnp.reshape(2 * mx + my, (1,)).astype(jnp.int32)

    h = x[0]
    target = loss_target[0]
    depth = norm_mix_pre.shape[0]
    d_model = h.shape[1]
    f_loc = ffn_w_down.shape[1]
    d_ff = f_loc * N_DEV

    def gather_rows8(a2d, name):
        rows = a2d.shape[0]
        pad = (-rows) % 8
        return _all_gather(jnp.pad(a2d, ((0, pad), (0, 0))), 1, name)[:rows]

    n_a = sc_w_in.shape[0]
    sc_conv_full = gather_rows8(sc_conv_w.reshape(n_a * 3, -1), "ag_sc_conv").reshape(n_a, 3, d_model)
    ffn_conv_full = gather_rows8(ffn_conv_w.reshape(depth * 3, -1), "ag_ffn_conv").reshape(depth, 3, 2 * d_ff)

    big = {}

    kinds = [i % 3 for i in range(depth)]
    mixer_names = [("sc_w_in", "sc_w_out"), ("sg_w_in", "sg_w_out"), ("sb_w_qkv", "sb_w_out")]
    mixer_split = [3, 2, 3]
    mixer_proj = ["sc_in_proj", "sg_in_proj", "sb_qkv_proj"]

    def layer_keys(i):
        wi, wo = mixer_names[kinds[i]]
        return (wi, i // 3, 1), (wo, i // 3, 0), ("ffn_w_up", i, 1), ("ffn_w_down", i, 0)

    def gather_job(key):
        name, j, axis = key
        return _GatherJob(weights[name][j].astype(BF16), axis)

    def hosted_mm(a, key, out_dtype, nsplit, name, gather_keys):
        keys = [k for k in gather_keys if k is not None]
        res = _mm_nn(a, big[key[:2]], out_dtype, nsplit, name, jobs=[gather_job(k) for k in keys])
        if not keys:
            return res
        for k, full in zip(keys, res[1:]):
            big[k[:2]] = full
        return res[0]

    sg_ws_t = jnp.swapaxes(sg_w_s, -1, -2)
    sg_bs3 = sg_b_s[..., None]

    k_in0 = layer_keys(0)[0]
    big[k_in0[:2]] = _all_gather(weights[k_in0[0]][k_in0[1]].astype(BF16), k_in0[2], f"ag_{k_in0[0]}")
    saved = []
    hn = _rms_fwd(h, norm_mix_pre[0:1], "rms_fwd")
    for i in range(depth):
        j = i // 3
        k_in, k_out, k_up, k_down = layer_keys(i)
        nxt = layer_keys(i + 1) if i + 1 < depth else (None, None, None, None)
        sv = {"h_in": h, "hn": hn}
        p = hosted_mm(hn, k_in, BF16, mixer_split[kinds[i]], mixer_proj[kinds[i]],
                      [k_out, k_up] if i == 0 else [])
        if kinds[i] == 0:
            y = _sc_gate_fwd(p, sc_conv_full[j], "sc_gate_fwd")
        elif kinds[i] == 1:
            y = _sgu_fwd(p, sg_ln_g[j:j + 1], sg_ln_b[j:j + 1], sg_w_s[j], sg_bs3[j], "sgu_fwd")
        else:
            y, sv["runs"] = _sb_fwd(p, "sb_fwd")
        m = hosted_mm(y, k_out, F32, 1, "mix_out_proj", [])
        sv.update(p=p, y=y, m=m)
        h, hn2 = _resid_rms_fwd(h, m, norm_mix_post[i:i + 1], norm_ffn_pre[i:i + 1], "resid_rms_fwd")
        sv["h_mid"] = h
        u = hosted_mm(hn2, k_up, BF16, 2, "ffn_up_proj", [k_down, nxt[0], nxt[1]])
        cw = ffn_conv_full[i].reshape(3, 2, d_ff).transpose(1, 0, 2)
        cb = ffn_conv_b[i].reshape(2, 1, d_ff)
        a = _ffn_act_fwd(u, cw, cb, "ffn_act_fwd")
        f = hosted_mm(a, k_down, F32, 1, "ffn_down_proj", [nxt[2]])
        sv.update(hn2=hn2, u=u, a=a, f=f, cw=cw, cb=cb)
        h, hn = _resid_rms_fwd(h, f, norm_ffn_post[i:i + 1], norm_mix_pre[i + 1:i + 2] if i + 1 < depth else None,
                               "resid_rms_fwd")
        saved.append(sv)

    dh, loss_part = _loss_head(h, target, "loss_head")

    grads_big = {}
    small = {}

    parts = {}
    queue = []

    stage1 = []

    def reduce_scatter(name, j, g_full, axis):
        stage1.append(((name, j), g_full, axis))

    def after_stage1(key, g_full, axis, recv1):
        parts[key] = _rs_core_add(g_full, recv1, core_arr, axis, f"rs_add_{key[0]}")
        queue.append(key)

    def hosting(mm, a, b, out_dtype, name, n_jobs):
        keys = [queue.pop(0) for _ in range(min(n_jobs, len(queue)))]
        first = list(stage1)
        del stage1[:]
        jobs = [_ChipExchangeJob(parts[k]) for k in keys] + [_CoreExchangeJob(g, ax) for _, g, ax in first]
        res = mm(a, b, out_dtype, name, jobs=jobs)
        if not jobs:
            return res
        for k, recv in zip(keys, res[1:1 + len(keys)]):
            grads_big[k] = (parts[k], recv)
        for (k, g, ax), recv1 in zip(first, res[1 + len(keys):]):
            after_stage1(k, g, ax, recv1)
        return res[0]

    for i in reversed(range(depth)):
        j = i // 3
        sv = saved[i]
        df, dg = _rms_bwd(sv["f"], norm_ffn_post[i:i + 1], dh, None, BF16, "rms_bwd_post")
        small.setdefault("norm_ffn_post", {})[i] = dg
        da = hosting(_mm_nt, df, big[("ffn_w_down", i)], BF16, "ffn_down_dx", 1)
        reduce_scatter("ffn_w_down", i, hosting(_mm_tn, sv["a"], df, BF16, "ffn_down_dw", 2), 0)
        du, dcw, dcb = _ffn_act_bwd(sv["u"], da, sv["cw"], sv["cb"], "ffn_act_bwd")
        small.setdefault("ffn_conv_w", {})[i] = dcw.transpose(1, 0, 2).reshape(3, 2 * d_ff)
        small.setdefault("ffn_conv_b", {})[i] = dcb.reshape(2 * d_ff)
        dhn2 = _mm_nt(du, big[("ffn_w_up", i)], F32, "ffn_up_dx")
        reduce_scatter("ffn_w_up", i, _mm_tn(sv["hn2"], du, BF16, "ffn_up_dw"), 1)
        dh, dg = _rms_bwd(sv["h_mid"], norm_ffn_pre[i:i + 1], dhn2, dh, F32, "rms_bwd_pre")
        small.setdefault("norm_ffn_pre", {})[i] = dg
        dm, dg = _rms_bwd(sv["m"], norm_mix_post[i:i + 1], dh, None, BF16, "rms_bwd_post")
        small.setdefault("norm_mix_post", {})[i] = dg
        if kinds[i] == 0:
            wo, wi = "sc_w_out", "sc_w_in"
        elif kinds[i] == 1:
            wo, wi = "sg_w_out", "sg_w_in"
        else:
            wo, wi = "sb_w_out", "sb_w_qkv"
        dy = hosting(_mm_nt, dm, big[(wo, j)], BF16, "mix_out_dx", 1)
        reduce_scatter(wo, j, hosting(_mm_tn, sv["y"], dm, BF16, "mix_out_dw", 1), 0)
        if kinds[i] == 0:
            dp, dcw = _sc_gate_bwd(sv["p"], dy, sc_conv_full[j], "sc_gate_bwd")
            small.setdefault("sc_conv_w", {})[j] = dcw
        elif kinds[i] == 1:
            dp, dlg, dlb, dws, dbs = _sgu_bwd(sv["p"], dy, sg_ln_g[j:j + 1], sg_ln_b[j:j + 1], sg_w_s[j], sg_ws_t[j],
                                              sg_bs3[j], "sgu_bwd")
            small.setdefault("sg_ln_g", {})[j] = dlg[0]
            small.setdefault("sg_ln_b", {})[j] = dlb[0]
            small.setdefault("sg_w_s", {})[j] = dws
            small.setdefault("sg_b_s", {})[j] = dbs[..., 0]
        else:
            dq, dkv = _sb_bwd(sv["p"], dy, sv["runs"], "sb_bwd")
            dp = jnp.concatenate([dq[None], dkv], axis=0)
        dhn = hosting(_mm_nt, dp, big[(wi, j)], F32, "mix_in_dx", 2)
        reduce_scatter(wi, j, _mm_tn(sv["hn"], dp, BF16, "mix_in_dw"), 1)
        dh, dg = _rms_bwd(sv["h_in"], norm_mix_pre[i:i + 1], dhn, dh, F32, "rms_bwd_pre")
        small.setdefault("norm_mix_pre", {})[i] = dg

    grad_x = dh[None]
    for k, g_full, ax in stage1:
        after_stage1(k, g_full, ax, _rs_core_exchange(g_full, ax, "rs1_tail"))
    for k, recv in zip(queue, _rs_chip_exchange([parts[k] for k in queue], "rs2_tail")):
        grads_big[k] = (parts[k], recv)

    small_names = ["norm_mix_pre", "norm_mix_post", "norm_ffn_pre", "norm_ffn_post", "sg_ln_g", "sg_ln_b", "sg_w_s",
                   "sg_b_s", "ffn_conv_b", "sc_conv_w", "ffn_conv_w"]
    full_shapes = {n: weights[n].shape for n in small_names}
    full_shapes["sc_conv_w"] = (n_a, 3, d_model)
    full_shapes["ffn_conv_w"] = (depth, 3, 2 * d_ff)
    partial = [jnp.stack([small[n][k].reshape(full_shapes[n][1:]) for k in sorted(small[n])]) for n in small_names]
    packed = _pack(partial + [loss_part])
    gathered = _all_gather(packed, 0, "ag_small_grads")
    summed = _sum_devices(gathered, "sum_small_grads")
    pieces = _unpack(summed, [full_shapes[n] for n in small_names] + [(1, 1)])
    loss = pieces[-1].reshape(())
    small_grads = dict(zip(small_names, pieces[:-1]))
    for n in ("sc_conv_w", "ffn_conv_w"):
        c_loc = weights[n].shape[-1]
        small_grads[n] = lax.dynamic_slice_in_dim(small_grads[n], dev * c_loc, c_loc, axis=2)
    shapes = [weights[n].shape for n in small_names]
    d_p, m_p, v_p = _adamw_packed(_pack([weights[n] for n in small_names]), _pack([mom1[n] for n in small_names]),
                                  _pack([mom2[n] for n in small_names]), _pack([small_grads[n] for n in small_names]),
                                  "adamw_small")
    out_g, out_d, out_m, out_v = dict(small_grads), {}, {}, {}
    for n, d_, m_, v_ in zip(small_names, _unpack(d_p, shapes), _unpack(m_p, shapes), _unpack(v_p, shapes)):
        out_d[n], out_m[n], out_v[n] = d_, m_, v_

    for n in order:
        if n in small_names:
            continue
        acc = None
        for j in range(weights[n].shape[0]):
            acc = _adamw_shard(weights[n], mom1[n], mom2[n], j, *grads_big[(n, j)], chip_arr, acc, f"adamw_{n}")
        out_g[n], out_d[n], out_m[n], out_v[n] = acc

    return (loss, grad_x, *[out_g[n] for n in order], *[out_d[n] for n in order],
            *[out_m[n] for n in order], *[out_v[n] for n in order])
```
